```python
import math
import jax, jax.numpy as jnp
from jax import lax
import numpy as np

D_MODEL = 1024
BATCH = 8
SEQ = 2048
DEPTH = 2
DEC_BATCH = 128
DEC_SEQ = 8
PAST_LEN = 16384
PAGE_SIZE = 128

POOL_WIDTH = D_MODEL // 2
POOL_WINDOWS = (2, 4, 8, 16)
POOL_GROUPS = len(POOL_WINDOWS)
POOL_GROUP_WIDTH = POOL_WIDTH // POOL_GROUPS
POOL_CTX = max(POOL_WINDOWS) - 1
RET_WIDTH = D_MODEL - POOL_WIDTH
RET_HEADS = 4
RET_HEAD_DIM = RET_WIDTH // RET_HEADS
RET_CHUNK = 128
ROPE_BASE = 10000.0
MIX_WIDTH = POOL_WIDTH + RET_WIDTH
IN_WIDTH = POOL_WIDTH + 4 * RET_WIDTH
D_FF = 4 * D_MODEL
RMS_EPS = 1e-6
GN_EPS = 1e-5

kernel_name = "hybrid_pool_retention_decoder_step"


def _rmsnorm(x, g):
    xf = x.astype(jnp.float32)
    y = xf * lax.rsqrt(jnp.mean(xf * xf, axis=-1, keepdims=True) + RMS_EPS)
    return (y * g.astype(jnp.float32)).astype(x.dtype)


def _ret_gamma():
    return 1.0 - jnp.exp2(-5.0 - jnp.arange(RET_HEADS, dtype=jnp.float32))


def _rotary(x, pos):
    half = x.shape[-1] // 2
    inv = ROPE_BASE ** (-jnp.arange(half, dtype=jnp.float32) / half)
    ang = pos.astype(jnp.float32)[:, None] * inv[None, :]
    cos = jnp.cos(ang)[None, :, None, :]
    sin = jnp.sin(ang)[None, :, None, :]
    x1, x2 = x[..., :half], x[..., half:]
    return jnp.concatenate([x1 * cos - x2 * sin, x1 * sin + x2 * cos], axis=-1)


def _pool_mixer(u, ctx, pos, w, scale):
    B, T, _ = u.shape
    ue = jnp.concatenate([ctx.astype(u.dtype), u], axis=1)
    cs = jnp.cumsum(ue.astype(jnp.float32), axis=1)
    cs = jnp.concatenate([jnp.zeros((B, 1, POOL_WIDTH), jnp.float32), cs], axis=1)
    outs = []
    for gi, win in enumerate(POOL_WINDOWS):
        sl = slice(gi * POOL_GROUP_WIDTH, (gi + 1) * POOL_GROUP_WIDTH)
        s = (cs[:, POOL_CTX + 1:POOL_CTX + 1 + T, sl]
             - cs[:, POOL_CTX + 1 - win:POOL_CTX + 1 - win + T, sl])
        cnt = jnp.minimum(pos + 1, win).astype(jnp.float32)[None, :, None]
        outs.append(s / cnt)
    pooled = jnp.concatenate(outs, axis=-1) - u.astype(jnp.float32)
    y = jnp.einsum('btgc,gcd->btgd',
                   pooled.reshape(B, T, POOL_GROUPS, POOL_GROUP_WIDTH),
                   w.astype(jnp.float32)).reshape(B, T, POOL_WIDTH)
    y = y * scale.astype(jnp.float32)
    return y, ue[:, -POOL_CTX:]


def _retention(q, k, v, s0, chunk):
    B, H, T, D = q.shape
    n = T // chunk
    lg = jnp.log(_ret_gamma())
    i = jnp.arange(chunk, dtype=jnp.float32)
    diff = i[:, None] - i[None, :]
    dmask = jnp.where(diff >= 0, jnp.exp(lg[:, None, None] * jnp.maximum(diff, 0.0)), 0.0)
    q_decay = jnp.exp(lg[:, None] * (i + 1.0))[..., None]
    k_decay = jnp.exp(lg[:, None] * (chunk - 1.0 - i))[..., None]
    c_decay = jnp.exp(lg * chunk)[:, None, None]

    def to_chunks(a):
        return a.reshape(B, H, n, chunk, a.shape[-1]).transpose(2, 0, 1, 3, 4)

    def step(S, xs):
        qc, kc, vc = xs
        sc = jnp.einsum('bhid,bhjd->bhij', qc, kc) * dmask
        o = (jnp.einsum('bhij,bhje->bhie', sc, vc)
             + jnp.einsum('bhid,bhde->bhie', qc * q_decay, S))
        S = S * c_decay + jnp.einsum('bhjd,bhje->bhde', kc * k_decay, vc)
        return S, o

    S, o = lax.scan(step, s0, (to_chunks(q), to_chunks(k), to_chunks(v)))
    o = o.transpose(1, 2, 0, 3, 4).reshape(B, H, T, D)
    return o, S


def _layer(x, pool_ctx, s0, pos0, g1, w_in, pool_w, pool_scale, gn_g, gn_b, w_out, g2, w_up, w_down):
    B, T, _ = x.shape
    h = _rmsnorm(x, g1)
    z = h @ w_in
    P, R = POOL_WIDTH, RET_WIDTH
    u, q, k, v, g = jnp.split(z, [P, P + R, P + 2 * R, P + 3 * R], axis=-1)
    pos = pos0 + jnp.arange(T, dtype=jnp.int32)

    pool_out, new_ctx = _pool_mixer(u, pool_ctx, pos, pool_w, pool_scale)

    def heads(a):
        return a.astype(jnp.float32).reshape(B, T, RET_HEADS, RET_HEAD_DIM)
    qh = _rotary(heads(q), pos).transpose(0, 2, 1, 3)
    kh = (_rotary(heads(k), pos) * (RET_HEAD_DIM ** -0.5)).transpose(0, 2, 1, 3)
    vh = heads(v).transpose(0, 2, 1, 3)
    chunk = RET_CHUNK if T % RET_CHUNK == 0 else T
    o, S = _retention(qh, kh, vh, s0.astype(jnp.float32), chunk)
    mu = jnp.mean(o, axis=-1, keepdims=True)
    var = jnp.mean(jnp.square(o - mu), axis=-1, keepdims=True)
    o = ((o - mu) * lax.rsqrt(var + GN_EPS)).transpose(0, 2, 1, 3).reshape(B, T, R)
    o = o * gn_g.astype(jnp.float32) + gn_b.astype(jnp.float32)
    ret_out = jax.nn.silu(g.astype(jnp.float32)) * o

    mix = jnp.concatenate([pool_out, ret_out], axis=-1).astype(x.dtype) @ w_out
    x = x + mix
    a = jnp.square(jax.nn.relu(_rmsnorm(x, g2) @ w_up))
    x = x + a @ w_down
    return x, new_ctx, S.astype(x.dtype)


def setup_inputs(seed: int = 0) -> dict:
    key = jax.random.key(seed)
    ks = jax.random.split(key, 18)
    f32 = jnp.float32
    nrm = lambda k, s: jax.random.normal(k, s, f32)
    return {
        "x_prompt": nrm(ks[0], (BATCH, SEQ, D_MODEL)),
        "x_sample": nrm(ks[1], (DEC_BATCH, DEC_SEQ, D_MODEL)),
        "state_pool": nrm(ks[2], (DEPTH, DEC_BATCH, POOL_CTX, POOL_WIDTH)),
        "state_ret": 0.5 * nrm(ks[3], (DEPTH, DEC_BATCH, RET_HEADS, RET_HEAD_DIM, RET_HEAD_DIM)),
        "norm1_g": 1.0 + 0.02 * nrm(ks[4], (DEPTH, D_MODEL)),
        "w_in": nrm(ks[5], (DEPTH, D_MODEL, IN_WIDTH)) * D_MODEL ** -0.5,
        "pool_w": nrm(ks[6], (DEPTH, POOL_GROUPS, POOL_GROUP_WIDTH, POOL_GROUP_WIDTH)) * POOL_GROUP_WIDTH ** -0.5,
        "pool_scale": 1.0 + 0.02 * nrm(ks[7], (DEPTH, POOL_WIDTH)),
        "gn_g": 1.0 + 0.02 * nrm(ks[8], (DEPTH, RET_WIDTH)),
        "gn_b": 0.02 * nrm(ks[9], (DEPTH, RET_WIDTH)),
        "w_out": nrm(ks[10], (DEPTH, MIX_WIDTH, D_MODEL)) * MIX_WIDTH ** -0.5,
        "norm2_g": 1.0 + 0.02 * nrm(ks[11], (DEPTH, D_MODEL)),
        "w_up": nrm(ks[12], (DEPTH, D_MODEL, D_FF)) * D_MODEL ** -0.5,
        "w_down": nrm(ks[13], (DEPTH, D_FF, D_MODEL)) * D_FF ** -0.5,
        "final_g": 1.0 + 0.02 * nrm(ks[14], (D_MODEL,)),
    }


def reference(x_prompt, x_sample, state_pool, state_ret, norm1_g, w_in, pool_w, pool_scale,
              gn_g, gn_b, w_out, norm2_g, w_up, w_down, final_g):
    yp, ys = x_prompt, x_sample
    bp = x_prompt.shape[0]
    pool_p, ret_p, pool_s, ret_s = [], [], [], []
    for l in range(DEPTH):
        params = (norm1_g[l], w_in[l], pool_w[l], pool_scale[l], gn_g[l], gn_b[l],
                  w_out[l], norm2_g[l], w_up[l], w_down[l])
        ctx0 = jnp.zeros((bp, POOL_CTX, POOL_WIDTH), x_prompt.dtype)
        s00 = jnp.zeros((bp, RET_HEADS, RET_HEAD_DIM, RET_HEAD_DIM), jnp.float32)
        yp, cp, sp = _layer(yp, ctx0, s00, 0, *params)
        ys, cs, ss = _layer(ys, state_pool[l], state_ret[l], PAST_LEN, *params)
        pool_p.append(cp)
        ret_p.append(sp)
        pool_s.append(cs)
        ret_s.append(ss)
    y_prompt = _rmsnorm(yp, final_g)
    y_sample = _rmsnorm(ys, final_g)
    return (y_prompt, y_sample, jnp.stack(pool_p), jnp.stack(ret_p), jnp.stack(pool_s), jnp.stack(ret_s))
```

```python
import functools

import jax
import jax.numpy as jnp
from jax import lax
from jax.experimental import pallas as pl
from jax.experimental.pallas import tpu as pltpu

D_MODEL = 1024
POOL_WIDTH = D_MODEL // 2
POOL_WINDOWS = (2, 4, 8, 16)
POOL_GROUP_WIDTH = POOL_WIDTH // len(POOL_WINDOWS)
POOL_CTX = max(POOL_WINDOWS) - 1
POOL_PAD = POOL_CTX + 1
RET_WIDTH = D_MODEL - POOL_WIDTH
RET_HEADS = 4
RET_HEAD_DIM = RET_WIDTH // RET_HEADS
RET_CHUNK = 128
ROPE_BASE = 10000.0
D_FF = 4 * D_MODEL
RMS_EPS = 1e-6
GN_EPS = 1e-5
PAST_LEN = 16384

SUBLANES = 8
VMEM_LIMIT_BYTES = 56 * 1024 * 1024

PROMPT_TILE = 512
SAMPLE_BATCH_TILE = 16
FFN_TILE = 512
FFN_CHUNK = 1024

F32 = jnp.float32
BF16 = jnp.bfloat16


def _dot(a, b):
    return jnp.dot(a, b, preferred_element_type=F32)


def _dot_nt(a, b):
    return lax.dot_general(a, b, (((1,), (1,)), ((), ())), preferred_element_type=F32)


def _dot_tn(a, b):
    return lax.dot_general(a, b, (((0,), (0,)), ((), ())), preferred_element_type=F32)


def _rmsnorm(x, g):
    ms = jnp.mean(x * x, axis=-1, keepdims=True)
    return x * lax.rsqrt(ms + RMS_EPS) * g


def _rotate(x, cos, sin_signed):
    return x * cos + pltpu.roll(x, RET_HEAD_DIM // 2, 1) * sin_signed


def _head_norm_gate(o, gate, gn_g, gn_b):
    mu = jnp.mean(o, axis=-1, keepdims=True)
    d = o - mu
    var = jnp.mean(d * d, axis=-1, keepdims=True)
    on = d * lax.rsqrt(var + GN_EPS) * gn_g + gn_b
    return gate * jax.nn.sigmoid(gate) * on


def _layer_spec(arr, layer):
    tail = (0,) * (arr.ndim - 1)
    return pl.BlockSpec((None,) + arr.shape[1:], lambda *_: (layer,) + tail)


def _project_in(x, g1_ref, win_ref):
    h = _rmsnorm(x, g1_ref[...]).astype(BF16)
    P, R = POOL_WIDTH, RET_WIDTH
    return [_dot(h, win_ref[:, lo:lo + w])
            for lo, w in ((0, P), (P, R), (P + R, R), (P + 2 * R, R), (P + 3 * R, R))]


def _prompt_mixer_kernel(x_ref, g1_ref, win_ref, poolw_ref, pscale_ref, gng_ref, gnb_ref,
                         wout_ref, cq_ref, sq_ref, ck_ref, sk_ref, qd_ref, kd_ref, cd_ref,
                         dmask_ref,
                         x1_ref, pool_out_ref, s_ref,
                         ubuf, q_scr, k_scr, v_scr, g_scr, mix_scr, *, tile):
    t = pl.program_id(1)

    @pl.when(t == 0)
    def _():
        ubuf[0:POOL_PAD, :] = jnp.zeros((POOL_PAD, POOL_WIDTH), F32)
        s_ref[...] = jnp.zeros_like(s_ref)

    x = x_ref[0]
    u, q, k, v, g = _project_in(x, g1_ref, win_ref)
    ubuf[POOL_PAD:POOL_PAD + tile, :] = u
    q_scr[...] = q
    k_scr[...] = k
    v_scr[...] = v
    g_scr[...] = g

    pos = t * tile + lax.broadcasted_iota(jnp.int32, (tile, 1), 0)
    for gi, win in enumerate(POOL_WINDOWS):
        sl = slice(gi * POOL_GROUP_WIDTH, (gi + 1) * POOL_GROUP_WIDTH)
        cur = ubuf[POOL_PAD:POOL_PAD + tile, sl]
        s = cur
        for j in range(1, win):
            s = s + ubuf[POOL_PAD - j:POOL_PAD - j + tile, sl]
        cnt = jnp.minimum(pos + 1, win).astype(F32)
        pooled = s / cnt - cur
        y = _dot(pooled.astype(BF16), poolw_ref[gi]) * pscale_ref[:, sl]
        mix_scr[:, sl] = y.astype(BF16)
    pool_out_ref[0] = ubuf[tile + 1:tile + POOL_PAD, :]
    ubuf[0:POOL_PAD, :] = ubuf[tile:tile + POOL_PAD, :]

    for c in range(tile // RET_CHUNK):
        rows = slice(c * RET_CHUNK, (c + 1) * RET_CHUNK)
        cq, sq = cq_ref[rows, :], sq_ref[rows, :]
        ck, sk = ck_ref[rows, :], sk_ref[rows, :]
        for hh in range(RET_HEADS):
            ls = slice(hh * RET_HEAD_DIM, (hh + 1) * RET_HEAD_DIM)
            qr = _rotate(q_scr[rows, ls], cq, sq)
            kr = _rotate(k_scr[rows, ls], ck, sk)
            vb = v_scr[rows, ls].astype(BF16)
            sc = _dot_nt(qr.astype(BF16), kr.astype(BF16)) * dmask_ref[hh]
            state = s_ref[0, hh]
            o = (_dot(sc.astype(BF16), vb)
                 + _dot((qr * qd_ref[:, ls]).astype(BF16), state.astype(BF16)))
            kdk = (kr * kd_ref[:, ls]).astype(BF16)
            s_ref[0, hh] = state * cd_ref[:, ls] + _dot_tn(kdk, vb)
            ret = _head_norm_gate(o, g_scr[rows, ls], gng_ref[:, ls], gnb_ref[:, ls])
            mix_scr[rows, POOL_WIDTH + hh * RET_HEAD_DIM:POOL_WIDTH + (hh + 1) * RET_HEAD_DIM] = (
                ret.astype(BF16))

    x1_ref[0] = x + _dot(mix_scr[...], wout_ref[...])


def _prompt_mixer(x, w, layer, tabs):
    B, T, D = x.shape
    tile = PROMPT_TILE
    assert T % tile == 0 and tile % RET_CHUNK == 0
    const2 = lambda b, t: (0, 0)
    const3 = lambda b, t: (0, 0, 0)
    tab_spec = pl.BlockSpec((tile, RET_HEAD_DIM), lambda b, t: (t, 0))
    in_specs = [
        pl.BlockSpec((1, tile, D), lambda b, t: (b, t, 0)),
        _layer_spec(w["g1"], layer),
        _layer_spec(w["w_in"], layer),
        _layer_spec(w["pool_w"], layer),
        _layer_spec(w["pool_scale"], layer),
        _layer_spec(w["gn_g"], layer),
        _layer_spec(w["gn_b"], layer),
        _layer_spec(w["w_out"], layer),
        tab_spec, tab_spec, tab_spec, tab_spec,
        pl.BlockSpec((RET_CHUNK, RET_WIDTH), const2),
        pl.BlockSpec((RET_CHUNK, RET_WIDTH), const2),
        pl.BlockSpec((1, RET_WIDTH), const2),
        pl.BlockSpec((RET_HEADS, RET_CHUNK, RET_CHUNK), const3),
    ]
    out_shape = (
        jax.ShapeDtypeStruct((B, T, D), F32),
        jax.ShapeDtypeStruct((B, POOL_CTX, POOL_WIDTH), F32),
        jax.ShapeDtypeStruct((B, RET_HEADS, RET_HEAD_DIM, RET_HEAD_DIM), F32),
    )
    out_specs = (
        pl.BlockSpec((1, tile, D), lambda b, t: (b, t, 0)),
        pl.BlockSpec((1, POOL_CTX, POOL_WIDTH), lambda b, t: (b, 0, 0)),
        pl.BlockSpec((1, RET_HEADS, RET_HEAD_DIM, RET_HEAD_DIM), lambda b, t: (b, 0, 0, 0)),
    )
    scratch = [
        pltpu.VMEM((POOL_PAD + tile, POOL_WIDTH), F32),
        pltpu.VMEM((tile, RET_WIDTH), F32),
        pltpu.VMEM((tile, RET_WIDTH), F32),
        pltpu.VMEM((tile, RET_WIDTH), F32),
        pltpu.VMEM((tile, RET_WIDTH), F32),
        pltpu.VMEM((tile, D_MODEL), BF16),
    ]
    return pl.pallas_call(
        functools.partial(_prompt_mixer_kernel, tile=tile),
        grid=(B, T // tile),
        in_specs=in_specs,
        out_specs=out_specs,
        out_shape=out_shape,
        scratch_shapes=scratch,
        compiler_params=pltpu.CompilerParams(
            dimension_semantics=("arbitrary", "arbitrary"),
            vmem_limit_bytes=VMEM_LIMIT_BYTES),
        name="prompt_mixer",
    )(x, w["g1"], w["w_in"], w["pool_w"], w["pool_scale"], w["gn_g"], w["gn_b"],
      w["w_out"], tabs["cq"], tabs["sq"], tabs["ck"], tabs["sk"], tabs["qd"], tabs["kd"],
      tabs["cd"], tabs["dmask"])


def _sample_mixer_kernel(x_ref, ctx_ref, s_in_ref, g1_ref, win_ref, poolw_ref, pscale_ref,
                         gng_ref, gnb_ref, wout_ref, cq_ref, sq_ref, ck_ref, sk_ref, qd_ref,
                         kd_ref, cd_ref, dmask_ref,
                         x1_ref, pool_out_ref, s_out_ref,
                         ue, qs_scr, ks_scr, v_scr, o_scr, *, nb, seq):
    rows = nb * seq
    x = x_ref[...]
    u, q, k, v, g = _project_in(x, g1_ref, win_ref)

    ue[:, 1:POOL_PAD, :] = ctx_ref[...]
    ue[:, POOL_PAD:POOL_PAD + seq, :] = u.reshape(nb, seq, POOL_WIDTH)
    pos = PAST_LEN + lax.broadcasted_iota(jnp.int32, (1, seq, 1), 1)
    mix_parts = []
    for gi, win in enumerate(POOL_WINDOWS):
        sl = slice(gi * POOL_GROUP_WIDTH, (gi + 1) * POOL_GROUP_WIDTH)
        cur = ue[:, POOL_PAD:POOL_PAD + seq, sl]
        s = cur
        for j in range(1, win):
            s = s + ue[:, POOL_PAD - j:POOL_PAD - j + seq, sl]
        cnt = jnp.minimum(pos + 1, win).astype(F32)
        pooled = (s / cnt - cur).reshape(rows, POOL_GROUP_WIDTH)
        y = _dot(pooled.astype(BF16), poolw_ref[gi]) * pscale_ref[:, sl]
        mix_parts.append(y.astype(BF16))
    pool_out_ref[...] = ue[:, seq + 1:seq + POOL_PAD, :]

    cq, sq, ck, sk = cq_ref[...], sq_ref[...], ck_ref[...], sk_ref[...]
    for hh in range(RET_HEADS):
        ls = slice(hh * RET_HEAD_DIM, (hh + 1) * RET_HEAD_DIM)
        qr = _rotate(q[:, ls], cq, sq)
        kr = _rotate(k[:, ls], ck, sk)
        vb = v[:, ls].astype(BF16)
        sc = _dot_nt(qr.astype(BF16), kr.astype(BF16)) * dmask_ref[hh]
        o_scr[:, ls] = _dot(sc.astype(BF16), vb)
        qs_scr[:, ls] = qr * qd_ref[:, ls]
        ks_scr[:, ls] = kr * kd_ref[:, ls]
    v_scr[...] = v

    def per_sequence(b, carry):
        r0 = pl.multiple_of(b * seq, seq)
        for hh in range(RET_HEADS):
            ls = slice(hh * RET_HEAD_DIM, (hh + 1) * RET_HEAD_DIM)
            state = s_in_ref[b, hh]
            qb = qs_scr[pl.ds(r0, seq), ls].astype(BF16)
            kb = ks_scr[pl.ds(r0, seq), ls].astype(BF16)
            vb = v_scr[pl.ds(r0, seq), ls].astype(BF16)
            o_scr[pl.ds(r0, seq), ls] = o_scr[pl.ds(r0, seq), ls] + _dot(qb, state.astype(BF16))
            s_out_ref[b, hh] = state * cd_ref[:, ls] + _dot_tn(kb, vb)
        return carry

    lax.fori_loop(0, nb, per_sequence, 0)

    for hh in range(RET_HEADS):
        ls = slice(hh * RET_HEAD_DIM, (hh + 1) * RET_HEAD_DIM)
        ret = _head_norm_gate(o_scr[:, ls], g[:, ls], gng_ref[:, ls], gnb_ref[:, ls])
        mix_parts.append(ret.astype(BF16))

    mix = jnp.concatenate(mix_parts, axis=-1)
    x1_ref[...] = x + _dot(mix, wout_ref[...])


def _sample_mixer(x2d, state_pool, state_ret, w, layer, tabs, seq):
    nb = SAMPLE_BATCH_TILE
    rows = nb * seq
    n_seq = state_pool.shape[1]
    assert n_seq % nb == 0 and seq == SUBLANES
    const2 = lambda i: (0, 0)
    const3 = lambda i: (0, 0, 0)
    in_specs = [
        pl.BlockSpec((rows, D_MODEL), lambda i: (i, 0)),
        pl.BlockSpec((None, nb, POOL_CTX, POOL_WIDTH), lambda i: (layer, i, 0, 0)),
        pl.BlockSpec((None, nb, RET_HEADS, RET_HEAD_DIM, RET_HEAD_DIM),
                     lambda i: (layer, i, 0, 0, 0)),
        _layer_spec(w["g1"], layer),
        _layer_spec(w["w_in"], layer),
        _layer_spec(w["pool_w"], layer),
        _layer_spec(w["pool_scale"], layer),
        _layer_spec(w["gn_g"], layer),
        _layer_spec(w["gn_b"], layer),
        _layer_spec(w["w_out"], layer),
        pl.BlockSpec((rows, RET_HEAD_DIM), const2),
        pl.BlockSpec((rows, RET_HEAD_DIM), const2),
        pl.BlockSpec((rows, RET_HEAD_DIM), const2),
        pl.BlockSpec((rows, RET_HEAD_DIM), const2),
        pl.BlockSpec((rows, RET_WIDTH), const2),
        pl.BlockSpec((rows, RET_WIDTH), const2),
        pl.BlockSpec((1, RET_WIDTH), const2),
        pl.BlockSpec((RET_HEADS, rows, rows), const3),
    ]
    out_shape = (
        jax.ShapeDtypeStruct(x2d.shape, F32),
        jax.ShapeDtypeStruct(state_pool.shape[1:], F32),
        jax.ShapeDtypeStruct(state_ret.shape[1:], F32),
    )
    out_specs = (
        pl.BlockSpec((rows, D_MODEL), lambda i: (i, 0)),
        pl.BlockSpec((nb, POOL_CTX, POOL_WIDTH), lambda i: (i, 0, 0)),
        pl.BlockSpec((nb, RET_HEADS, RET_HEAD_DIM, RET_HEAD_DIM), lambda i: (i, 0, 0, 0)),
    )
    scratch = [
        pltpu.VMEM((nb, POOL_PAD + seq, POOL_WIDTH), F32),
        pltpu.VMEM((rows, RET_WIDTH), F32),
        pltpu.VMEM((rows, RET_WIDTH), F32),
        pltpu.VMEM((rows, RET_WIDTH), F32),
        pltpu.VMEM((rows, RET_WIDTH), F32),
    ]
    return pl.pallas_call(
        functools.partial(_sample_mixer_kernel, nb=nb, seq=seq),
        grid=(n_seq // nb,),
        in_specs=in_specs,
        out_specs=out_specs,
        out_shape=out_shape,
        scratch_shapes=scratch,
        compiler_params=pltpu.CompilerParams(
            dimension_semantics=("arbitrary",),
            vmem_limit_bytes=VMEM_LIMIT_BYTES),
        name="sample_mixer",
    )(x2d, state_pool, state_ret, w["g1"], w["w_in"], w["pool_w"], w["pool_scale"], w["gn_g"],
      w["gn_b"], w["w_out"], tabs["cq"], tabs["sq"], tabs["ck"], tabs["sk"], tabs["qd"],
      tabs["kd"], tabs["cd"], tabs["dmask"])


def _ffn_kernel(x_ref, g2_ref, wup_ref, wdown_ref, fg_ref, o_ref, *, final_norm):
    x = x_ref[...]
    h = _rmsnorm(x, g2_ref[...]).astype(BF16)
    acc = x
    for c in range(D_FF // FFN_CHUNK):
        cols = slice(c * FFN_CHUNK, (c + 1) * FFN_CHUNK)
        a = jnp.square(jnp.maximum(_dot(h, wup_ref[:, cols]), 0.0)).astype(BF16)
        acc = acc + _dot(a, wdown_ref[cols, :])
    if final_norm:
        acc = _rmsnorm(acc, fg_ref[...])
    o_ref[...] = acc


def _ffn(x2d, w, layer, final_g, final_norm):
    n, d = x2d.shape
    tile = FFN_TILE
    assert n % tile == 0
    const2 = lambda i: (0, 0)
    return pl.pallas_call(
        functools.partial(_ffn_kernel, final_norm=final_norm),
        grid=(n // tile,),
        in_specs=[
            pl.BlockSpec((tile, d), lambda i: (i, 0)),
            _layer_spec(w["g2"], layer),
            _layer_spec(w["w_up"], layer),
            _layer_spec(w["w_down"], layer),
            pl.BlockSpec((1, d), const2),
        ],
        out_specs=pl.BlockSpec((tile, d), lambda i: (i, 0)),
        out_shape=jax.ShapeDtypeStruct((n, d), F32),
        compiler_params=pltpu.CompilerParams(
            dimension_semantics=("arbitrary",),
            vmem_limit_bytes=VMEM_LIMIT_BYTES),
        name="ffn",
    )(x2d, w["g2"], w["w_up"], w["w_down"], final_g)


def _retention_tables(pos, chunk, reps):
    half = RET_HEAD_DIM // 2
    inv = ROPE_BASE ** (-jnp.arange(half, dtype=F32) / half)
    ang = pos.astype(F32)[:, None] * inv[None, :]
    cos, sin = jnp.cos(ang), jnp.sin(ang)
    cq = jnp.concatenate([cos, cos], axis=-1)
    sq = jnp.concatenate([-sin, sin], axis=-1)
    kscale = RET_HEAD_DIM ** -0.5

    gamma = 1.0 - jnp.exp2(-5.0 - jnp.arange(RET_HEADS, dtype=F32))
    lg = jnp.log(gamma)
    i = jnp.arange(chunk, dtype=F32)
    diff = i[:, None] - i[None, :]
    dmask = jnp.where(diff >= 0, jnp.exp(lg[:, None, None] * jnp.maximum(diff, 0.0)), 0.0)
    q_decay = jnp.exp(lg[:, None] * (i + 1.0))
    k_decay = jnp.exp(lg[:, None] * (chunk - 1.0 - i))
    c_decay = jnp.exp(lg * chunk)

    def lanes(a):
        a = jnp.repeat(a.T[:, :, None], RET_HEAD_DIM, axis=2).reshape(chunk, RET_WIDTH)
        return jnp.tile(a, (reps, 1))

    if reps > 1:
        eye = jnp.eye(reps, dtype=F32)
        dmask = jnp.einsum("ab,hij->haibj", eye, dmask).reshape(
            RET_HEADS, reps * chunk, reps * chunk)
    return {
        "cq": cq, "sq": sq, "ck": cq * kscale, "sk": sq * kscale,
        "qd": lanes(q_decay), "kd": lanes(k_decay),
        "cd": jnp.repeat(c_decay, RET_HEAD_DIM)[None, :],
        "dmask": dmask,
    }


def kernel(x_prompt, x_sample, state_pool, state_ret, norm1_g, w_in, pool_w, pool_scale,
           gn_g, gn_b, w_out, norm2_g, w_up, w_down, final_g):
    depth = w_in.shape[0]
    B, T, D = x_prompt.shape
    NB, TS, _ = x_sample.shape

    tabs_p = _retention_tables(jnp.arange(T, dtype=jnp.int32), RET_CHUNK, 1)
    pos_s = PAST_LEN + jnp.arange(TS, dtype=jnp.int32)
    tabs_s = _retention_tables(jnp.tile(pos_s, SAMPLE_BATCH_TILE), TS, SAMPLE_BATCH_TILE)

    yp = x_prompt
    ys = x_sample.reshape(NB * TS, D)
    fg = final_g.reshape(1, D)
    pool_p, ret_p, pool_s, ret_s = [], [], [], []
    w = {
        "g1": norm1_g.reshape(depth, 1, D),
        "w_in": w_in.astype(BF16),
        "pool_w": pool_w.astype(BF16),
        "pool_scale": pool_scale.reshape(depth, 1, POOL_WIDTH),
        "gn_g": gn_g.reshape(depth, 1, RET_WIDTH),
        "gn_b": gn_b.reshape(depth, 1, RET_WIDTH),
        "w_out": w_out.astype(BF16),
        "g2": norm2_g.reshape(depth, 1, D),
        "w_up": w_up.astype(BF16),
        "w_down": w_down.astype(BF16),
    }
    for l in range(depth):
        last = l == depth - 1
        x1p, cp, sp = _prompt_mixer(yp, w, l, tabs_p)
        yp = _ffn(x1p.reshape(B * T, D), w, l, fg, last).reshape(B, T, D)
        x1s, cs, ss = _sample_mixer(ys, state_pool, state_ret, w, l, tabs_s, TS)
        ys = _ffn(x1s, w, l, fg, last)
        pool_p.append(cp)
        ret_p.append(sp)
        pool_s.append(cs)
        ret_s.append(ss)
    return (yp, ys.reshape(NB, TS, D), jnp.stack(pool_p), jnp.stack(ret_p),
            jnp.stack(pool_s), jnp.stack(ret_s))
```

```python
import functools

import jax
import jax.numpy as jnp
from jax import lax
from jax.experimental import pallas as pl
from jax.experimental.pallas import tpu as pltpu

D_MODEL = 1024
POOL_WIDTH = D_MODEL // 2
POOL_WINDOWS = (2, 4, 8, 16)
POOL_GROUP_WIDTH = POOL_WIDTH // len(POOL_WINDOWS)
POOL_CTX = max(POOL_WINDOWS) - 1
POOL_PAD = POOL_CTX + 1
RET_WIDTH = D_MODEL - POOL_WIDTH
RET_HEADS = 4
RET_HEAD_DIM = RET_WIDTH // RET_HEADS
RET_CHUNK = 128
ROPE_BASE = 10000.0
D_FF = 4 * D_MODEL
RMS_EPS = 1e-6
GN_EPS = 1e-5
PAST_LEN = 16384

SUBLANES = 8
VMEM_LIMIT_BYTES = 56 * 1024 * 1024

PROMPT_TILE = 1024
PROMPT_SUB = 256
SAMPLE_BATCH_TILE = 16
FFN_TILE = 512
FFN_CHUNK = 1024

F32 = jnp.float32
BF16 = jnp.bfloat16


def _dot(a, b):
    return jnp.dot(a, b, preferred_element_type=F32)


def _dot_nt(a, b):
    return lax.dot_general(a, b, (((1,), (1,)), ((), ())), preferred_element_type=F32)


def _dot_tn(a, b):
    return lax.dot_general(a, b, (((0,), (0,)), ((), ())), preferred_element_type=F32)


def _rmsnorm(x, g):
    ms = jnp.mean(x * x, axis=-1, keepdims=True)
    return x * lax.rsqrt(ms + RMS_EPS) * g


def _rotate(x, cos, sin_signed):
    return x * cos + pltpu.roll(x, RET_HEAD_DIM // 2, 1) * sin_signed


def _head_norm_gate(o, gate, gn_g, gn_b):
    mu = jnp.mean(o, axis=-1, keepdims=True)
    d = o - mu
    var = jnp.mean(d * d, axis=-1, keepdims=True)
    on = d * lax.rsqrt(var + GN_EPS) * gn_g + gn_b
    return gate * jax.nn.sigmoid(gate) * on


def _layer_spec(arr, layer):
    tail = (0,) * (arr.ndim - 1)
    return pl.BlockSpec((None,) + arr.shape[1:], lambda *_: (layer,) + tail)


def _stacked_outputs(body, n_in, prev):
    prev = tuple(prev)

    def kernel(*refs):
        body(*refs[:n_in], *refs[n_in + len(prev):])

    specs = [pl.BlockSpec(memory_space=pl.ANY)] * len(prev)
    aliases = {n_in + k: 1 + k for k in range(len(prev))}
    return kernel, specs, aliases, prev


def _project_in(x, g1_ref, win_ref):
    h = _rmsnorm(x, g1_ref[...]).astype(BF16)
    P, R = POOL_WIDTH, RET_WIDTH
    return [_dot(h, win_ref[:, lo:lo + w])
            for lo, w in ((0, P), (P, R), (P + R, R), (P + 2 * R, R), (P + 3 * R, R))]


def _prompt_mixer_kernel(x_ref, g1_ref, win_ref, poolw_ref, pscale_ref, gng_ref, gnb_ref,
                         wout_ref, cq_ref, sq_ref, ck_ref, sk_ref, qd_ref, kd_ref, cd_ref,
                         dmask_ref,
                         x1_ref, pool_out_ref, s_ref,
                         ubuf, h_scr, q_scr, k_scr, v_scr, g_scr, mix_scr, *, tile, sub):
    t = pl.program_id(1)

    @pl.when(t == 0)
    def _():
        ubuf[0:POOL_PAD, :] = jnp.zeros((POOL_PAD, POOL_WIDTH), F32)
        s_ref[...] = jnp.zeros_like(s_ref)

    P, R = POOL_WIDTH, RET_WIDTH

    def project_items(r0):
        rows = slice(r0, r0 + sub)

        def norm():
            h_scr[...] = _rmsnorm(x_ref[0, rows, :], g1_ref[...]).astype(BF16)

        def column_group(lo, dst, dst_rows):
            def run():
                dst[dst_rows, :] = _dot(h_scr[...], win_ref[:, lo:lo + R])
            return run

        return [norm,
                column_group(0, ubuf, slice(POOL_PAD + r0, POOL_PAD + r0 + sub)),
                column_group(P, q_scr, rows),
                column_group(P + R, k_scr, rows),
                column_group(P + 2 * R, v_scr, rows),
                column_group(P + 3 * R, g_scr, rows)]

    def mix_items(r0):
        base = POOL_PAD + r0
        items = []

        def pool_group(gi, win):
            def run():
                sl = slice(gi * POOL_GROUP_WIDTH, (gi + 1) * POOL_GROUP_WIDTH)
                pos = t * tile + r0 + lax.broadcasted_iota(jnp.int32, (sub, 1), 0)
                cur = ubuf[base:base + sub, sl]
                s = cur
                for j in range(1, win):
                    s = s + ubuf[base - j:base - j + sub, sl]
                cnt = jnp.minimum(pos + 1, win).astype(F32)
                pooled = s / cnt - cur
                y = _dot(pooled.astype(BF16), poolw_ref[gi]) * pscale_ref[:, sl]
                mix_scr[r0:r0 + sub, sl] = y.astype(BF16)
            return run

        live = {}

        def scores(c, hh):
            def run():
                rows = slice(r0 + c * RET_CHUNK, r0 + (c + 1) * RET_CHUNK)
                ls = slice(hh * RET_HEAD_DIM, (hh + 1) * RET_HEAD_DIM)
                qr = _rotate(q_scr[rows, ls], cq_ref[rows, :], sq_ref[rows, :])
                kr = _rotate(k_scr[rows, ls], ck_ref[rows, :], sk_ref[rows, :])
                sc = _dot_nt(qr.astype(BF16), kr.astype(BF16)) * dmask_ref[hh]
                lhs = jnp.concatenate(
                    [sc.astype(BF16), (qr * qd_ref[:, ls]).astype(BF16)], axis=1)
                kdk = (kr * kd_ref[:, ls]).astype(BF16)
                live[(c, hh)] = (lhs, kdk)
            return run

        def outputs(c, hh):
            def run():
                rows = slice(r0 + c * RET_CHUNK, r0 + (c + 1) * RET_CHUNK)
                ls = slice(hh * RET_HEAD_DIM, (hh + 1) * RET_HEAD_DIM)
                lhs, kdk = live.pop((c, hh))
                vb = v_scr[rows, ls].astype(BF16)
                state = s_ref[0, hh]
                rhs = jnp.concatenate([vb, state.astype(BF16)], axis=0)
                live[(c, hh, "o")] = _dot(lhs, rhs)
                s_ref[0, hh] = state * cd_ref[:, ls] + _dot_tn(kdk, vb)
            return run

        def normalise(c, hh):
            def run():
                rows = slice(r0 + c * RET_CHUNK, r0 + (c + 1) * RET_CHUNK)
                ls = slice(hh * RET_HEAD_DIM, (hh + 1) * RET_HEAD_DIM)
                o = live.pop((c, hh, "o"))
                ret = _head_norm_gate(o, g_scr[rows, ls], gng_ref[:, ls], gnb_ref[:, ls])
                mix_scr[rows, P + hh * RET_HEAD_DIM:P + (hh + 1) * RET_HEAD_DIM] = (
                    ret.astype(BF16))
            return run

        def out_project(lo, width):
            def run():
                rows = slice(r0, r0 + sub)
                x1_ref[0, rows, lo:lo + width] = (
                    x_ref[0, rows, lo:lo + width]
                    + _dot(mix_scr[rows, :], wout_ref[:, lo:lo + width]))
            return run

        heads = [(c, hh) for c in range(sub // RET_CHUNK) for hh in range(RET_HEADS)]
        stages = (scores, outputs, normalise)
        pools = [pool_group(gi, win) for gi, win in enumerate(POOL_WINDOWS)]
        for n in range(len(heads) + len(stages) - 1):
            for lag, stage in enumerate(stages):
                if 0 <= n - lag < len(heads):
                    items.append(stage(*heads[n - lag]))
            if n % 2 == 1 and pools:
                items.append(pools.pop(0))
        items.extend(pools)
        half = D_MODEL // 2
        items.extend([out_project(0, half), out_project(half, half)])
        return items

    def interleave(major, minor):
        out, taken = [], 0
        for n, item in enumerate(major):
            out.append(item)
            want = (n + 1) * len(minor) // len(major)
            out.extend(minor[taken:want])
            taken = want
        return out

    schedule = project_items(0)
    for r0 in range(0, tile, sub):
        nxt = project_items(r0 + sub) if r0 + sub < tile else []
        schedule += interleave(mix_items(r0), nxt)
    for item in schedule:
        item()

    pool_out_ref[0] = ubuf[tile + 1:tile + POOL_PAD, :]
    ubuf[0:POOL_PAD, :] = ubuf[tile:tile + POOL_PAD, :]


def _prompt_mixer(x, w, layer, tabs, prev):
    B, T, D = x.shape
    depth = w["w_in"].shape[0]
    tile = PROMPT_TILE
    assert T % tile == 0 and tile % RET_CHUNK == 0
    const2 = lambda b, t: (0, 0)
    const3 = lambda b, t: (0, 0, 0)
    tab_spec = pl.BlockSpec((tile, RET_HEAD_DIM), lambda b, t: (t, 0))
    in_specs = [
        pl.BlockSpec((1, tile, D), lambda b, t: (b, t, 0)),
        _layer_spec(w["g1"], layer),
        _layer_spec(w["w_in"], layer),
        _layer_spec(w["pool_w"], layer),
        _layer_spec(w["pool_scale"], layer),
        _layer_spec(w["gn_g"], layer),
        _layer_spec(w["gn_b"], layer),
        _layer_spec(w["w_out"], layer),
        tab_spec, tab_spec, tab_spec, tab_spec,
        pl.BlockSpec((RET_CHUNK, RET_WIDTH), const2),
        pl.BlockSpec((RET_CHUNK, RET_WIDTH), const2),
        pl.BlockSpec((1, RET_WIDTH), const2),
        pl.BlockSpec((RET_HEADS, RET_CHUNK, RET_CHUNK), const3),
    ]
    out_shape = (
        jax.ShapeDtypeStruct((B, T, D), F32),
        jax.ShapeDtypeStruct((depth, B, POOL_CTX, POOL_WIDTH), F32),
        jax.ShapeDtypeStruct((depth, B, RET_HEADS, RET_HEAD_DIM, RET_HEAD_DIM), F32),
    )
    out_specs = (
        pl.BlockSpec((1, tile, D), lambda b, t: (b, t, 0)),
        pl.BlockSpec((None, 1, POOL_CTX, POOL_WIDTH), lambda b, t: (layer, b, 0, 0)),
        pl.BlockSpec((None, 1, RET_HEADS, RET_HEAD_DIM, RET_HEAD_DIM),
                     lambda b, t: (layer, b, 0, 0, 0)),
    )
    body = functools.partial(_prompt_mixer_kernel, tile=tile, sub=PROMPT_SUB)
    kernel_fn, alias_specs, aliases, alias_args = _stacked_outputs(body, len(in_specs), prev)
    scratch = [
        pltpu.VMEM((POOL_PAD + tile, POOL_WIDTH), F32),
        pltpu.VMEM((PROMPT_SUB, D_MODEL), BF16),
        pltpu.VMEM((tile, RET_WIDTH), F32),
        pltpu.VMEM((tile, RET_WIDTH), F32),
        pltpu.VMEM((tile, RET_WIDTH), F32),
        pltpu.VMEM((tile, RET_WIDTH), F32),
        pltpu.VMEM((tile, D_MODEL), BF16),
    ]
    return pl.pallas_call(
        kernel_fn,
        grid=(B, T // tile),
        in_specs=in_specs + alias_specs,
        out_specs=out_specs,
        out_shape=out_shape,
        scratch_shapes=scratch,
        input_output_aliases=aliases,
        compiler_params=pltpu.CompilerParams(
            dimension_semantics=("arbitrary", "arbitrary"),
            vmem_limit_bytes=VMEM_LIMIT_BYTES),
        name="prompt_mixer",
    )(x, w["g1"], w["w_in"], w["pool_w"], w["pool_scale"], w["gn_g"], w["gn_b"],
      w["w_out"], tabs["cq"], tabs["sq"], tabs["ck"], tabs["sk"], tabs["qd"], tabs["kd"],
      tabs["cd"], tabs["dmask"], *alias_args)


def _sample_mixer_kernel(x_ref, ctx_ref, s_in_ref, g1_ref, win_ref, poolw_ref, pscale_ref,
                         gng_ref, gnb_ref, wout_ref, cq_ref, sq_ref, ck_ref, sk_ref, qd_ref,
                         kd_ref, cd_ref, dmask_ref,
                         x1_ref, pool_out_ref, s_out_ref,
                         ue, qs_scr, ks_scr, v_scr, o_scr, *, nb, seq):
    rows = nb * seq
    x = x_ref[...]
    u, q, k, v, g = _project_in(x, g1_ref, win_ref)

    ue[:, 1:POOL_PAD, :] = ctx_ref[...]
    ue[:, POOL_PAD:POOL_PAD + seq, :] = u.reshape(nb, seq, POOL_WIDTH)
    pos = PAST_LEN + lax.broadcasted_iota(jnp.int32, (1, seq, 1), 1)
    mix_parts = []
    for gi, win in enumerate(POOL_WINDOWS):
        sl = slice(gi * POOL_GROUP_WIDTH, (gi + 1) * POOL_GROUP_WIDTH)
        cur = ue[:, POOL_PAD:POOL_PAD + seq, sl]
        s = cur
        for j in range(1, win):
            s = s + ue[:, POOL_PAD - j:POOL_PAD - j + seq, sl]
        cnt = jnp.minimum(pos + 1, win).astype(F32)
        pooled = (s / cnt - cur).reshape(rows, POOL_GROUP_WIDTH)
        y = _dot(pooled.astype(BF16), poolw_ref[gi]) * pscale_ref[:, sl]
        mix_parts.append(y.astype(BF16))
    pool_out_ref[...] = ue[:, seq + 1:seq + POOL_PAD, :]

    cq, sq, ck, sk = cq_ref[...], sq_ref[...], ck_ref[...], sk_ref[...]
    for hh in range(RET_HEADS):
        ls = slice(hh * RET_HEAD_DIM, (hh + 1) * RET_HEAD_DIM)
        qr = _rotate(q[:, ls], cq, sq)
        kr = _rotate(k[:, ls], ck, sk)
        vb = v[:, ls].astype(BF16)
        sc = _dot_nt(qr.astype(BF16), kr.astype(BF16)) * dmask_ref[hh]
        o_scr[:, ls] = _dot(sc.astype(BF16), vb)
        qs_scr[:, ls] = qr * qd_ref[:, ls]
        ks_scr[:, ls] = kr * kd_ref[:, ls]
    v_scr[...] = v

    def per_sequence(b, carry):
        r0 = pl.multiple_of(b * seq, seq)
        for hh in range(RET_HEADS):
            ls = slice(hh * RET_HEAD_DIM, (hh + 1) * RET_HEAD_DIM)
            state = s_in_ref[b, hh]
            qb = qs_scr[pl.ds(r0, seq), ls].astype(BF16)
            kb = ks_scr[pl.ds(r0, seq), ls].astype(BF16)
            vb = v_scr[pl.ds(r0, seq), ls].astype(BF16)
            o_scr[pl.ds(r0, seq), ls] = o_scr[pl.ds(r0, seq), ls] + _dot(qb, state.astype(BF16))
            s_out_ref[b, hh] = state * cd_ref[:, ls] + _dot_tn(kb, vb)
        return carry

    lax.fori_loop(0, nb, per_sequence, 0)

    for hh in range(RET_HEADS):
        ls = slice(hh * RET_HEAD_DIM, (hh + 1) * RET_HEAD_DIM)
        ret = _head_norm_gate(o_scr[:, ls], g[:, ls], gng_ref[:, ls], gnb_ref[:, ls])
        mix_parts.append(ret.astype(BF16))

    mix = jnp.concatenate(mix_parts, axis=-1)
    x1_ref[...] = x + _dot(mix, wout_ref[...])


def _sample_mixer(x2d, state_pool, state_ret, w, layer, tabs, seq, prev):
    nb = SAMPLE_BATCH_TILE
    rows = nb * seq
    n_seq = state_pool.shape[1]
    assert n_seq % nb == 0 and seq == SUBLANES
    const2 = lambda i: (0, 0)
    const3 = lambda i: (0, 0, 0)
    in_specs = [
        pl.BlockSpec((rows, D_MODEL), lambda i: (i, 0)),
        pl.BlockSpec((None, nb, POOL_CTX, POOL_WIDTH), lambda i: (layer, i, 0, 0)),
        pl.BlockSpec((None, nb, RET_HEADS, RET_HEAD_DIM, RET_HEAD_DIM),
                     lambda i: (layer, i, 0, 0, 0)),
        _layer_spec(w["g1"], layer),
        _layer_spec(w["w_in"], layer),
        _layer_spec(w["pool_w"], layer),
        _layer_spec(w["pool_scale"], layer),
        _layer_spec(w["gn_g"], layer),
        _layer_spec(w["gn_b"], layer),
        _layer_spec(w["w_out"], layer),
        pl.BlockSpec((rows, RET_HEAD_DIM), const2),
        pl.BlockSpec((rows, RET_HEAD_DIM), const2),
        pl.BlockSpec((rows, RET_HEAD_DIM), const2),
        pl.BlockSpec((rows, RET_HEAD_DIM), const2),
        pl.BlockSpec((rows, RET_WIDTH), const2),
        pl.BlockSpec((rows, RET_WIDTH), const2),
        pl.BlockSpec((1, RET_WIDTH), const2),
        pl.BlockSpec((RET_HEADS, rows, rows), const3),
    ]
    out_shape = (
        jax.ShapeDtypeStruct(x2d.shape, F32),
        jax.ShapeDtypeStruct(state_pool.shape, F32),
        jax.ShapeDtypeStruct(state_ret.shape, F32),
    )
    out_specs = (
        pl.BlockSpec((rows, D_MODEL), lambda i: (i, 0)),
        pl.BlockSpec((None, nb, POOL_CTX, POOL_WIDTH), lambda i: (layer, i, 0, 0)),
        pl.BlockSpec((None, nb, RET_HEADS, RET_HEAD_DIM, RET_HEAD_DIM),
                     lambda i: (layer, i, 0, 0, 0)),
    )
    body = functools.partial(_sample_mixer_kernel, nb=nb, seq=seq)
    kernel_fn, alias_specs, aliases, alias_args = _stacked_outputs(body, len(in_specs), prev)
    scratch = [
        pltpu.VMEM((nb, POOL_PAD + seq, POOL_WIDTH), F32),
        pltpu.VMEM((rows, RET_WIDTH), F32),
        pltpu.VMEM((rows, RET_WIDTH), F32),
        pltpu.VMEM((rows, RET_WIDTH), F32),
        pltpu.VMEM((rows, RET_WIDTH), F32),
    ]
    return pl.pallas_call(
        kernel_fn,
        grid=(n_seq // nb,),
        in_specs=in_specs + alias_specs,
        out_specs=out_specs,
        out_shape=out_shape,
        scratch_shapes=scratch,
        input_output_aliases=aliases,
        compiler_params=pltpu.CompilerParams(
            dimension_semantics=("arbitrary",),
            vmem_limit_bytes=VMEM_LIMIT_BYTES),
        name="sample_mixer",
    )(x2d, state_pool, state_ret, w["g1"], w["w_in"], w["pool_w"], w["pool_scale"], w["gn_g"],
      w["gn_b"], w["w_out"], tabs["cq"], tabs["sq"], tabs["ck"], tabs["sk"], tabs["qd"],
      tabs["kd"], tabs["cd"], tabs["dmask"], *alias_args)


def _ffn_kernel(x_ref, g2_ref, wup_ref, wdown_ref, fg_ref, o_ref, *, final_norm):
    x = x_ref[...]
    h = _rmsnorm(x, g2_ref[...]).astype(BF16)
    acc = x
    for c in range(D_FF // FFN_CHUNK):
        cols = slice(c * FFN_CHUNK, (c + 1) * FFN_CHUNK)
        a = jnp.square(jnp.maximum(_dot(h, wup_ref[:, cols]), 0.0)).astype(BF16)
        acc = acc + _dot(a, wdown_ref[cols, :])
    if final_norm:
        acc = _rmsnorm(acc, fg_ref[...])
    o_ref[...] = acc


def _ffn(x2d, w, layer, final_g, final_norm):
    n, d = x2d.shape
    tile = FFN_TILE
    assert n % tile == 0
    const2 = lambda i: (0, 0)
    return pl.pallas_call(
        functools.partial(_ffn_kernel, final_norm=final_norm),
        grid=(n // tile,),
        in_specs=[
            pl.BlockSpec((tile, d), lambda i: (i, 0)),
            _layer_spec(w["g2"], layer),
            _layer_spec(w["w_up"], layer),
            _layer_spec(w["w_down"], layer),
            pl.BlockSpec((1, d), const2),
        ],
        out_specs=pl.BlockSpec((tile, d), lambda i: (i, 0)),
        out_shape=jax.ShapeDtypeStruct((n, d), F32),
        compiler_params=pltpu.CompilerParams(
            dimension_semantics=("arbitrary",),
            vmem_limit_bytes=VMEM_LIMIT_BYTES),
        name="ffn",
    )(x2d, w["g2"], w["w_up"], w["w_down"], final_g)


def _retention_tables(pos, chunk, reps):
    half = RET_HEAD_DIM // 2
    inv = ROPE_BASE ** (-jnp.arange(half, dtype=F32) / half)
    ang = pos.astype(F32)[:, None] * inv[None, :]
    cos, sin = jnp.cos(ang), jnp.sin(ang)
    cq = jnp.concatenate([cos, cos], axis=-1)
    sq = jnp.concatenate([-sin, sin], axis=-1)
    kscale = RET_HEAD_DIM ** -0.5

    gamma = 1.0 - jnp.exp2(-5.0 - jnp.arange(RET_HEADS, dtype=F32))
    lg = jnp.log(gamma)
    i = jnp.arange(chunk, dtype=F32)
    diff = i[:, None] - i[None, :]
    dmask = jnp.where(diff >= 0, jnp.exp(lg[:, None, None] * jnp.maximum(diff, 0.0)), 0.0)
    q_decay = jnp.exp(lg[:, None] * (i + 1.0))
    k_decay = jnp.exp(lg[:, None] * (chunk - 1.0 - i))
    c_decay = jnp.exp(lg * chunk)

    def lanes(a):
        a = jnp.repeat(a.T[:, :, None], RET_HEAD_DIM, axis=2).reshape(chunk, RET_WIDTH)
        return jnp.tile(a, (reps, 1))

    if reps > 1:
        eye = jnp.eye(reps, dtype=F32)
        dmask = jnp.einsum("ab,hij->haibj", eye, dmask).reshape(
            RET_HEADS, reps * chunk, reps * chunk)
    return {
        "cq": cq, "sq": sq, "ck": cq * kscale, "sk": sq * kscale,
        "qd": lanes(q_decay), "kd": lanes(k_decay),
        "cd": jnp.repeat(c_decay, RET_HEAD_DIM)[None, :],
        "dmask": dmask,
    }


def kernel(x_prompt, x_sample, state_pool, state_ret, norm1_g, w_in, pool_w, pool_scale,
           gn_g, gn_b, w_out, norm2_g, w_up, w_down, final_g):
    depth = w_in.shape[0]
    B, T, D = x_prompt.shape
    NB, TS, _ = x_sample.shape

    tabs_p = _retention_tables(jnp.arange(T, dtype=jnp.int32), RET_CHUNK, 1)
    pos_s = PAST_LEN + jnp.arange(TS, dtype=jnp.int32)
    tabs_s = _retention_tables(jnp.tile(pos_s, SAMPLE_BATCH_TILE), TS, SAMPLE_BATCH_TILE)

    yp = x_prompt
    ys = x_sample.reshape(NB * TS, D)
    fg = final_g.reshape(1, D)
    w = {
        "g1": norm1_g.reshape(depth, 1, D),
        "w_in": w_in.astype(BF16),
        "pool_w": pool_w.astype(BF16),
        "pool_scale": pool_scale.reshape(depth, 1, POOL_WIDTH),
        "gn_g": gn_g.reshape(depth, 1, RET_WIDTH),
        "gn_b": gn_b.reshape(depth, 1, RET_WIDTH),
        "w_out": w_out.astype(BF16),
        "g2": norm2_g.reshape(depth, 1, D),
        "w_up": w_up.astype(BF16),
        "w_down": w_down.astype(BF16),
    }
    states_p, states_s = (), ()
    for l in range(depth):
        last = l == depth - 1
        x1p, *states_p = _prompt_mixer(yp, w, l, tabs_p, states_p)
        yp = _ffn(x1p.reshape(B * T, D), w, l, fg, last).reshape(B, T, D)
        x1s, *states_s = _sample_mixer(ys, state_pool, state_ret, w, l, tabs_s, TS, states_s)
        ys = _ffn(x1s, w, l, fg, last)
    return (yp, ys.reshape(NB, TS, D), states_p[0], states_p[1], states_s[0], states_s[1])
```

```python
import functools

import jax
import jax.numpy as jnp
from jax import lax
from jax.experimental import pallas as pl
from jax.experimental.pallas import tpu as pltpu

D_MODEL = 1024
POOL_WIDTH = D_MODEL // 2
POOL_WINDOWS = (2, 4, 8, 16)
POOL_GROUP_WIDTH = POOL_WIDTH // len(POOL_WINDOWS)
POOL_CTX = max(POOL_WINDOWS) - 1
POOL_PAD = POOL_CTX + 1
RET_WIDTH = D_MODEL - POOL_WIDTH
RET_HEADS = 4
RET_HEAD_DIM = RET_WIDTH // RET_HEADS
RET_CHUNK = 128
ROPE_BASE = 10000.0
D_FF = 4 * D_MODEL
RMS_EPS = 1e-6
GN_EPS = 1e-5
PAST_LEN = 16384

SUBLANES = 8
VMEM_LIMIT_BYTES = 56 * 1024 * 1024

PROMPT_TILE = 1024
PROMPT_SUB = 256
SAMPLE_BATCH_TILE = 16
FFN_TILE = 512
FFN_CHUNK = 1024

F32 = jnp.float32
BF16 = jnp.bfloat16


def _dot(a, b):
    return jnp.dot(a, b, preferred_element_type=F32)


def _dot_nt(a, b):
    return lax.dot_general(a, b, (((1,), (1,)), ((), ())), preferred_element_type=F32)


def _dot_tn(a, b):
    return lax.dot_general(a, b, (((0,), (0,)), ((), ())), preferred_element_type=F32)


def _rmsnorm(x, g):
    ms = jnp.mean(x * x, axis=-1, keepdims=True)
    return x * lax.rsqrt(ms + RMS_EPS) * g


def _rotate(x, cos, sin_signed):
    return x * cos + pltpu.roll(x, RET_HEAD_DIM // 2, 1) * sin_signed


def _head_norm_gate(o, gate, gn_g, gn_b):
    mu = jnp.mean(o, axis=-1, keepdims=True)
    d = o - mu
    var = jnp.mean(d * d, axis=-1, keepdims=True)
    on = d * lax.rsqrt(var + GN_EPS) * gn_g + gn_b
    return gate * jax.nn.sigmoid(gate) * on


def _layer_spec(arr, layer):
    tail = (0,) * (arr.ndim - 1)
    return pl.BlockSpec((None,) + arr.shape[1:], lambda *_: (layer,) + tail)


def _stacked_outputs(body, n_in, prev):
    prev = tuple(prev)

    def kernel(*refs):
        body(*refs[:n_in], *refs[n_in + len(prev):])

    specs = [pl.BlockSpec(memory_space=pl.ANY)] * len(prev)
    aliases = {n_in + k: 1 + k for k in range(len(prev))}
    return kernel, specs, aliases, prev


def _project_in(x, g1_ref, win_ref):
    h = _rmsnorm(x, g1_ref[...]).astype(BF16)
    P, R = POOL_WIDTH, RET_WIDTH
    return [_dot(h, win_ref[:, lo:lo + w])
            for lo, w in ((0, P), (P, R), (P + R, R), (P + 2 * R, R), (P + 3 * R, R))]


def _prompt_mixer_kernel(x_ref, g1_ref, win_ref, poolw_ref, pscale_ref, gng_ref, gnb_ref,
                         wout_ref, cq_ref, sq_ref, ck_ref, sk_ref, qd_ref, kd_ref, cd_ref,
                         dmask_ref,
                         x1_ref, pool_out_ref, s_ref,
                         ubuf, h_scr, q_scr, k_scr, v_scr, g_scr, mix_scr, *, tile, sub):
    t = pl.program_id(1)

    @pl.when(t == 0)
    def _():
        ubuf[0:POOL_PAD, :] = jnp.zeros((POOL_PAD, POOL_WIDTH), F32)
        s_ref[...] = jnp.zeros_like(s_ref)

    P, R = POOL_WIDTH, RET_WIDTH

    def project_items(r0):
        rows = slice(r0, r0 + sub)

        def norm():
            h_scr[...] = _rmsnorm(x_ref[0, rows, :], g1_ref[...]).astype(BF16)

        def column_group(lo, dst, dst_rows):
            def run():
                dst[dst_rows, :] = _dot(h_scr[...], win_ref[:, lo:lo + R])
            return run

        return [norm,
                column_group(0, ubuf, slice(POOL_PAD + r0, POOL_PAD + r0 + sub)),
                column_group(P, q_scr, rows),
                column_group(P + R, k_scr, rows),
                column_group(P + 2 * R, v_scr, rows),
                column_group(P + 3 * R, g_scr, rows)]

    def mix_items(r0):
        base = POOL_PAD + r0
        items = []

        def pool_group(gi, win):
            def run():
                sl = slice(gi * POOL_GROUP_WIDTH, (gi + 1) * POOL_GROUP_WIDTH)
                pos = t * tile + r0 + lax.broadcasted_iota(jnp.int32, (sub, 1), 0)
                cur = ubuf[base:base + sub, sl]
                s = cur
                for j in range(1, win):
                    s = s + ubuf[base - j:base - j + sub, sl]
                cnt = jnp.minimum(pos + 1, win).astype(F32)
                pooled = s / cnt - cur
                y = _dot(pooled.astype(BF16), poolw_ref[gi]) * pscale_ref[:, sl]
                mix_scr[r0:r0 + sub, sl] = y.astype(BF16)
            return run

        live = {}

        def scores(c, hh):
            def run():
                rows = slice(r0 + c * RET_CHUNK, r0 + (c + 1) * RET_CHUNK)
                ls = slice(hh * RET_HEAD_DIM, (hh + 1) * RET_HEAD_DIM)
                qr = _rotate(q_scr[rows, ls], cq_ref[rows, :], sq_ref[rows, :])
                kr = _rotate(k_scr[rows, ls], ck_ref[rows, :], sk_ref[rows, :])
                sc = _dot_nt(qr.astype(BF16), kr.astype(BF16)) * dmask_ref[hh]
                lhs = jnp.concatenate(
                    [sc.astype(BF16), (qr * qd_ref[:, ls]).astype(BF16)], axis=1)
                kdk = (kr * kd_ref[:, ls]).astype(BF16)
                live[(c, hh)] = (lhs, kdk)
            return run

        def outputs(c, hh):
            def run():
                rows = slice(r0 + c * RET_CHUNK, r0 + (c + 1) * RET_CHUNK)
                ls = slice(hh * RET_HEAD_DIM, (hh + 1) * RET_HEAD_DIM)
                lhs, kdk = live.pop((c, hh))
                vb = v_scr[rows, ls].astype(BF16)
                state = s_ref[0, hh]
                rhs = jnp.concatenate([vb, state.astype(BF16)], axis=0)
                live[(c, hh, "o")] = _dot(lhs, rhs)
                s_ref[0, hh] = state * cd_ref[:, ls] + _dot_tn(kdk, vb)
            return run

        def normalise(c, hh):
            def run():
                rows = slice(r0 + c * RET_CHUNK, r0 + (c + 1) * RET_CHUNK)
                ls = slice(hh * RET_HEAD_DIM, (hh + 1) * RET_HEAD_DIM)
                o = live.pop((c, hh, "o"))
                ret = _head_norm_gate(o, g_scr[rows, ls], gng_ref[:, ls], gnb_ref[:, ls])
                mix_scr[rows, P + hh * RET_HEAD_DIM:P + (hh + 1) * RET_HEAD_DIM] = (
                    ret.astype(BF16))
            return run

        def out_project(lo, width):
            def run():
                rows = slice(r0, r0 + sub)
                x1_ref[0, rows, lo:lo + width] = (
                    x_ref[0, rows, lo:lo + width]
                    + _dot(mix_scr[rows, :], wout_ref[:, lo:lo + width]))
            return run

        heads = [(c, hh) for c in range(sub // RET_CHUNK) for hh in range(RET_HEADS)]
        stages = (scores, outputs, normalise)
        pools = [pool_group(gi, win) for gi, win in enumerate(POOL_WINDOWS)]
        for n in range(len(heads) + len(stages) - 1):
            for lag, stage in enumerate(stages):
                if 0 <= n - lag < len(heads):
                    items.append(stage(*heads[n - lag]))
            if n % 2 == 1 and pools:
                items.append(pools.pop(0))
        items.extend(pools)
        half = D_MODEL // 2
        items.extend([out_project(0, half), out_project(half, half)])
        return items

    def interleave(major, minor):
        out, taken = [], 0
        for n, item in enumerate(major):
            out.append(item)
            want = (n + 1) * len(minor) // len(major)
            out.extend(minor[taken:want])
            taken = want
        return out

    schedule = project_items(0)
    for r0 in range(0, tile, sub):
        nxt = project_items(r0 + sub) if r0 + sub < tile else []
        schedule += interleave(mix_items(r0), nxt)
    for item in schedule:
        item()

    pool_out_ref[0] = ubuf[tile + 1:tile + POOL_PAD, :]
    ubuf[0:POOL_PAD, :] = ubuf[tile:tile + POOL_PAD, :]


def _prompt_mixer(x, w, layer, tabs, prev):
    B, T, D = x.shape
    depth = w["w_in"].shape[0]
    tile = PROMPT_TILE
    assert T % tile == 0 and tile % RET_CHUNK == 0
    const2 = lambda b, t: (0, 0)
    const3 = lambda b, t: (0, 0, 0)
    tab_spec = pl.BlockSpec((tile, RET_HEAD_DIM), lambda b, t: (t, 0))
    in_specs = [
        pl.BlockSpec((1, tile, D), lambda b, t: (b, t, 0)),
        _layer_spec(w["g1"], layer),
        _layer_spec(w["w_in"], layer),
        _layer_spec(w["pool_w"], layer),
        _layer_spec(w["pool_scale"], layer),
        _layer_spec(w["gn_g"], layer),
        _layer_spec(w["gn_b"], layer),
        _layer_spec(w["w_out"], layer),
        tab_spec, tab_spec, tab_spec, tab_spec,
        pl.BlockSpec((RET_CHUNK, RET_WIDTH), const2),
        pl.BlockSpec((RET_CHUNK, RET_WIDTH), const2),
        pl.BlockSpec((1, RET_WIDTH), const2),
        pl.BlockSpec((RET_HEADS, RET_CHUNK, RET_CHUNK), const3),
    ]
    out_shape = (
        jax.ShapeDtypeStruct((B, T, D), F32),
        jax.ShapeDtypeStruct((depth, B, POOL_CTX, POOL_WIDTH), F32),
        jax.ShapeDtypeStruct((depth, B, RET_HEADS, RET_HEAD_DIM, RET_HEAD_DIM), F32),
    )
    out_specs = (
        pl.BlockSpec((1, tile, D), lambda b, t: (b, t, 0)),
        pl.BlockSpec((None, 1, POOL_CTX, POOL_WIDTH), lambda b, t: (layer, b, 0, 0)),
        pl.BlockSpec((None, 1, RET_HEADS, RET_HEAD_DIM, RET_HEAD_DIM),
                     lambda b, t: (layer, b, 0, 0, 0)),
    )
    body = functools.partial(_prompt_mixer_kernel, tile=tile, sub=PROMPT_SUB)
    kernel_fn, alias_specs, aliases, alias_args = _stacked_outputs(body, len(in_specs), prev)
    scratch = [
        pltpu.VMEM((POOL_PAD + tile, POOL_WIDTH), F32),
        pltpu.VMEM((PROMPT_SUB, D_MODEL), BF16),
        pltpu.VMEM((tile, RET_WIDTH), F32),
        pltpu.VMEM((tile, RET_WIDTH), F32),
        pltpu.VMEM((tile, RET_WIDTH), F32),
        pltpu.VMEM((tile, RET_WIDTH), F32),
        pltpu.VMEM((tile, D_MODEL), BF16),
    ]
    return pl.pallas_call(
        kernel_fn,
        grid=(B, T // tile),
        in_specs=in_specs + alias_specs,
        out_specs=out_specs,
        out_shape=out_shape,
        scratch_shapes=scratch,
        input_output_aliases=aliases,
        compiler_params=pltpu.CompilerParams(
            dimension_semantics=("arbitrary", "arbitrary"),
            vmem_limit_bytes=VMEM_LIMIT_BYTES),
        name="prompt_mixer",
    )(x, w["g1"], w["w_in"], w["pool_w"], w["pool_scale"], w["gn_g"], w["gn_b"],
      w["w_out"], tabs["cq"], tabs["sq"], tabs["ck"], tabs["sk"], tabs["qd"], tabs["kd"],
      tabs["cd"], tabs["dmask"], *alias_args)


def _sample_mixer_kernel(x_ref, ctx_ref, s_in_ref, g1_ref, win_ref, poolw_ref, pscale_ref,
                         gng_ref, gnb_ref, wout_ref, cq_ref, sq_ref, ck_ref, sk_ref, qd_ref,
                         kd_ref, cd_ref, dmask_ref,
                         x1_ref, pool_out_ref, s_out_ref,
                         ue, qs_scr, ks_scr, v_scr, o_scr, *, nb, seq):
    rows = nb * seq
    x = x_ref[...]
    u, q, k, v, g = _project_in(x, g1_ref, win_ref)

    ue[:, 1:POOL_PAD, :] = ctx_ref[...]
    ue[:, POOL_PAD:POOL_PAD + seq, :] = u.reshape(nb, seq, POOL_WIDTH)
    pos = PAST_LEN + lax.broadcasted_iota(jnp.int32, (1, seq, 1), 1)
    mix_parts = []
    for gi, win in enumerate(POOL_WINDOWS):
        sl = slice(gi * POOL_GROUP_WIDTH, (gi + 1) * POOL_GROUP_WIDTH)
        cur = ue[:, POOL_PAD:POOL_PAD + seq, sl]
        s = cur
        for j in range(1, win):
            s = s + ue[:, POOL_PAD - j:POOL_PAD - j + seq, sl]
        cnt = jnp.minimum(pos + 1, win).astype(F32)
        pooled = (s / cnt - cur).reshape(rows, POOL_GROUP_WIDTH)
        y = _dot(pooled.astype(BF16), poolw_ref[gi]) * pscale_ref[:, sl]
        mix_parts.append(y.astype(BF16))
    pool_out_ref[...] = ue[:, seq + 1:seq + POOL_PAD, :]

    cq, sq, ck, sk = cq_ref[...], sq_ref[...], ck_ref[...], sk_ref[...]
    for hh in range(RET_HEADS):
        ls = slice(hh * RET_HEAD_DIM, (hh + 1) * RET_HEAD_DIM)
        qr = _rotate(q[:, ls], cq, sq)
        kr = _rotate(k[:, ls], ck, sk)
        vb = v[:, ls].astype(BF16)
        sc = _dot_nt(qr.astype(BF16), kr.astype(BF16)) * dmask_ref[hh]
        o_scr[:, ls] = _dot(sc.astype(BF16), vb)
        qs_scr[:, ls] = qr * qd_ref[:, ls]
        ks_scr[:, ls] = kr * kd_ref[:, ls]
    v_scr[...] = v

    def per_sequence(b, carry):
        r0 = pl.multiple_of(b * seq, seq)
        for hh in range(RET_HEADS):
            ls = slice(hh * RET_HEAD_DIM, (hh + 1) * RET_HEAD_DIM)
            state = s_in_ref[b, hh]
            qb = qs_scr[pl.ds(r0, seq), ls].astype(BF16)
            kb = ks_scr[pl.ds(r0, seq), ls].astype(BF16)
            vb = v_scr[pl.ds(r0, seq), ls].astype(BF16)
            o_scr[pl.ds(r0, seq), ls] = o_scr[pl.ds(r0, seq), ls] + _dot(qb, state.astype(BF16))
            s_out_ref[b, hh] = state * cd_ref[:, ls] + _dot_tn(kb, vb)
        return carry

    lax.fori_loop(0, nb, per_sequence, 0, unroll=8)

    for hh in range(RET_HEADS):
        ls = slice(hh * RET_HEAD_DIM, (hh + 1) * RET_HEAD_DIM)
        ret = _head_norm_gate(o_scr[:, ls], g[:, ls], gng_ref[:, ls], gnb_ref[:, ls])
        mix_parts.append(ret.astype(BF16))

    mix = jnp.concatenate(mix_parts, axis=-1)
    x1_ref[...] = x + _dot(mix, wout_ref[...])


def _sample_mixer(x2d, state_pool, state_ret, w, layer, tabs, seq, prev):
    nb = SAMPLE_BATCH_TILE
    rows = nb * seq
    n_seq = state_pool.shape[1]
    assert n_seq % nb == 0 and seq == SUBLANES
    const2 = lambda i: (0, 0)
    const3 = lambda i: (0, 0, 0)
    in_specs = [
        pl.BlockSpec((rows, D_MODEL), lambda i: (i, 0)),
        pl.BlockSpec((None, nb, POOL_CTX, POOL_WIDTH), lambda i: (layer, i, 0, 0)),
        pl.BlockSpec((None, nb, RET_HEADS, RET_HEAD_DIM, RET_HEAD_DIM),
                     lambda i: (layer, i, 0, 0, 0)),
        _layer_spec(w["g1"], layer),
        _layer_spec(w["w_in"], layer),
        _layer_spec(w["pool_w"], layer),
        _layer_spec(w["pool_scale"], layer),
        _layer_spec(w["gn_g"], layer),
        _layer_spec(w["gn_b"], layer),
        _layer_spec(w["w_out"], layer),
        pl.BlockSpec((rows, RET_HEAD_DIM), const2),
        pl.BlockSpec((rows, RET_HEAD_DIM), const2),
        pl.BlockSpec((rows, RET_HEAD_DIM), const2),
        pl.BlockSpec((rows, RET_HEAD_DIM), const2),
        pl.BlockSpec((rows, RET_WIDTH), const2),
        pl.BlockSpec((rows, RET_WIDTH), const2),
        pl.BlockSpec((1, RET_WIDTH), const2),
        pl.BlockSpec((RET_HEADS, rows, rows), const3),
    ]
    out_shape = (
        jax.ShapeDtypeStruct(x2d.shape, F32),
        jax.ShapeDtypeStruct(state_pool.shape, F32),
        jax.ShapeDtypeStruct(state_ret.shape, F32),
    )
    out_specs = (
        pl.BlockSpec((rows, D_MODEL), lambda i: (i, 0)),
        pl.BlockSpec((None, nb, POOL_CTX, POOL_WIDTH), lambda i: (layer, i, 0, 0)),
        pl.BlockSpec((None, nb, RET_HEADS, RET_HEAD_DIM, RET_HEAD_DIM),
                     lambda i: (layer, i, 0, 0, 0)),
    )
    body = functools.partial(_sample_mixer_kernel, nb=nb, seq=seq)
    kernel_fn, alias_specs, aliases, alias_args = _stacked_outputs(body, len(in_specs), prev)
    scratch = [
        pltpu.VMEM((nb, POOL_PAD + seq, POOL_WIDTH), F32),
        pltpu.VMEM((rows, RET_WIDTH), F32),
        pltpu.VMEM((rows, RET_WIDTH), F32),
        pltpu.VMEM((rows, RET_WIDTH), F32),
        pltpu.VMEM((rows, RET_WIDTH), F32),
    ]
    return pl.pallas_call(
        kernel_fn,
        grid=(n_seq // nb,),
        in_specs=in_specs + alias_specs,
        out_specs=out_specs,
        out_shape=out_shape,
        scratch_shapes=scratch,
        input_output_aliases=aliases,
        compiler_params=pltpu.CompilerParams(
            dimension_semantics=("arbitrary",),
            vmem_limit_bytes=VMEM_LIMIT_BYTES),
        name="sample_mixer",
    )(x2d, state_pool, state_ret, w["g1"], w["w_in"], w["pool_w"], w["pool_scale"], w["gn_g"],
      w["gn_b"], w["w_out"], tabs["cq"], tabs["sq"], tabs["ck"], tabs["sk"], tabs["qd"],
      tabs["kd"], tabs["cd"], tabs["dmask"], *alias_args)


def _ffn_kernel(xp_ref, xs_ref, g2_ref, wup_ref, wdown_ref, fg_ref, op_ref, os_ref, *,
                final_norm):
    n_p = xp_ref.shape[0]
    xp, xs = xp_ref[...], xs_ref[...]
    h = jnp.concatenate([_rmsnorm(xp, g2_ref[...]).astype(BF16),
                         _rmsnorm(xs, g2_ref[...]).astype(BF16)], axis=0)
    acc = None
    for c in range(D_FF // FFN_CHUNK):
        cols = slice(c * FFN_CHUNK, (c + 1) * FFN_CHUNK)
        a = jnp.square(jnp.maximum(_dot(h, wup_ref[:, cols]), 0.0)).astype(BF16)
        d = _dot(a, wdown_ref[cols, :])
        acc = d if acc is None else acc + d
    yp, ys = xp + acc[:n_p], xs + acc[n_p:]
    if final_norm:
        yp, ys = _rmsnorm(yp, fg_ref[...]), _rmsnorm(ys, fg_ref[...])
    op_ref[...] = yp
    os_ref[...] = ys


def _ffn(xp2d, xs2d, w, layer, final_g, final_norm):
    n, d = xp2d.shape
    tile = FFN_TILE
    steps = n // tile
    tile_s = xs2d.shape[0] // steps
    assert n % tile == 0 and xs2d.shape[0] == steps * tile_s and tile_s % (2 * SUBLANES) == 0
    const2 = lambda i: (0, 0)
    return pl.pallas_call(
        functools.partial(_ffn_kernel, final_norm=final_norm),
        grid=(steps,),
        in_specs=[
            pl.BlockSpec((tile, d), lambda i: (i, 0)),
            pl.BlockSpec((tile_s, d), lambda i: (i, 0)),
            _layer_spec(w["g2"], layer),
            _layer_spec(w["w_up"], layer),
            _layer_spec(w["w_down"], layer),
            pl.BlockSpec((1, d), const2),
        ],
        out_specs=(pl.BlockSpec((tile, d), lambda i: (i, 0)),
                   pl.BlockSpec((tile_s, d), lambda i: (i, 0))),
        out_shape=(jax.ShapeDtypeStruct((n, d), F32),
                   jax.ShapeDtypeStruct(xs2d.shape, F32)),
        compiler_params=pltpu.CompilerParams(
            dimension_semantics=("arbitrary",),
            vmem_limit_bytes=VMEM_LIMIT_BYTES),
        name="ffn",
    )(xp2d, xs2d, w["g2"], w["w_up"], w["w_down"], final_g)


def _retention_tables(pos, chunk, reps):
    half = RET_HEAD_DIM // 2
    inv = ROPE_BASE ** (-jnp.arange(half, dtype=F32) / half)
    ang = pos.astype(F32)[:, None] * inv[None, :]
    cos, sin = jnp.cos(ang), jnp.sin(ang)
    cq = jnp.concatenate([cos, cos], axis=-1)
    sq = jnp.concatenate([-sin, sin], axis=-1)
    kscale = RET_HEAD_DIM ** -0.5

    gamma = 1.0 - jnp.exp2(-5.0 - jnp.arange(RET_HEADS, dtype=F32))
    lg = jnp.log(gamma)
    i = jnp.arange(chunk, dtype=F32)
    diff = i[:, None] - i[None, :]
    dmask = jnp.where(diff >= 0, jnp.exp(lg[:, None, None] * jnp.maximum(diff, 0.0)), 0.0)
    q_decay = jnp.exp(lg[:, None] * (i + 1.0))
    k_decay = jnp.exp(lg[:, None] * (chunk - 1.0 - i))
    c_decay = jnp.exp(lg * chunk)

    def lanes(a):
        a = jnp.repeat(a.T[:, :, None], RET_HEAD_DIM, axis=2).reshape(chunk, RET_WIDTH)
        return jnp.tile(a, (reps, 1))

    if reps > 1:
        eye = jnp.eye(reps, dtype=F32)
        dmask = jnp.einsum("ab,hij->haibj", eye, dmask).reshape(
            RET_HEADS, reps * chunk, reps * chunk)
    return {
        "cq": cq, "sq": sq, "ck": cq * kscale, "sk": sq * kscale,
        "qd": lanes(q_decay), "kd": lanes(k_decay),
        "cd": jnp.repeat(c_decay, RET_HEAD_DIM)[None, :],
        "dmask": dmask,
    }


def kernel(x_prompt, x_sample, state_pool, state_ret, norm1_g, w_in, pool_w, pool_scale,
           gn_g, gn_b, w_out, norm2_g, w_up, w_down, final_g):
    depth = w_in.shape[0]
    B, T, D = x_prompt.shape
    NB, TS, _ = x_sample.shape

    tabs_p = _retention_tables(jnp.arange(T, dtype=jnp.int32), RET_CHUNK, 1)
    pos_s = PAST_LEN + jnp.arange(TS, dtype=jnp.int32)
    tabs_s = _retention_tables(jnp.tile(pos_s, SAMPLE_BATCH_TILE), TS, SAMPLE_BATCH_TILE)

    yp = x_prompt
    ys = x_sample.reshape(NB * TS, D)
    fg = final_g.reshape(1, D)
    w = {
        "g1": norm1_g.reshape(depth, 1, D),
        "w_in": w_in.astype(BF16),
        "pool_w": pool_w.astype(BF16),
        "pool_scale": pool_scale.reshape(depth, 1, POOL_WIDTH),
        "gn_g": gn_g.reshape(depth, 1, RET_WIDTH),
        "gn_b": gn_b.reshape(depth, 1, RET_WIDTH),
        "w_out": w_out.astype(BF16),
        "g2": norm2_g.reshape(depth, 1, D),
        "w_up": w_up.astype(BF16),
        "w_down": w_down.astype(BF16),
    }
    states_p, states_s = (), ()
    for l in range(depth):
        last = l == depth - 1
        x1p, *states_p = _prompt_mixer(yp, w, l, tabs_p, states_p)
        x1s, *states_s = _sample_mixer(ys, state_pool, state_ret, w, l, tabs_s, TS, states_s)
        yp, ys = _ffn(x1p.reshape(B * T, D), x1s, w, l, fg, last)
        yp = yp.reshape(B, T, D)
    return (yp, ys.reshape(NB, TS, D), states_p[0], states_p[1], states_s[0], states_s[1])
```

```python
import functools

import jax
import jax.numpy as jnp
from jax import lax
from jax.experimental import pallas as pl
from jax.experimental.pallas import tpu as pltpu

D_MODEL = 1024
POOL_WIDTH = D_MODEL // 2
POOL_WINDOWS = (2, 4, 8, 16)
POOL_GROUP_WIDTH = POOL_WIDTH // len(POOL_WINDOWS)
POOL_CTX = max(POOL_WINDOWS) - 1
POOL_PAD = POOL_CTX + 1
RET_WIDTH = D_MODEL - POOL_WIDTH
RET_HEADS = 4
RET_HEAD_DIM = RET_WIDTH // RET_HEADS
RET_CHUNK = 128
ROPE_BASE = 10000.0
D_FF = 4 * D_MODEL
RMS_EPS = 1e-6
GN_EPS = 1e-5
PAST_LEN = 16384

SUBLANES = 8
VMEM_LIMIT_BYTES = 56 * 1024 * 1024

PROMPT_TILE = 1024
PROMPT_SUB = 256
SAMPLE_BATCH_TILE = 16
FFN_TILE = 512
FFN_CHUNK = 1024

F32 = jnp.float32
BF16 = jnp.bfloat16


def _dot(a, b):
    return jnp.dot(a, b, preferred_element_type=F32)


def _dot_nt(a, b):
    return lax.dot_general(a, b, (((1,), (1,)), ((), ())), preferred_element_type=F32)


def _dot_tn(a, b):
    return lax.dot_general(a, b, (((0,), (0,)), ((), ())), preferred_element_type=F32)


def _rmsnorm(x, g):
    ms = jnp.mean(x * x, axis=-1, keepdims=True)
    return x * lax.rsqrt(ms + RMS_EPS) * g


def _rotate(x, cos, sin_signed):
    return x * cos + pltpu.roll(x, RET_HEAD_DIM // 2, 1) * sin_signed


def _head_norm_gate(o, gate, gn_g, gn_b):
    mu = jnp.mean(o, axis=-1, keepdims=True)
    d = o - mu
    var = jnp.mean(d * d, axis=-1, keepdims=True)
    on = d * lax.rsqrt(var + GN_EPS) * gn_g + gn_b
    return gate * jax.nn.sigmoid(gate) * on


def _layer_spec(arr, layer):
    tail = (0,) * (arr.ndim - 1)
    return pl.BlockSpec((None,) + arr.shape[1:], lambda *_: (layer,) + tail)


def _whole_spec(arr):
    zeros = (0,) * arr.ndim
    return pl.BlockSpec(arr.shape, lambda *_: zeros)


def _cast_specs(src, layer, steps, step_of):
    _, rows, cols = src.shape
    assert rows % (steps * 2 * SUBLANES) == 0
    blk = rows // steps
    in_spec = pl.BlockSpec((None, blk, cols), lambda *g: (layer, step_of(*g), 0))
    out_spec = pl.BlockSpec((blk, cols), lambda *g: (step_of(*g), 0))
    return in_spec, out_spec, jax.ShapeDtypeStruct((rows, cols), BF16)


def _stacked_outputs(body, n_in, prev):
    prev = tuple(prev)

    def kernel(*refs):
        body(*refs[:n_in], *refs[n_in + len(prev):])

    specs = [pl.BlockSpec(memory_space=pl.ANY)] * len(prev)
    aliases = {n_in + k: 1 + k for k in range(len(prev))}
    return kernel, specs, aliases, prev


def _project_in(x, g1_ref, win_ref):
    h = _rmsnorm(x, g1_ref[...]).astype(BF16)
    P, R = POOL_WIDTH, RET_WIDTH
    return [_dot(h, win_ref[:, lo:lo + w])
            for lo, w in ((0, P), (P, R), (P + R, R), (P + 2 * R, R), (P + 3 * R, R))]


def _prompt_mixer_kernel(x_ref, g1_ref, win_ref, poolw_ref, pscale_ref, gng_ref, gnb_ref,
                         wout_ref, cq_ref, sq_ref, ck_ref, sk_ref, qd_ref, kd_ref, cd_ref,
                         dmask_ref, wup32_ref, wdown32_ref,
                         x1_ref, pool_out_ref, s_ref, wup16_ref, wdown16_ref,
                         ubuf, h_scr, q_scr, k_scr, v_scr, g_scr, mix_scr, *, tile, sub):
    t = pl.program_id(1)

    wup16_ref[...] = wup32_ref[...].astype(BF16)
    wdown16_ref[...] = wdown32_ref[...].astype(BF16)

    @pl.when(t == 0)
    def _():
        ubuf[0:POOL_PAD, :] = jnp.zeros((POOL_PAD, POOL_WIDTH), F32)
        s_ref[...] = jnp.zeros_like(s_ref)

    P, R = POOL_WIDTH, RET_WIDTH

    def project_items(r0):
        rows = slice(r0, r0 + sub)

        def norm():
            h_scr[...] = _rmsnorm(x_ref[0, rows, :], g1_ref[...]).astype(BF16)

        def column_group(lo, dst, dst_rows):
            def run():
                dst[dst_rows, :] = _dot(h_scr[...], win_ref[:, lo:lo + R])
            return run

        return [norm,
                column_group(0, ubuf, slice(POOL_PAD + r0, POOL_PAD + r0 + sub)),
                column_group(P, q_scr, rows),
                column_group(P + R, k_scr, rows),
                column_group(P + 2 * R, v_scr, rows),
                column_group(P + 3 * R, g_scr, rows)]

    def mix_items(r0):
        base = POOL_PAD + r0
        items = []

        def pool_group(gi, win):
            def run():
                sl = slice(gi * POOL_GROUP_WIDTH, (gi + 1) * POOL_GROUP_WIDTH)
                pos = t * tile + r0 + lax.broadcasted_iota(jnp.int32, (sub, 1), 0)
                cur = ubuf[base:base + sub, sl]
                s = cur
                for j in range(1, win):
                    s = s + ubuf[base - j:base - j + sub, sl]
                cnt = jnp.minimum(pos + 1, win).astype(F32)
                pooled = s / cnt - cur
                y = _dot(pooled.astype(BF16), poolw_ref[gi]) * pscale_ref[:, sl]
                mix_scr[r0:r0 + sub, sl] = y.astype(BF16)
            return run

        live = {}

        def scores(c, hh):
            def run():
                rows = slice(r0 + c * RET_CHUNK, r0 + (c + 1) * RET_CHUNK)
                ls = slice(hh * RET_HEAD_DIM, (hh + 1) * RET_HEAD_DIM)
                qr = _rotate(q_scr[rows, ls], cq_ref[rows, :], sq_ref[rows, :])
                kr = _rotate(k_scr[rows, ls], ck_ref[rows, :], sk_ref[rows, :])
                sc = _dot_nt(qr.astype(BF16), kr.astype(BF16)) * dmask_ref[hh]
                lhs = jnp.concatenate(
                    [sc.astype(BF16), (qr * qd_ref[:, ls]).astype(BF16)], axis=1)
                kdk = (kr * kd_ref[:, ls]).astype(BF16)
                live[(c, hh)] = (lhs, kdk)
            return run

        def outputs(c, hh):
            def run():
                rows = slice(r0 + c * RET_CHUNK, r0 + (c + 1) * RET_CHUNK)
                ls = slice(hh * RET_HEAD_DIM, (hh + 1) * RET_HEAD_DIM)
                lhs, kdk = live.pop((c, hh))
                vb = v_scr[rows, ls].astype(BF16)
                state = s_ref[0, hh]
                rhs = jnp.concatenate([vb, state.astype(BF16)], axis=0)
                live[(c, hh, "o")] = _dot(lhs, rhs)
                s_ref[0, hh] = state * cd_ref[:, ls] + _dot_tn(kdk, vb)
            return run

        def normalise(c, hh):
            def run():
                rows = slice(r0 + c * RET_CHUNK, r0 + (c + 1) * RET_CHUNK)
                ls = slice(hh * RET_HEAD_DIM, (hh + 1) * RET_HEAD_DIM)
                o = live.pop((c, hh, "o"))
                ret = _head_norm_gate(o, g_scr[rows, ls], gng_ref[:, ls], gnb_ref[:, ls])
                mix_scr[rows, P + hh * RET_HEAD_DIM:P + (hh + 1) * RET_HEAD_DIM] = (
                    ret.astype(BF16))
            return run

        def out_project(lo, width):
            def run():
                rows = slice(r0, r0 + sub)
                x1_ref[0, rows, lo:lo + width] = (
                    x_ref[0, rows, lo:lo + width]
                    + _dot(mix_scr[rows, :], wout_ref[:, lo:lo + width]))
            return run

        heads = [(c, hh) for c in range(sub // RET_CHUNK) for hh in range(RET_HEADS)]
        stages = (scores, outputs, normalise)
        pools = [pool_group(gi, win) for gi, win in enumerate(POOL_WINDOWS)]
        for n in range(len(heads) + len(stages) - 1):
            for lag, stage in enumerate(stages):
                if 0 <= n - lag < len(heads):
                    items.append(stage(*heads[n - lag]))
            if n % 2 == 1 and pools:
                items.append(pools.pop(0))
        items.extend(pools)
        half = D_MODEL // 2
        items.extend([out_project(0, half), out_project(half, half)])
        return items

    def interleave(major, minor):
        out, taken = [], 0
        for n, item in enumerate(major):
            out.append(item)
            want = (n + 1) * len(minor) // len(major)
            out.extend(minor[taken:want])
            taken = want
        return out

    schedule = project_items(0)
    for r0 in range(0, tile, sub):
        nxt = project_items(r0 + sub) if r0 + sub < tile else []
        schedule += interleave(mix_items(r0), nxt)
    for item in schedule:
        item()

    pool_out_ref[0] = ubuf[tile + 1:tile + POOL_PAD, :]
    ubuf[0:POOL_PAD, :] = ubuf[tile:tile + POOL_PAD, :]


def _prompt_mixer(x, w, big, layer, tabs, prev, ffn_w32):
    B, T, D = x.shape
    depth = w["g1"].shape[0]
    tile = PROMPT_TILE
    assert T % tile == 0 and tile % RET_CHUNK == 0
    const2 = lambda b, t: (0, 0)
    const3 = lambda b, t: (0, 0, 0)
    tab_spec = pl.BlockSpec((tile, RET_HEAD_DIM), lambda b, t: (t, 0))
    in_specs = [
        pl.BlockSpec((1, tile, D), lambda b, t: (b, t, 0)),
        _layer_spec(w["g1"], layer),
        _whole_spec(big["w_in"]),
        _layer_spec(w["pool_w"], layer),
        _layer_spec(w["pool_scale"], layer),
        _layer_spec(w["gn_g"], layer),
        _layer_spec(w["gn_b"], layer),
        _whole_spec(big["w_out"]),
        tab_spec, tab_spec, tab_spec, tab_spec,
        pl.BlockSpec((RET_CHUNK, RET_WIDTH), const2),
        pl.BlockSpec((RET_CHUNK, RET_WIDTH), const2),
        pl.BlockSpec((1, RET_WIDTH), const2),
        pl.BlockSpec((RET_HEADS, RET_CHUNK, RET_CHUNK), const3),
    ]
    n_t = T // tile
    casts = [_cast_specs(src, layer, B * n_t, lambda b, t: b * n_t + t) for src in ffn_w32]
    in_specs += [c[0] for c in casts]
    out_shape = (
        jax.ShapeDtypeStruct((B, T, D), F32),
        jax.ShapeDtypeStruct((depth, B, POOL_CTX, POOL_WIDTH), F32),
        jax.ShapeDtypeStruct((depth, B, RET_HEADS, RET_HEAD_DIM, RET_HEAD_DIM), F32),
    ) + tuple(c[2] for c in casts)
    out_specs = (
        pl.BlockSpec((1, tile, D), lambda b, t: (b, t, 0)),
        pl.BlockSpec((None, 1, POOL_CTX, POOL_WIDTH), lambda b, t: (layer, b, 0, 0)),
        pl.BlockSpec((None, 1, RET_HEADS, RET_HEAD_DIM, RET_HEAD_DIM),
                     lambda b, t: (layer, b, 0, 0, 0)),
    ) + tuple(c[1] for c in casts)
    body = functools.partial(_prompt_mixer_kernel, tile=tile, sub=PROMPT_SUB)
    kernel_fn, alias_specs, aliases, alias_args = _stacked_outputs(body, len(in_specs), prev)
    scratch = [
        pltpu.VMEM((POOL_PAD + tile, POOL_WIDTH), F32),
        pltpu.VMEM((PROMPT_SUB, D_MODEL), BF16),
        pltpu.VMEM((tile, RET_WIDTH), F32),
        pltpu.VMEM((tile, RET_WIDTH), F32),
        pltpu.VMEM((tile, RET_WIDTH), F32),
        pltpu.VMEM((tile, RET_WIDTH), F32),
        pltpu.VMEM((tile, D_MODEL), BF16),
    ]
    return pl.pallas_call(
        kernel_fn,
        grid=(B, T // tile),
        in_specs=in_specs + alias_specs,
        out_specs=out_specs,
        out_shape=out_shape,
        scratch_shapes=scratch,
        input_output_aliases=aliases,
        compiler_params=pltpu.CompilerParams(
            dimension_semantics=("arbitrary", "arbitrary"),
            vmem_limit_bytes=VMEM_LIMIT_BYTES),
        name="prompt_mixer",
    )(x, w["g1"], big["w_in"], w["pool_w"], w["pool_scale"], w["gn_g"], w["gn_b"],
      big["w_out"], tabs["cq"], tabs["sq"], tabs["ck"], tabs["sk"], tabs["qd"], tabs["kd"],
      tabs["cd"], tabs["dmask"], *ffn_w32, *alias_args)


def _sample_mixer_kernel(x_ref, ctx_ref, s_in_ref, g1_ref, win_ref, poolw_ref, pscale_ref,
                         gng_ref, gnb_ref, wout_ref, cq_ref, sq_ref, ck_ref, sk_ref, qd_ref,
                         kd_ref, cd_ref, dmask_ref,
                         x1_ref, pool_out_ref, s_out_ref,
                         ue, qs_scr, ks_scr, v_scr, o_scr, *, nb, seq):
    rows = nb * seq
    x = x_ref[...]
    u, q, k, v, g = _project_in(x, g1_ref, win_ref)

    ue[:, 1:POOL_PAD, :] = ctx_ref[...]
    ue[:, POOL_PAD:POOL_PAD + seq, :] = u.reshape(nb, seq, POOL_WIDTH)
    pos = PAST_LEN + lax.broadcasted_iota(jnp.int32, (1, seq, 1), 1)
    mix_parts = []
    for gi, win in enumerate(POOL_WINDOWS):
        sl = slice(gi * POOL_GROUP_WIDTH, (gi + 1) * POOL_GROUP_WIDTH)
        cur = ue[:, POOL_PAD:POOL_PAD + seq, sl]
        s = cur
        for j in range(1, win):
            s = s + ue[:, POOL_PAD - j:POOL_PAD - j + seq, sl]
        cnt = jnp.minimum(pos + 1, win).astype(F32)
        pooled = (s / cnt - cur).reshape(rows, POOL_GROUP_WIDTH)
        y = _dot(pooled.astype(BF16), poolw_ref[gi]) * pscale_ref[:, sl]
        mix_parts.append(y.astype(BF16))
    pool_out_ref[...] = ue[:, seq + 1:seq + POOL_PAD, :]

    cq, sq, ck, sk = cq_ref[...], sq_ref[...], ck_ref[...], sk_ref[...]
    for hh in range(RET_HEADS):
        ls = slice(hh * RET_HEAD_DIM, (hh + 1) * RET_HEAD_DIM)
        qr = _rotate(q[:, ls], cq, sq)
        kr = _rotate(k[:, ls], ck, sk)
        vb = v[:, ls].astype(BF16)
        sc = _dot_nt(qr.astype(BF16), kr.astype(BF16)) * dmask_ref[hh]
        o_scr[:, ls] = _dot(sc.astype(BF16), vb)
        qs_scr[:, ls] = qr * qd_ref[:, ls]
        ks_scr[:, ls] = kr * kd_ref[:, ls]
    v_scr[...] = v

    def per_sequence(b, carry):
        r0 = pl.multiple_of(b * seq, seq)
        for hh in range(RET_HEADS):
            ls = slice(hh * RET_HEAD_DIM, (hh + 1) * RET_HEAD_DIM)
            state = s_in_ref[b, hh]
            qb = qs_scr[pl.ds(r0, seq), ls].astype(BF16)
            kb = ks_scr[pl.ds(r0, seq), ls].astype(BF16)
            vb = v_scr[pl.ds(r0, seq), ls].astype(BF16)
            o_scr[pl.ds(r0, seq), ls] = o_scr[pl.ds(r0, seq), ls] + _dot(qb, state.astype(BF16))
            s_out_ref[b, hh] = state * cd_ref[:, ls] + _dot_tn(kb, vb)
        return carry

    lax.fori_loop(0, nb, per_sequence, 0, unroll=8)

    for hh in range(RET_HEADS):
        ls = slice(hh * RET_HEAD_DIM, (hh + 1) * RET_HEAD_DIM)
        ret = _head_norm_gate(o_scr[:, ls], g[:, ls], gng_ref[:, ls], gnb_ref[:, ls])
        mix_parts.append(ret.astype(BF16))

    mix = jnp.concatenate(mix_parts, axis=-1)
    x1_ref[...] = x + _dot(mix, wout_ref[...])


def _sample_mixer(x2d, state_pool, state_ret, w, big, layer, tabs, seq, prev):
    nb = SAMPLE_BATCH_TILE
    rows = nb * seq
    n_seq = state_pool.shape[1]
    assert n_seq % nb == 0 and seq == SUBLANES
    const2 = lambda i: (0, 0)
    const3 = lambda i: (0, 0, 0)
    in_specs = [
        pl.BlockSpec((rows, D_MODEL), lambda i: (i, 0)),
        pl.BlockSpec((None, nb, POOL_CTX, POOL_WIDTH), lambda i: (layer, i, 0, 0)),
        pl.BlockSpec((None, nb, RET_HEADS, RET_HEAD_DIM, RET_HEAD_DIM),
                     lambda i: (layer, i, 0, 0, 0)),
        _layer_spec(w["g1"], layer),
        _whole_spec(big["w_in"]),
        _layer_spec(w["pool_w"], layer),
        _layer_spec(w["pool_scale"], layer),
        _layer_spec(w["gn_g"], layer),
        _layer_spec(w["gn_b"], layer),
        _whole_spec(big["w_out"]),
        pl.BlockSpec((rows, RET_HEAD_DIM), const2),
        pl.BlockSpec((rows, RET_HEAD_DIM), const2),
        pl.BlockSpec((rows, RET_HEAD_DIM), const2),
        pl.BlockSpec((rows, RET_HEAD_DIM), const2),
        pl.BlockSpec((rows, RET_WIDTH), const2),
        pl.BlockSpec((rows, RET_WIDTH), const2),
        pl.BlockSpec((1, RET_WIDTH), const2),
        pl.BlockSpec((RET_HEADS, rows, rows), const3),
    ]
    out_shape = (
        jax.ShapeDtypeStruct(x2d.shape, F32),
        jax.ShapeDtypeStruct(state_pool.shape, F32),
        jax.ShapeDtypeStruct(state_ret.shape, F32),
    )
    out_specs = (
        pl.BlockSpec((rows, D_MODEL), lambda i: (i, 0)),
        pl.BlockSpec((None, nb, POOL_CTX, POOL_WIDTH), lambda i: (layer, i, 0, 0)),
        pl.BlockSpec((None, nb, RET_HEADS, RET_HEAD_DIM, RET_HEAD_DIM),
                     lambda i: (layer, i, 0, 0, 0)),
    )
    body = functools.partial(_sample_mixer_kernel, nb=nb, seq=seq)
    kernel_fn, alias_specs, aliases, alias_args = _stacked_outputs(body, len(in_specs), prev)
    scratch = [
        pltpu.VMEM((nb, POOL_PAD + seq, POOL_WIDTH), F32),
        pltpu.VMEM((rows, RET_WIDTH), F32),
        pltpu.VMEM((rows, RET_WIDTH), F32),
        pltpu.VMEM((rows, RET_WIDTH), F32),
        pltpu.VMEM((rows, RET_WIDTH), F32),
    ]
    return pl.pallas_call(
        kernel_fn,
        grid=(n_seq // nb,),
        in_specs=in_specs + alias_specs,
        out_specs=out_specs,
        out_shape=out_shape,
        scratch_shapes=scratch,
        input_output_aliases=aliases,
        compiler_params=pltpu.CompilerParams(
            dimension_semantics=("arbitrary",),
            vmem_limit_bytes=VMEM_LIMIT_BYTES),
        name="sample_mixer",
    )(x2d, state_pool, state_ret, w["g1"], big["w_in"], w["pool_w"], w["pool_scale"], w["gn_g"],
      w["gn_b"], big["w_out"], tabs["cq"], tabs["sq"], tabs["ck"], tabs["sk"], tabs["qd"],
      tabs["kd"], tabs["cd"], tabs["dmask"], *alias_args)


def _ffn_kernel(xp_ref, xs_ref, g2_ref, wup_ref, wdown_ref, fg_ref, *rest, final_norm):
    if final_norm:
        op_ref, os_ref = rest
    else:
        win32_ref, wout32_ref, op_ref, os_ref, win16_ref, wout16_ref = rest
        win16_ref[...] = win32_ref[...].astype(BF16)
        wout16_ref[...] = wout32_ref[...].astype(BF16)
    n_p = xp_ref.shape[0]
    xp, xs = xp_ref[...], xs_ref[...]
    h = jnp.concatenate([_rmsnorm(xp, g2_ref[...]).astype(BF16),
                         _rmsnorm(xs, g2_ref[...]).astype(BF16)], axis=0)
    acc = None
    for c in range(D_FF // FFN_CHUNK):
        cols = slice(c * FFN_CHUNK, (c + 1) * FFN_CHUNK)
        a = jnp.square(jnp.maximum(_dot(h, wup_ref[:, cols]), 0.0)).astype(BF16)
        d = _dot(a, wdown_ref[cols, :])
        acc = d if acc is None else acc + d
    yp, ys = xp + acc[:n_p], xs + acc[n_p:]
    if final_norm:
        yp, ys = _rmsnorm(yp, fg_ref[...]), _rmsnorm(ys, fg_ref[...])
    op_ref[...] = yp
    os_ref[...] = ys


def _ffn(xp2d, xs2d, w, big, layer, final_g, mixer_w32):
    final_norm = not mixer_w32
    n, d = xp2d.shape
    tile = FFN_TILE
    steps = n // tile
    tile_s = xs2d.shape[0] // steps
    assert n % tile == 0 and xs2d.shape[0] == steps * tile_s and tile_s % (2 * SUBLANES) == 0
    const2 = lambda i: (0, 0)
    casts = [_cast_specs(src, layer + 1, steps, lambda i: i) for src in mixer_w32]
    return pl.pallas_call(
        functools.partial(_ffn_kernel, final_norm=final_norm),
        grid=(steps,),
        in_specs=[
            pl.BlockSpec((tile, d), lambda i: (i, 0)),
            pl.BlockSpec((tile_s, d), lambda i: (i, 0)),
            _layer_spec(w["g2"], layer),
            _whole_spec(big["w_up"]),
            _whole_spec(big["w_down"]),
            pl.BlockSpec((1, d), const2),
        ] + [c[0] for c in casts],
        out_specs=(pl.BlockSpec((tile, d), lambda i: (i, 0)),
                   pl.BlockSpec((tile_s, d), lambda i: (i, 0))) + tuple(c[1] for c in casts),
        out_shape=(jax.ShapeDtypeStruct((n, d), F32),
                   jax.ShapeDtypeStruct(xs2d.shape, F32)) + tuple(c[2] for c in casts),
        compiler_params=pltpu.CompilerParams(
            dimension_semantics=("arbitrary",),
            vmem_limit_bytes=VMEM_LIMIT_BYTES),
        name="ffn",
    )(xp2d, xs2d, w["g2"], big["w_up"], big["w_down"], final_g, *mixer_w32)


def _retention_tables(pos, chunk, reps):
    half = RET_HEAD_DIM // 2
    inv = ROPE_BASE ** (-jnp.arange(half, dtype=F32) / half)
    ang = pos.astype(F32)[:, None] * inv[None, :]
    cos, sin = jnp.cos(ang), jnp.sin(ang)
    cq = jnp.concatenate([cos, cos], axis=-1)
    sq = jnp.concatenate([-sin, sin], axis=-1)
    kscale = RET_HEAD_DIM ** -0.5

    gamma = 1.0 - jnp.exp2(-5.0 - jnp.arange(RET_HEADS, dtype=F32))
    lg = jnp.log(gamma)
    i = jnp.arange(chunk, dtype=F32)
    diff = i[:, None] - i[None, :]
    dmask = jnp.where(diff >= 0, jnp.exp(lg[:, None, None] * jnp.maximum(diff, 0.0)), 0.0)
    q_decay = jnp.exp(lg[:, None] * (i + 1.0))
    k_decay = jnp.exp(lg[:, None] * (chunk - 1.0 - i))
    c_decay = jnp.exp(lg * chunk)

    def lanes(a):
        a = jnp.repeat(a.T[:, :, None], RET_HEAD_DIM, axis=2).reshape(chunk, RET_WIDTH)
        return jnp.tile(a, (reps, 1))

    if reps > 1:
        eye = jnp.eye(reps, dtype=F32)
        dmask = jnp.einsum("ab,hij->haibj", eye, dmask).reshape(
            RET_HEADS, reps * chunk, reps * chunk)
    return {
        "cq": cq, "sq": sq, "ck": cq * kscale, "sk": sq * kscale,
        "qd": lanes(q_decay), "kd": lanes(k_decay),
        "cd": jnp.repeat(c_decay, RET_HEAD_DIM)[None, :],
        "dmask": dmask,
    }


def kernel(x_prompt, x_sample, state_pool, state_ret, norm1_g, w_in, pool_w, pool_scale,
           gn_g, gn_b, w_out, norm2_g, w_up, w_down, final_g):
    depth = w_in.shape[0]
    B, T, D = x_prompt.shape
    NB, TS, _ = x_sample.shape

    tabs_p = _retention_tables(jnp.arange(T, dtype=jnp.int32), RET_CHUNK, 1)
    pos_s = PAST_LEN + jnp.arange(TS, dtype=jnp.int32)
    tabs_s = _retention_tables(jnp.tile(pos_s, SAMPLE_BATCH_TILE), TS, SAMPLE_BATCH_TILE)

    yp = x_prompt
    ys = x_sample.reshape(NB * TS, D)
    fg = final_g.reshape(1, D)
    w = {
        "g1": norm1_g.reshape(depth, 1, D),
        "pool_w": pool_w.astype(BF16),
        "pool_scale": pool_scale.reshape(depth, 1, POOL_WIDTH),
        "gn_g": gn_g.reshape(depth, 1, RET_WIDTH),
        "gn_b": gn_b.reshape(depth, 1, RET_WIDTH),
        "g2": norm2_g.reshape(depth, 1, D),
    }
    big = {"w_in": w_in[0].astype(BF16), "w_out": w_out[0].astype(BF16)}
    states_p, states_s = (), ()
    for l in range(depth):
        x1p, pool_p, ret_p, big["w_up"], big["w_down"] = _prompt_mixer(
            yp, w, big, l, tabs_p, states_p, (w_up, w_down))
        states_p = (pool_p, ret_p)
        x1s, *states_s = _sample_mixer(ys, state_pool, state_ret, w, big, l, tabs_s, TS,
                                       states_s)
        nxt = (w_in, w_out) if l + 1 < depth else ()
        yp, ys, *nxt16 = _ffn(x1p.reshape(B * T, D), x1s, w, big, l, fg, nxt)
        if nxt16:
            big["w_in"], big["w_out"] = nxt16
        yp = yp.reshape(B, T, D)
    return (yp, ys.reshape(NB, TS, D), states_p[0], states_p[1], states_s[0], states_s[1])
```

```python
import functools

import jax
import jax.numpy as jnp
from jax import lax
from jax.experimental import pallas as pl
from jax.experimental.pallas import tpu as pltpu

D_MODEL = 1024
POOL_WIDTH = D_MODEL // 2
POOL_WINDOWS = (2, 4, 8, 16)
POOL_GROUP_WIDTH = POOL_WIDTH // len(POOL_WINDOWS)
POOL_CTX = max(POOL_WINDOWS) - 1
POOL_PAD = POOL_CTX + 1
RET_WIDTH = D_MODEL - POOL_WIDTH
RET_HEADS = 4
RET_HEAD_DIM = RET_WIDTH // RET_HEADS
RET_CHUNK = 128
ROPE_BASE = 10000.0
D_FF = 4 * D_MODEL
RMS_EPS = 1e-6
GN_EPS = 1e-5
PAST_LEN = 16384

SUBLANES = 8
VMEM_LIMIT_BYTES = 56 * 1024 * 1024

PROMPT_TILE = 1024
PROMPT_SUB = 256
SAMPLE_BATCH_TILE = 16
FFN_TILE = 1024
FFN_CHUNK = 1024

F32 = jnp.float32
BF16 = jnp.bfloat16


def _dot(a, b):
    return jnp.dot(a, b, preferred_element_type=F32)


def _dot_nt(a, b):
    return lax.dot_general(a, b, (((1,), (1,)), ((), ())), preferred_element_type=F32)


def _dot_tn(a, b):
    return lax.dot_general(a, b, (((0,), (0,)), ((), ())), preferred_element_type=F32)


def _rmsnorm(x, g):
    ms = jnp.mean(x * x, axis=-1, keepdims=True)
    return x * lax.rsqrt(ms + RMS_EPS) * g


def _rotate(x, cos, sin_signed):
    return x * cos + pltpu.roll(x, RET_HEAD_DIM // 2, 1) * sin_signed


def _head_norm_gate(o, gate, gn_g, gn_b):
    mu = jnp.mean(o, axis=-1, keepdims=True)
    d = o - mu
    var = jnp.mean(d * d, axis=-1, keepdims=True)
    on = d * lax.rsqrt(var + GN_EPS) * gn_g + gn_b
    return gate * jax.nn.sigmoid(gate) * on


def _layer_spec(arr, layer):
    tail = (0,) * (arr.ndim - 1)
    return pl.BlockSpec((None,) + arr.shape[1:], lambda *_: (layer,) + tail)


def _whole_spec(arr):
    zeros = (0,) * arr.ndim
    return pl.BlockSpec(arr.shape, lambda *_: zeros)


def _cast_specs(src, layer, steps, step_of):
    _, rows, cols = src.shape
    assert rows % (steps * 2 * SUBLANES) == 0
    blk = rows // steps
    in_spec = pl.BlockSpec((None, blk, cols), lambda *g: (layer, step_of(*g), 0))
    out_spec = pl.BlockSpec((blk, cols), lambda *g: (step_of(*g), 0))
    return in_spec, out_spec, jax.ShapeDtypeStruct((rows, cols), BF16)


def _stacked_outputs(body, n_in, prev):
    prev = tuple(prev)

    def kernel(*refs):
        body(*refs[:n_in], *refs[n_in + len(prev):])

    specs = [pl.BlockSpec(memory_space=pl.ANY)] * len(prev)
    aliases = {n_in + k: 1 + k for k in range(len(prev))}
    return kernel, specs, aliases, prev


def _project_in(x, g1_ref, win_ref):
    h = _rmsnorm(x, g1_ref[...]).astype(BF16)
    P, R = POOL_WIDTH, RET_WIDTH
    return [_dot(h, win_ref[:, lo:lo + w])
            for lo, w in ((0, P), (P, R), (P + R, R), (P + 2 * R, R), (P + 3 * R, R))]


def _prompt_mixer_kernel(x_ref, g1_ref, win_ref, poolw_ref, pscale_ref, gng_ref, gnb_ref,
                         wout_ref, cq_ref, sq_ref, ck_ref, sk_ref, qd_ref, kd_ref, cd_ref,
                         dmask_ref, wup32_ref, wdown32_ref,
                         x1_ref, pool_out_ref, s_ref, wup16_ref, wdown16_ref,
                         ubuf, h_scr, q_scr, k_scr, v_scr, g_scr, mix_scr, *, tile, sub):
    t = pl.program_id(1)

    wup16_ref[...] = wup32_ref[...].astype(BF16)
    wdown16_ref[...] = wdown32_ref[...].astype(BF16)

    @pl.when(t == 0)
    def _():
        ubuf[0:POOL_PAD, :] = jnp.zeros((POOL_PAD, POOL_WIDTH), F32)
        s_ref[...] = jnp.zeros_like(s_ref)

    P, R = POOL_WIDTH, RET_WIDTH

    def project_items(r0):
        rows = slice(r0, r0 + sub)

        def norm():
            h_scr[...] = _rmsnorm(x_ref[0, rows, :], g1_ref[...]).astype(BF16)

        def column_group(lo, dst, dst_rows):
            def run():
                dst[dst_rows, :] = _dot(h_scr[...], win_ref[:, lo:lo + R])
            return run

        return [norm,
                column_group(0, ubuf, slice(POOL_PAD + r0, POOL_PAD + r0 + sub)),
                column_group(P, q_scr, rows),
                column_group(P + R, k_scr, rows),
                column_group(P + 2 * R, v_scr, rows),
                column_group(P + 3 * R, g_scr, rows)]

    def mix_items(r0):
        base = POOL_PAD + r0
        items = []

        def pool_group(gi, win):
            def run():
                sl = slice(gi * POOL_GROUP_WIDTH, (gi + 1) * POOL_GROUP_WIDTH)
                pos = t * tile + r0 + lax.broadcasted_iota(jnp.int32, (sub, 1), 0)
                cur = ubuf[base:base + sub, sl]
                s = cur
                for j in range(1, win):
                    s = s + ubuf[base - j:base - j + sub, sl]
                cnt = jnp.minimum(pos + 1, win).astype(F32)
                pooled = s / cnt - cur
                y = _dot(pooled.astype(BF16), poolw_ref[gi]) * pscale_ref[:, sl]
                mix_scr[r0:r0 + sub, sl] = y.astype(BF16)
            return run

        live = {}

        def scores(c, hh):
            def run():
                rows = slice(r0 + c * RET_CHUNK, r0 + (c + 1) * RET_CHUNK)
                ls = slice(hh * RET_HEAD_DIM, (hh + 1) * RET_HEAD_DIM)
                qr = _rotate(q_scr[rows, ls], cq_ref[rows, :], sq_ref[rows, :])
                kr = _rotate(k_scr[rows, ls], ck_ref[rows, :], sk_ref[rows, :])
                sc = _dot_nt(qr.astype(BF16), kr.astype(BF16)) * dmask_ref[hh]
                lhs = jnp.concatenate(
                    [sc.astype(BF16), (qr * qd_ref[:, ls]).astype(BF16)], axis=1)
                kdk = (kr * kd_ref[:, ls]).astype(BF16)
                live[(c, hh)] = (lhs, kdk)
            return run

        def outputs(c, hh):
            def run():
                rows = slice(r0 + c * RET_CHUNK, r0 + (c + 1) * RET_CHUNK)
                ls = slice(hh * RET_HEAD_DIM, (hh + 1) * RET_HEAD_DIM)
                lhs, kdk = live.pop((c, hh))
                vb = v_scr[rows, ls].astype(BF16)
                state = s_ref[0, hh]
                rhs = jnp.concatenate([vb, state.astype(BF16)], axis=0)
                live[(c, hh, "o")] = _dot(lhs, rhs)
                s_ref[0, hh] = state * cd_ref[:, ls] + _dot_tn(kdk, vb)
            return run

        def normalise(c, hh):
            def run():
                rows = slice(r0 + c * RET_CHUNK, r0 + (c + 1) * RET_CHUNK)
                ls = slice(hh * RET_HEAD_DIM, (hh + 1) * RET_HEAD_DIM)
                o = live.pop((c, hh, "o"))
                ret = _head_norm_gate(o, g_scr[rows, ls], gng_ref[:, ls], gnb_ref[:, ls])
                mix_scr[rows, P + hh * RET_HEAD_DIM:P + (hh + 1) * RET_HEAD_DIM] = (
                    ret.astype(BF16))
            return run

        def out_project(lo, width):
            def run():
                rows = slice(r0, r0 + sub)
                x1_ref[0, rows, lo:lo + width] = (
                    x_ref[0, rows, lo:lo + width]
                    + _dot(mix_scr[rows, :], wout_ref[:, lo:lo + width]))
            return run

        heads = [(c, hh) for c in range(sub // RET_CHUNK) for hh in range(RET_HEADS)]
        stages = (scores, outputs, normalise)
        pools = [pool_group(gi, win) for gi, win in enumerate(POOL_WINDOWS)]
        for n in range(len(heads) + len(stages) - 1):
            for lag, stage in enumerate(stages):
                if 0 <= n - lag < len(heads):
                    items.append(stage(*heads[n - lag]))
            if n % 2 == 1 and pools:
                items.append(pools.pop(0))
        items.extend(pools)
        half = D_MODEL // 2
        items.extend([out_project(0, half), out_project(half, half)])
        return items

    def interleave(major, minor):
        out, taken = [], 0
        for n, item in enumerate(major):
            out.append(item)
            want = (n + 1) * len(minor) // len(major)
            out.extend(minor[taken:want])
            taken = want
        return out

    schedule = project_items(0)
    for r0 in range(0, tile, sub):
        nxt = project_items(r0 + sub) if r0 + sub < tile else []
        schedule += interleave(mix_items(r0), nxt)
    for item in schedule:
        item()

    pool_out_ref[0] = ubuf[tile + 1:tile + POOL_PAD, :]
    ubuf[0:POOL_PAD, :] = ubuf[tile:tile + POOL_PAD, :]


def _prompt_mixer(x, w, big, layer, tabs, prev, ffn_w32):
    B, T, D = x.shape
    depth = w["g1"].shape[0]
    tile = PROMPT_TILE
    assert T % tile == 0 and tile % RET_CHUNK == 0
    const2 = lambda b, t: (0, 0)
    const3 = lambda b, t: (0, 0, 0)
    tab_spec = pl.BlockSpec((tile, RET_HEAD_DIM), lambda b, t: (t, 0))
    in_specs = [
        pl.BlockSpec((1, tile, D), lambda b, t: (b, t, 0)),
        _layer_spec(w["g1"], layer),
        _whole_spec(big["w_in"]),
        _layer_spec(w["pool_w"], layer),
        _layer_spec(w["pool_scale"], layer),
        _layer_spec(w["gn_g"], layer),
        _layer_spec(w["gn_b"], layer),
        _whole_spec(big["w_out"]),
        tab_spec, tab_spec, tab_spec, tab_spec,
        pl.BlockSpec((RET_CHUNK, RET_WIDTH), const2),
        pl.BlockSpec((RET_CHUNK, RET_WIDTH), const2),
        pl.BlockSpec((1, RET_WIDTH), const2),
        pl.BlockSpec((RET_HEADS, RET_CHUNK, RET_CHUNK), const3),
    ]
    n_t = T // tile
    casts = [_cast_specs(src, layer, B * n_t, lambda b, t: b * n_t + t) for src in ffn_w32]
    in_specs += [c[0] for c in casts]
    out_shape = (
        jax.ShapeDtypeStruct((B, T, D), F32),
        jax.ShapeDtypeStruct((depth, B, POOL_CTX, POOL_WIDTH), F32),
        jax.ShapeDtypeStruct((depth, B, RET_HEADS, RET_HEAD_DIM, RET_HEAD_DIM), F32),
    ) + tuple(c[2] for c in casts)
    out_specs = (
        pl.BlockSpec((1, tile, D), lambda b, t: (b, t, 0)),
        pl.BlockSpec((None, 1, POOL_CTX, POOL_WIDTH), lambda b, t: (layer, b, 0, 0)),
        pl.BlockSpec((None, 1, RET_HEADS, RET_HEAD_DIM, RET_HEAD_DIM),
                     lambda b, t: (layer, b, 0, 0, 0)),
    ) + tuple(c[1] for c in casts)
    body = functools.partial(_prompt_mixer_kernel, tile=tile, sub=PROMPT_SUB)
    kernel_fn, alias_specs, aliases, alias_args = _stacked_outputs(body, len(in_specs), prev)
    scratch = [
        pltpu.VMEM((POOL_PAD + tile, POOL_WIDTH), F32),
        pltpu.VMEM((PROMPT_SUB, D_MODEL), BF16),
        pltpu.VMEM((tile, RET_WIDTH), F32),
        pltpu.VMEM((tile, RET_WIDTH), F32),
        pltpu.VMEM((tile, RET_WIDTH), F32),
        pltpu.VMEM((tile, RET_WIDTH), F32),
        pltpu.VMEM((tile, D_MODEL), BF16),
    ]
    return pl.pallas_call(
        kernel_fn,
        grid=(B, T // tile),
        in_specs=in_specs + alias_specs,
        out_specs=out_specs,
        out_shape=out_shape,
        scratch_shapes=scratch,
        input_output_aliases=aliases,
        compiler_params=pltpu.CompilerParams(
            dimension_semantics=("arbitrary", "arbitrary"),
            vmem_limit_bytes=VMEM_LIMIT_BYTES),
        name="prompt_mixer",
    )(x, w["g1"], big["w_in"], w["pool_w"], w["pool_scale"], w["gn_g"], w["gn_b"],
      big["w_out"], tabs["cq"], tabs["sq"], tabs["ck"], tabs["sk"], tabs["qd"], tabs["kd"],
      tabs["cd"], tabs["dmask"], *ffn_w32, *alias_args)


def _sample_mixer_kernel(x_ref, ctx_ref, s_in_ref, g1_ref, win_ref, poolw_ref, pscale_ref,
                         gng_ref, gnb_ref, wout_ref, cq_ref, sq_ref, ck_ref, sk_ref, qd_ref,
                         kd_ref, cd_ref, dmask_ref,
                         x1_ref, pool_out_ref, s_out_ref,
                         ue, qs_scr, ks_scr, v_scr, o_scr, *, nb, seq):
    rows = nb * seq
    x = x_ref[...]
    u, q, k, v, g = _project_in(x, g1_ref, win_ref)

    ue[:, 1:POOL_PAD, :] = ctx_ref[...]
    ue[:, POOL_PAD:POOL_PAD + seq, :] = u.reshape(nb, seq, POOL_WIDTH)
    pos = PAST_LEN + lax.broadcasted_iota(jnp.int32, (1, seq, 1), 1)
    mix_parts = []
    for gi, win in enumerate(POOL_WINDOWS):
        sl = slice(gi * POOL_GROUP_WIDTH, (gi + 1) * POOL_GROUP_WIDTH)
        cur = ue[:, POOL_PAD:POOL_PAD + seq, sl]
        s = cur
        for j in range(1, win):
            s = s + ue[:, POOL_PAD - j:POOL_PAD - j + seq, sl]
        cnt = jnp.minimum(pos + 1, win).astype(F32)
        pooled = (s / cnt - cur).reshape(rows, POOL_GROUP_WIDTH)
        y = _dot(pooled.astype(BF16), poolw_ref[gi]) * pscale_ref[:, sl]
        mix_parts.append(y.astype(BF16))
    pool_out_ref[...] = ue[:, seq + 1:seq + POOL_PAD, :]

    cq, sq, ck, sk = cq_ref[...], sq_ref[...], ck_ref[...], sk_ref[...]
    for hh in range(RET_HEADS):
        ls = slice(hh * RET_HEAD_DIM, (hh + 1) * RET_HEAD_DIM)
        qr = _rotate(q[:, ls], cq, sq)
        kr = _rotate(k[:, ls], ck, sk)
        vb = v[:, ls].astype(BF16)
        sc = _dot_nt(qr.astype(BF16), kr.astype(BF16)) * dmask_ref[hh]
        o_scr[:, ls] = _dot(sc.astype(BF16), vb)
        qs_scr[:, ls] = qr * qd_ref[:, ls]
        ks_scr[:, ls] = kr * kd_ref[:, ls]
    v_scr[...] = v

    def per_sequence(b, carry):
        r0 = pl.multiple_of(b * seq, seq)
        for hh in range(RET_HEADS):
            ls = slice(hh * RET_HEAD_DIM, (hh + 1) * RET_HEAD_DIM)
            state = s_in_ref[b, hh]
            qb = qs_scr[pl.ds(r0, seq), ls].astype(BF16)
            kb = ks_scr[pl.ds(r0, seq), ls].astype(BF16)
            vb = v_scr[pl.ds(r0, seq), ls].astype(BF16)
            o_scr[pl.ds(r0, seq), ls] = o_scr[pl.ds(r0, seq), ls] + _dot(qb, state.astype(BF16))
            s_out_ref[b, hh] = state * cd_ref[:, ls] + _dot_tn(kb, vb)
        return carry

    lax.fori_loop(0, nb, per_sequence, 0, unroll=8)

    for hh in range(RET_HEADS):
        ls = slice(hh * RET_HEAD_DIM, (hh + 1) * RET_HEAD_DIM)
        ret = _head_norm_gate(o_scr[:, ls], g[:, ls], gng_ref[:, ls], gnb_ref[:, ls])
        mix_parts.append(ret.astype(BF16))

    mix = jnp.concatenate(mix_parts, axis=-1)
    x1_ref[...] = x + _dot(mix, wout_ref[...])


def _sample_mixer(x2d, state_pool, state_ret, w, big, layer, tabs, seq, prev):
    nb = SAMPLE_BATCH_TILE
    rows = nb * seq
    n_seq = state_pool.shape[1]
    assert n_seq % nb == 0 and seq == SUBLANES
    const2 = lambda i: (0, 0)
    const3 = lambda i: (0, 0, 0)
    in_specs = [
        pl.BlockSpec((rows, D_MODEL), lambda i: (i, 0)),
        pl.BlockSpec((None, nb, POOL_CTX, POOL_WIDTH), lambda i: (layer, i, 0, 0)),
        pl.BlockSpec((None, nb, RET_HEADS, RET_HEAD_DIM, RET_HEAD_DIM),
                     lambda i: (layer, i, 0, 0, 0)),
        _layer_spec(w["g1"], layer),
        _whole_spec(big["w_in"]),
        _layer_spec(w["pool_w"], layer),
        _layer_spec(w["pool_scale"], layer),
        _layer_spec(w["gn_g"], layer),
        _layer_spec(w["gn_b"], layer),
        _whole_spec(big["w_out"]),
        pl.BlockSpec((rows, RET_HEAD_DIM), const2),
        pl.BlockSpec((rows, RET_HEAD_DIM), const2),
        pl.BlockSpec((rows, RET_HEAD_DIM), const2),
        pl.BlockSpec((rows, RET_HEAD_DIM), const2),
        pl.BlockSpec((rows, RET_WIDTH), const2),
        pl.BlockSpec((rows, RET_WIDTH), const2),
        pl.BlockSpec((1, RET_WIDTH), const2),
        pl.BlockSpec((RET_HEADS, rows, rows), const3),
    ]
    out_shape = (
        jax.ShapeDtypeStruct(x2d.shape, F32),
        jax.ShapeDtypeStruct(state_pool.shape, F32),
        jax.ShapeDtypeStruct(state_ret.shape, F32),
    )
    out_specs = (
        pl.BlockSpec((rows, D_MODEL), lambda i: (i, 0)),
        pl.BlockSpec((None, nb, POOL_CTX, POOL_WIDTH), lambda i: (layer, i, 0, 0)),
        pl.BlockSpec((None, nb, RET_HEADS, RET_HEAD_DIM, RET_HEAD_DIM),
                     lambda i: (layer, i, 0, 0, 0)),
    )
    body = functools.partial(_sample_mixer_kernel, nb=nb, seq=seq)
    kernel_fn, alias_specs, aliases, alias_args = _stacked_outputs(body, len(in_specs), prev)
    scratch = [
        pltpu.VMEM((nb, POOL_PAD + seq, POOL_WIDTH), F32),
        pltpu.VMEM((rows, RET_WIDTH), F32),
        pltpu.VMEM((rows, RET_WIDTH), F32),
        pltpu.VMEM((rows, RET_WIDTH), F32),
        pltpu.VMEM((rows, RET_WIDTH), F32),
    ]
    return pl.pallas_call(
        kernel_fn,
        grid=(n_seq // nb,),
        in_specs=in_specs + alias_specs,
        out_specs=out_specs,
        out_shape=out_shape,
        scratch_shapes=scratch,
        input_output_aliases=aliases,
        compiler_params=pltpu.CompilerParams(
            dimension_semantics=("arbitrary",),
            vmem_limit_bytes=VMEM_LIMIT_BYTES),
        name="sample_mixer",
    )(x2d, state_pool, state_ret, w["g1"], big["w_in"], w["pool_w"], w["pool_scale"], w["gn_g"],
      w["gn_b"], big["w_out"], tabs["cq"], tabs["sq"], tabs["ck"], tabs["sk"], tabs["qd"],
      tabs["kd"], tabs["cd"], tabs["dmask"], *alias_args)


def _ffn_kernel(xp_ref, xs_ref, g2_ref, wup_ref, wdown_ref, fg_ref, *rest, final_norm):
    if final_norm:
        op_ref, os_ref = rest
    else:
        win32_ref, wout32_ref, op_ref, os_ref, win16_ref, wout16_ref = rest
        win16_ref[...] = win32_ref[...].astype(BF16)
        wout16_ref[...] = wout32_ref[...].astype(BF16)
    n_p = xp_ref.shape[0]
    xp, xs = xp_ref[...], xs_ref[...]
    h = jnp.concatenate([_rmsnorm(xp, g2_ref[...]).astype(BF16),
                         _rmsnorm(xs, g2_ref[...]).astype(BF16)], axis=0)
    acc = None
    for c in range(D_FF // FFN_CHUNK):
        cols = slice(c * FFN_CHUNK, (c + 1) * FFN_CHUNK)
        a = jnp.square(jnp.maximum(_dot(h, wup_ref[:, cols]), 0.0)).astype(BF16)
        d = _dot(a, wdown_ref[cols, :])
        acc = d if acc is None else acc + d
    yp, ys = xp + acc[:n_p], xs + acc[n_p:]
    if final_norm:
        yp, ys = _rmsnorm(yp, fg_ref[...]), _rmsnorm(ys, fg_ref[...])
    op_ref[...] = yp
    os_ref[...] = ys


def _ffn(xp2d, xs2d, w, big, layer, final_g, mixer_w32):
    final_norm = not mixer_w32
    n, d = xp2d.shape
    tile = FFN_TILE
    steps = n // tile
    tile_s = xs2d.shape[0] // steps
    assert n % tile == 0 and xs2d.shape[0] == steps * tile_s and tile_s % (2 * SUBLANES) == 0
    const2 = lambda i: (0, 0)
    casts = [_cast_specs(src, layer + 1, steps, lambda i: i) for src in mixer_w32]
    return pl.pallas_call(
        functools.partial(_ffn_kernel, final_norm=final_norm),
        grid=(steps,),
        in_specs=[
            pl.BlockSpec((tile, d), lambda i: (i, 0)),
            pl.BlockSpec((tile_s, d), lambda i: (i, 0)),
            _layer_spec(w["g2"], layer),
            _whole_spec(big["w_up"]),
            _whole_spec(big["w_down"]),
            pl.BlockSpec((1, d), const2),
        ] + [c[0] for c in casts],
        out_specs=(pl.BlockSpec((tile, d), lambda i: (i, 0)),
                   pl.BlockSpec((tile_s, d), lambda i: (i, 0))) + tuple(c[1] for c in casts),
        out_shape=(jax.ShapeDtypeStruct((n, d), F32),
                   jax.ShapeDtypeStruct(xs2d.shape, F32)) + tuple(c[2] for c in casts),
        compiler_params=pltpu.CompilerParams(
            dimension_semantics=("arbitrary",),
            vmem_limit_bytes=VMEM_LIMIT_BYTES),
        name="ffn",
    )(xp2d, xs2d, w["g2"], big["w_up"], big["w_down"], final_g, *mixer_w32)


def _retention_tables(pos, chunk, reps):
    half = RET_HEAD_DIM // 2
    inv = ROPE_BASE ** (-jnp.arange(half, dtype=F32) / half)
    ang = pos.astype(F32)[:, None] * inv[None, :]
    cos, sin = jnp.cos(ang), jnp.sin(ang)
    cq = jnp.concatenate([cos, cos], axis=-1)
    sq = jnp.concatenate([-sin, sin], axis=-1)
    kscale = RET_HEAD_DIM ** -0.5

    gamma = 1.0 - jnp.exp2(-5.0 - jnp.arange(RET_HEADS, dtype=F32))
    lg = jnp.log(gamma)
    i = jnp.arange(chunk, dtype=F32)
    diff = i[:, None] - i[None, :]
    dmask = jnp.where(diff >= 0, jnp.exp(lg[:, None, None] * jnp.maximum(diff, 0.0)), 0.0)
    q_decay = jnp.exp(lg[:, None] * (i + 1.0))
    k_decay = jnp.exp(lg[:, None] * (chunk - 1.0 - i))
    c_decay = jnp.exp(lg * chunk)

    def lanes(a):
        a = jnp.repeat(a.T[:, :, None], RET_HEAD_DIM, axis=2).reshape(chunk, RET_WIDTH)
        return jnp.tile(a, (reps, 1))

    if reps > 1:
        eye = jnp.eye(reps, dtype=F32)
        dmask = jnp.einsum("ab,hij->haibj", eye, dmask).reshape(
            RET_HEADS, reps * chunk, reps * chunk)
    return {
        "cq": cq, "sq": sq, "ck": cq * kscale, "sk": sq * kscale,
        "qd": lanes(q_decay), "kd": lanes(k_decay),
        "cd": jnp.repeat(c_decay, RET_HEAD_DIM)[None, :],
        "dmask": dmask,
    }


def kernel(x_prompt, x_sample, state_pool, state_ret, norm1_g, w_in, pool_w, pool_scale,
           gn_g, gn_b, w_out, norm2_g, w_up, w_down, final_g):
    depth = w_in.shape[0]
    B, T, D = x_prompt.shape
    NB, TS, _ = x_sample.shape

    tabs_p = _retention_tables(jnp.arange(T, dtype=jnp.int32), RET_CHUNK, 1)
    pos_s = PAST_LEN + jnp.arange(TS, dtype=jnp.int32)
    tabs_s = _retention_tables(jnp.tile(pos_s, SAMPLE_BATCH_TILE), TS, SAMPLE_BATCH_TILE)

    yp = x_prompt
    ys = x_sample.reshape(NB * TS, D)
    fg = final_g.reshape(1, D)
    w = {
        "g1": norm1_g.reshape(depth, 1, D),
        "pool_w": pool_w.astype(BF16),
        "pool_scale": pool_scale.reshape(depth, 1, POOL_WIDTH),
        "gn_g": gn_g.reshape(depth, 1, RET_WIDTH),
        "gn_b": gn_b.reshape(depth, 1, RET_WIDTH),
        "g2": norm2_g.reshape(depth, 1, D),
    }
    big = {"w_in": w_in[0].astype(BF16), "w_out": w_out[0].astype(BF16)}
    states_p, states_s = (), ()
    for l in range(depth):
        x1p, pool_p, ret_p, big["w_up"], big["w_down"] = _prompt_mixer(
            yp, w, big, l, tabs_p, states_p, (w_up, w_down))
        states_p = (pool_p, ret_p)
        x1s, *states_s = _sample_mixer(ys, state_pool, state_ret, w, big, l, tabs_s, TS,
                                       states_s)
        nxt = (w_in, w_out) if l + 1 < depth else ()
        yp, ys, *nxt16 = _ffn(x1p.reshape(B * T, D), x1s, w, big, l, fg, nxt)
        if nxt16:
            big["w_in"], big["w_out"] = nxt16
        yp = yp.reshape(B, T, D)
    return (yp, ys.reshape(NB, TS, D), states_p[0], states_p[1], states_s[0], states_s[1])
```

```python
import functools

import jax
import jax.numpy as jnp
from jax import lax
from jax.experimental import pallas as pl
from jax.experimental.pallas import tpu as pltpu

D_MODEL = 1024
POOL_WIDTH = D_MODEL // 2
POOL_WINDOWS = (2, 4, 8, 16)
POOL_GROUP_WIDTH = POOL_WIDTH // len(POOL_WINDOWS)
POOL_CTX = max(POOL_WINDOWS) - 1
POOL_PAD = POOL_CTX + 1
RET_WIDTH = D_MODEL - POOL_WIDTH
RET_HEADS = 4
RET_HEAD_DIM = RET_WIDTH // RET_HEADS
RET_CHUNK = 128
ROPE_BASE = 10000.0
D_FF = 4 * D_MODEL
RMS_EPS = 1e-6
GN_EPS = 1e-5
PAST_LEN = 16384

SUBLANES = 8
VMEM_LIMIT_BYTES = 56 * 1024 * 1024

PROMPT_TILE = 1024
PROMPT_SUB = 256
FFN_TILE = 512
FFN_CHUNK = 1024

F32 = jnp.float32
BF16 = jnp.bfloat16


def _dot(a, b):
    return jnp.dot(a, b, preferred_element_type=F32)


def _dot_nt(a, b):
    return lax.dot_general(a, b, (((1,), (1,)), ((), ())), preferred_element_type=F32)


def _dot_tn(a, b):
    return lax.dot_general(a, b, (((0,), (0,)), ((), ())), preferred_element_type=F32)


def _rmsnorm(x, g):
    ms = jnp.mean(x * x, axis=-1, keepdims=True)
    return x * lax.rsqrt(ms + RMS_EPS) * g


def _rotate(x, cos, sin_signed):
    return x * cos + pltpu.roll(x, RET_HEAD_DIM // 2, 1) * sin_signed


def _head_norm_gate(o, gate, gn_g, gn_b):
    mu = jnp.mean(o, axis=-1, keepdims=True)
    d = o - mu
    var = jnp.mean(d * d, axis=-1, keepdims=True)
    on = d * lax.rsqrt(var + GN_EPS) * gn_g + gn_b
    return gate * jax.nn.sigmoid(gate) * on


def _head(hh):
    return slice(hh * RET_HEAD_DIM, (hh + 1) * RET_HEAD_DIM)


def _whole_spec(arr):
    zeros = (0,) * arr.ndim
    return pl.BlockSpec(arr.shape, lambda *_: zeros)


def _layer_spec(arr, layer):
    tail = (0,) * (arr.ndim - 1)
    return pl.BlockSpec((None,) + arr.shape[1:], lambda *_: (layer,) + tail)


def _cast_specs(src, layer, steps, step_of):
    _, rows, cols = src.shape
    assert rows % (steps * 2 * SUBLANES) == 0
    blk = rows // steps
    in_spec = pl.BlockSpec((None, blk, cols), lambda *g: (layer, step_of(*g), 0))
    out_spec = pl.BlockSpec((blk, cols), lambda *g: (step_of(*g), 0))
    return in_spec, out_spec, jax.ShapeDtypeStruct((rows, cols), BF16)


def _stacked_outputs(body, n_in, prev):
    prev = tuple(prev)

    def kernel(*refs):
        body(*refs[:n_in], *refs[n_in + len(prev):])

    specs = [pl.BlockSpec(memory_space=pl.ANY)] * len(prev)
    aliases = {n_in + k: 1 + k for k in range(len(prev))}
    return kernel, specs, aliases, prev


def _interleave(major, minor):
    out, taken = [], 0
    for n, item in enumerate(major):
        out.append(item)
        want = (n + 1) * len(minor) // len(major)
        out.extend(minor[taken:want])
        taken = want
    return out


def _mixer_kernel(x_ref, xs_ref, ctx_ref, s_in_ref,
                  g1_ref, win_ref, poolw_ref, pscale_ref, gng_ref, gnb_ref, wout_ref,
                  cq_ref, sq_ref, ck_ref, sk_ref, qd_ref, kd_ref, cd_ref, dmask_ref,
                  cqs_ref, sqs_ref, cks_ref, sks_ref, qds_ref, kds_ref, cds_ref, dmasks_ref,
                  wup32_ref, wdown32_ref,
                  x1_ref, pool_out_ref, s_ref, pools_out_ref, ss_out_ref, x1s_ref,
                  wup16_ref, wdown16_ref,
                  ubuf, h_scr, q_scr, k_scr, v_scr, g_scr, mix_scr,
                  ue, qraw_s, kraw_s, v_s, g_s, qdec_s, kdec_s, o_s,
                  *, tile, sub, nb, seq):
    t = pl.program_id(1)
    P, R = POOL_WIDTH, RET_WIDTH
    rows_s = nb * seq
    last_r0 = tile - sub

    wup16_ref[...] = wup32_ref[...].astype(BF16)
    wdown16_ref[...] = wdown32_ref[...].astype(BF16)

    @pl.when(t == 0)
    def _():
        ubuf[0:POOL_PAD, :] = jnp.zeros((POOL_PAD, POOL_WIDTH), F32)
        s_ref[...] = jnp.zeros_like(s_ref)

    def project_items(r0):
        rows = slice(r0, r0 + sub)
        n_rows = sub + rows_s if r0 == 0 else sub

        def norm():
            h_scr[0:sub, :] = _rmsnorm(x_ref[0, rows, :], g1_ref[...]).astype(BF16)
            if r0 == 0:
                h_scr[sub:n_rows, :] = _rmsnorm(xs_ref[...], g1_ref[...]).astype(BF16)

        def column_group(lo, dst, dst_rows, put_sample):
            def run():
                z = _dot(h_scr[0:n_rows, :], win_ref[:, lo:lo + R])
                dst[dst_rows, :] = z[0:sub]
                if r0 == 0:
                    put_sample(z[sub:n_rows])
            return run

        def put_u(z):
            ue[:, POOL_PAD:POOL_PAD + seq, :] = z.reshape(nb, seq, POOL_WIDTH)

        def put(dst):
            def run(z):
                dst[...] = z
            return run

        return [norm,
                column_group(0, ubuf, slice(POOL_PAD + r0, POOL_PAD + r0 + sub), put_u),
                column_group(P, q_scr, rows, put(qraw_s)),
                column_group(P + R, k_scr, rows, put(kraw_s)),
                column_group(P + 2 * R, v_scr, rows, put(v_s)),
                column_group(P + 3 * R, g_scr, rows, put(g_s))]

    def mix_items(r0):
        base = POOL_PAD + r0
        items = []

        def pool_group(gi, win):
            def run():
                sl = slice(gi * POOL_GROUP_WIDTH, (gi + 1) * POOL_GROUP_WIDTH)
                pos = t * tile + r0 + lax.broadcasted_iota(jnp.int32, (sub, 1), 0)
                cur = ubuf[base:base + sub, sl]
                s = cur
                for j in range(1, win):
                    s = s + ubuf[base - j:base - j + sub, sl]
                cnt = jnp.minimum(pos + 1, win).astype(F32)
                pooled = s / cnt - cur
                y = _dot(pooled.astype(BF16), poolw_ref[gi]) * pscale_ref[:, sl]
                mix_scr[r0:r0 + sub, sl] = y.astype(BF16)
            return run

        live = {}

        def scores(c, hh):
            def run():
                rows = slice(r0 + c * RET_CHUNK, r0 + (c + 1) * RET_CHUNK)
                ls = _head(hh)
                qr = _rotate(q_scr[rows, ls], cq_ref[rows, :], sq_ref[rows, :])
                kr = _rotate(k_scr[rows, ls], ck_ref[rows, :], sk_ref[rows, :])
                sc = _dot_nt(qr.astype(BF16), kr.astype(BF16)) * dmask_ref[hh]
                lhs = jnp.concatenate(
                    [sc.astype(BF16), (qr * qd_ref[:, ls]).astype(BF16)], axis=1)
                kdk = (kr * kd_ref[:, ls]).astype(BF16)
                live[(c, hh)] = (lhs, kdk)
            return run

        def outputs(c, hh):
            def run():
                rows = slice(r0 + c * RET_CHUNK, r0 + (c + 1) * RET_CHUNK)
                ls = _head(hh)
                lhs, kdk = live.pop((c, hh))
                vb = v_scr[rows, ls].astype(BF16)
                state = s_ref[0, hh]
                rhs = jnp.concatenate([vb, state.astype(BF16)], axis=0)
                live[(c, hh, "o")] = _dot(lhs, rhs)
                s_ref[0, hh] = state * cd_ref[:, ls] + _dot_tn(kdk, vb)
            return run

        def normalise(c, hh):
            def run():
                rows = slice(r0 + c * RET_CHUNK, r0 + (c + 1) * RET_CHUNK)
                ls = _head(hh)
                o = live.pop((c, hh, "o"))
                ret = _head_norm_gate(o, g_scr[rows, ls], gng_ref[:, ls], gnb_ref[:, ls])
                mix_scr[rows, P + hh * RET_HEAD_DIM:P + (hh + 1) * RET_HEAD_DIM] = (
                    ret.astype(BF16))
            return run

        def out_project(lo, width):
            def run():
                n_rows = sub + rows_s if r0 == last_r0 else sub
                y = _dot(mix_scr[r0:r0 + n_rows, :], wout_ref[:, lo:lo + width])
                x1_ref[0, r0:r0 + sub, lo:lo + width] = (
                    x_ref[0, r0:r0 + sub, lo:lo + width] + y[0:sub])
                if r0 == last_r0:
                    x1s_ref[:, lo:lo + width] = xs_ref[:, lo:lo + width] + y[sub:n_rows]
            return run

        heads = [(c, hh) for c in range(sub // RET_CHUNK) for hh in range(RET_HEADS)]
        stages = (scores, outputs, normalise)
        pools = [pool_group(gi, win) for gi, win in enumerate(POOL_WINDOWS)]
        for n in range(len(heads) + len(stages) - 1):
            for lag, stage in enumerate(stages):
                if 0 <= n - lag < len(heads):
                    items.append(stage(*heads[n - lag]))
            if n % 2 == 1 and pools:
                items.append(pools.pop(0))
        items.extend(pools)
        half = D_MODEL // 2
        return items, [out_project(0, half), out_project(half, half)]

    def sample_items():
        srows = slice(tile, tile + rows_s)

        def stage_context():
            ue[:, 1:POOL_PAD, :] = ctx_ref[...]

        def pool_group(gi, win):
            def run():
                sl = slice(gi * POOL_GROUP_WIDTH, (gi + 1) * POOL_GROUP_WIDTH)
                pos = PAST_LEN + lax.broadcasted_iota(jnp.int32, (1, seq, 1), 1)
                cur = ue[:, POOL_PAD:POOL_PAD + seq, sl]
                s = cur
                for j in range(1, win):
                    s = s + ue[:, POOL_PAD - j:POOL_PAD - j + seq, sl]
                cnt = jnp.minimum(pos + 1, win).astype(F32)
                pooled = (s / cnt - cur).reshape(rows_s, POOL_GROUP_WIDTH)
                y = _dot(pooled.astype(BF16), poolw_ref[gi]) * pscale_ref[:, sl]
                mix_scr[srows, sl] = y.astype(BF16)
            return run

        def new_context():
            pools_out_ref[...] = ue[:, seq + 1:seq + POOL_PAD, :]

        def scores(hh):
            def run():
                ls = _head(hh)
                qr = _rotate(qraw_s[:, ls], cqs_ref[...], sqs_ref[...])
                kr = _rotate(kraw_s[:, ls], cks_ref[...], sks_ref[...])
                sc = _dot_nt(qr.astype(BF16), kr.astype(BF16)) * dmasks_ref[hh]
                o_s[:, ls] = _dot(sc.astype(BF16), v_s[:, ls].astype(BF16))
                qdec_s[:, ls] = qr * qds_ref[:, ls]
                kdec_s[:, ls] = kr * kds_ref[:, ls]
            return run

        def state(b):
            def run():
                r = slice(b * seq, (b + 1) * seq)
                for hh in range(RET_HEADS):
                    ls = _head(hh)
                    st = s_in_ref[b, hh]
                    vb = v_s[r, ls].astype(BF16)
                    o_s[r, ls] = o_s[r, ls] + _dot(qdec_s[r, ls].astype(BF16), st.astype(BF16))
                    ss_out_ref[b, hh] = (st * cds_ref[:, ls]
                                         + _dot_tn(kdec_s[r, ls].astype(BF16), vb))
            return run

        def normalise(hh):
            def run():
                ls = _head(hh)
                ret = _head_norm_gate(o_s[:, ls], g_s[:, ls], gng_ref[:, ls], gnb_ref[:, ls])
                mix_scr[srows, P + hh * RET_HEAD_DIM:P + (hh + 1) * RET_HEAD_DIM] = (
                    ret.astype(BF16))
            return run

        return ([stage_context]
                + [pool_group(gi, win) for gi, win in enumerate(POOL_WINDOWS)]
                + [new_context]
                + [scores(hh) for hh in range(RET_HEADS)]
                + [state(b) for b in range(nb)]
                + [normalise(hh) for hh in range(RET_HEADS)])

    first = project_items(0)
    middle, closing = [], []
    for r0 in range(0, tile, sub):
        nxt = project_items(r0 + sub) if r0 + sub < tile else []
        items, out_proj = mix_items(r0)
        if r0 == last_r0:
            middle += _interleave(items, nxt)
            closing = out_proj
        else:
            middle += _interleave(items + out_proj, nxt)
    for item in first + _interleave(middle, sample_items()) + closing:
        item()

    pool_out_ref[0] = ubuf[tile + 1:tile + POOL_PAD, :]
    ubuf[0:POOL_PAD, :] = ubuf[tile:tile + POOL_PAD, :]


def _mixer(x, xs2d, state_pool, state_ret, w, big, layer, tabs, tabs_s, prev, ffn_w32):
    B, T, D = x.shape
    depth, n_seq = state_pool.shape[:2]
    tile, sub = PROMPT_TILE, PROMPT_SUB
    n_t = T // tile
    steps = B * n_t
    nb = n_seq // steps
    seq = xs2d.shape[0] // n_seq
    rows_s = nb * seq
    assert T % tile == 0 and tile % sub == 0 and sub % RET_CHUNK == 0
    assert n_seq == nb * steps and seq == SUBLANES and rows_s % (2 * SUBLANES) == 0
    step_of = lambda b, t: b * n_t + t
    tab_spec = pl.BlockSpec((tile, RET_HEAD_DIM), lambda b, t: (t, 0))
    in_specs = [
        pl.BlockSpec((1, tile, D), lambda b, t: (b, t, 0)),
        pl.BlockSpec((rows_s, D), lambda b, t: (step_of(b, t), 0)),
        pl.BlockSpec((None, nb, POOL_CTX, POOL_WIDTH), lambda b, t: (layer, step_of(b, t), 0, 0)),
        pl.BlockSpec((None, nb, RET_HEADS, RET_HEAD_DIM, RET_HEAD_DIM),
                     lambda b, t: (layer, step_of(b, t), 0, 0, 0)),
        _layer_spec(w["g1"], layer),
        _whole_spec(big["w_in"]),
        _layer_spec(w["pool_w"], layer),
        _layer_spec(w["pool_scale"], layer),
        _layer_spec(w["gn_g"], layer),
        _layer_spec(w["gn_b"], layer),
        _whole_spec(big["w_out"]),
        tab_spec, tab_spec, tab_spec, tab_spec,
        _whole_spec(tabs["qd"]), _whole_spec(tabs["kd"]), _whole_spec(tabs["cd"]),
        _whole_spec(tabs["dmask"]),
    ] + [_whole_spec(tabs_s[k]) for k in ("cq", "sq", "ck", "sk", "qd", "kd", "cd", "dmask")]
    casts = [_cast_specs(src, layer, steps, step_of) for src in ffn_w32]
    in_specs += [c[0] for c in casts]
    out_shape = (
        jax.ShapeDtypeStruct((B, T, D), F32),
        jax.ShapeDtypeStruct((depth, B, POOL_CTX, POOL_WIDTH), F32),
        jax.ShapeDtypeStruct((depth, B, RET_HEADS, RET_HEAD_DIM, RET_HEAD_DIM), F32),
        jax.ShapeDtypeStruct(state_pool.shape, F32),
        jax.ShapeDtypeStruct(state_ret.shape, F32),
        jax.ShapeDtypeStruct(xs2d.shape, F32),
    ) + tuple(c[2] for c in casts)
    out_specs = (
        pl.BlockSpec((1, tile, D), lambda b, t: (b, t, 0)),
        pl.BlockSpec((None, 1, POOL_CTX, POOL_WIDTH), lambda b, t: (layer, b, 0, 0)),
        pl.BlockSpec((None, 1, RET_HEADS, RET_HEAD_DIM, RET_HEAD_DIM),
                     lambda b, t: (layer, b, 0, 0, 0)),
        pl.BlockSpec((None, nb, POOL_CTX, POOL_WIDTH), lambda b, t: (layer, step_of(b, t), 0, 0)),
        pl.BlockSpec((None, nb, RET_HEADS, RET_HEAD_DIM, RET_HEAD_DIM),
                     lambda b, t: (layer, step_of(b, t), 0, 0, 0)),
        pl.BlockSpec((rows_s, D), lambda b, t: (step_of(b, t), 0)),
    ) + tuple(c[1] for c in casts)
    body = functools.partial(_mixer_kernel, tile=tile, sub=sub, nb=nb, seq=seq)
    kernel_fn, alias_specs, aliases, alias_args = _stacked_outputs(body, len(in_specs), prev)
    sample_tile = pltpu.VMEM((rows_s, RET_WIDTH), F32)
    scratch = [
        pltpu.VMEM((POOL_PAD + tile, POOL_WIDTH), F32),
        pltpu.VMEM((sub + rows_s, D_MODEL), BF16),
        pltpu.VMEM((tile, RET_WIDTH), F32),
        pltpu.VMEM((tile, RET_WIDTH), F32),
        pltpu.VMEM((tile, RET_WIDTH), F32),
        pltpu.VMEM((tile, RET_WIDTH), F32),
        pltpu.VMEM((tile + rows_s, D_MODEL), BF16),
        pltpu.VMEM((nb, POOL_PAD + seq, POOL_WIDTH), F32),
    ] + [sample_tile] * 7
    return pl.pallas_call(
        kernel_fn,
        grid=(B, n_t),
        in_specs=in_specs + alias_specs,
        out_specs=out_specs,
        out_shape=out_shape,
        scratch_shapes=scratch,
        input_output_aliases=aliases,
        compiler_params=pltpu.CompilerParams(
            dimension_semantics=("arbitrary", "arbitrary"),
            vmem_limit_bytes=VMEM_LIMIT_BYTES),
        name="mixer",
    )(x, xs2d, state_pool, state_ret, w["g1"], big["w_in"], w["pool_w"], w["pool_scale"],
      w["gn_g"], w["gn_b"], big["w_out"],
      tabs["cq"], tabs["sq"], tabs["ck"], tabs["sk"], tabs["qd"], tabs["kd"], tabs["cd"],
      tabs["dmask"],
      *[tabs_s[k] for k in ("cq", "sq", "ck", "sk", "qd", "kd", "cd", "dmask")],
      *ffn_w32, *alias_args)


def _ffn_kernel(xp_ref, xs_ref, g2_ref, wup_ref, wdown_ref, fg_ref, *rest, final_norm):
    if final_norm:
        op_ref, os_ref = rest
    else:
        win32_ref, wout32_ref, op_ref, os_ref, win16_ref, wout16_ref = rest
        win16_ref[...] = win32_ref[...].astype(BF16)
        wout16_ref[...] = wout32_ref[...].astype(BF16)
    n_p = xp_ref.shape[0]
    xp, xs = xp_ref[...], xs_ref[...]
    h = jnp.concatenate([_rmsnorm(xp, g2_ref[...]).astype(BF16),
                         _rmsnorm(xs, g2_ref[...]).astype(BF16)], axis=0)
    acc = None
    for c in range(D_FF // FFN_CHUNK):
        cols = slice(c * FFN_CHUNK, (c + 1) * FFN_CHUNK)
        a = jnp.square(jnp.maximum(_dot(h, wup_ref[:, cols]), 0.0)).astype(BF16)
        d = _dot(a, wdown_ref[cols, :])
        acc = d if acc is None else acc + d
    yp, ys = xp + acc[:n_p], xs + acc[n_p:]
    if final_norm:
        yp, ys = _rmsnorm(yp, fg_ref[...]), _rmsnorm(ys, fg_ref[...])
    op_ref[...] = yp
    os_ref[...] = ys


def _ffn(xp2d, xs2d, w, big, layer, final_g, mixer_w32):
    final_norm = not mixer_w32
    n, d = xp2d.shape
    tile = FFN_TILE
    steps = n // tile
    tile_s = xs2d.shape[0] // steps
    assert n % tile == 0 and xs2d.shape[0] == steps * tile_s and tile_s % (2 * SUBLANES) == 0
    const2 = lambda i: (0, 0)
    casts = [_cast_specs(src, layer + 1, steps, lambda i: i) for src in mixer_w32]
    return pl.pallas_call(
        functools.partial(_ffn_kernel, final_norm=final_norm),
        grid=(steps,),
        in_specs=[
            pl.BlockSpec((tile, d), lambda i: (i, 0)),
            pl.BlockSpec((tile_s, d), lambda i: (i, 0)),
            _layer_spec(w["g2"], layer),
            _whole_spec(big["w_up"]),
            _whole_spec(big["w_down"]),
            pl.BlockSpec((1, d), const2),
        ] + [c[0] for c in casts],
        out_specs=(pl.BlockSpec((tile, d), lambda i: (i, 0)),
                   pl.BlockSpec((tile_s, d), lambda i: (i, 0))) + tuple(c[1] for c in casts),
        out_shape=(jax.ShapeDtypeStruct((n, d), F32),
                   jax.ShapeDtypeStruct(xs2d.shape, F32)) + tuple(c[2] for c in casts),
        compiler_params=pltpu.CompilerParams(
            dimension_semantics=("arbitrary",),
            vmem_limit_bytes=VMEM_LIMIT_BYTES),
        name="ffn",
    )(xp2d, xs2d, w["g2"], big["w_up"], big["w_down"], final_g, *mixer_w32)


def _retention_tables(pos, chunk, reps):
    half = RET_HEAD_DIM // 2
    inv = ROPE_BASE ** (-jnp.arange(half, dtype=F32) / half)
    ang = pos.astype(F32)[:, None] * inv[None, :]
    cos, sin = jnp.cos(ang), jnp.sin(ang)
    cq = jnp.concatenate([cos, cos], axis=-1)
    sq = jnp.concatenate([-sin, sin], axis=-1)
    kscale = RET_HEAD_DIM ** -0.5

    gamma = 1.0 - jnp.exp2(-5.0 - jnp.arange(RET_HEADS, dtype=F32))
    lg = jnp.log(gamma)
    i = jnp.arange(chunk, dtype=F32)
    diff = i[:, None] - i[None, :]
    dmask = jnp.where(diff >= 0, jnp.exp(lg[:, None, None] * jnp.maximum(diff, 0.0)), 0.0)
    q_decay = jnp.exp(lg[:, None] * (i + 1.0))
    k_decay = jnp.exp(lg[:, None] * (chunk - 1.0 - i))
    c_decay = jnp.exp(lg * chunk)

    def lanes(a):
        a = jnp.repeat(a.T[:, :, None], RET_HEAD_DIM, axis=2).reshape(chunk, RET_WIDTH)
        return jnp.tile(a, (reps, 1))

    if reps > 1:
        eye = jnp.eye(reps, dtype=F32)
        dmask = jnp.einsum("ab,hij->haibj", eye, dmask).reshape(
            RET_HEADS, reps * chunk, reps * chunk)
    return {
        "cq": cq, "sq": sq, "ck": cq * kscale, "sk": sq * kscale,
        "qd": lanes(q_decay), "kd": lanes(k_decay),
        "cd": jnp.repeat(c_decay, RET_HEAD_DIM)[None, :],
        "dmask": dmask,
    }


def kernel(x_prompt, x_sample, state_pool, state_ret, norm1_g, w_in, pool_w, pool_scale,
           gn_g, gn_b, w_out, norm2_g, w_up, w_down, final_g):
    depth = w_in.shape[0]
    B, T, D = x_prompt.shape
    NB, TS, _ = x_sample.shape
    seqs_per_step = NB // (B * (T // PROMPT_TILE))

    tabs_p = _retention_tables(jnp.arange(T, dtype=jnp.int32), RET_CHUNK, 1)
    pos_s = PAST_LEN + jnp.arange(TS, dtype=jnp.int32)
    tabs_s = _retention_tables(jnp.tile(pos_s, seqs_per_step), TS, seqs_per_step)

    yp = x_prompt
    ys = x_sample.reshape(NB * TS, D)
    fg = final_g.reshape(1, D)
    w = {
        "g1": norm1_g.reshape(depth, 1, D),
        "pool_w": pool_w.astype(BF16),
        "pool_scale": pool_scale.reshape(depth, 1, POOL_WIDTH),
        "gn_g": gn_g.reshape(depth, 1, RET_WIDTH),
        "gn_b": gn_b.reshape(depth, 1, RET_WIDTH),
        "g2": norm2_g.reshape(depth, 1, D),
    }
    big = {"w_in": w_in[0].astype(BF16), "w_out": w_out[0].astype(BF16)}
    states = ()
    for l in range(depth):
        x1p, *states, x1s, big["w_up"], big["w_down"] = _mixer(
            yp, ys, state_pool, state_ret, w, big, l, tabs_p, tabs_s, states, (w_up, w_down))
        nxt = (w_in, w_out) if l + 1 < depth else ()
        yp, ys, *nxt16 = _ffn(x1p.reshape(B * T, D), x1s, w, big, l, fg, nxt)
        if nxt16:
            big["w_in"], big["w_out"] = nxt16
        yp = yp.reshape(B, T, D)
    pool_p, ret_p, pool_s, ret_s = states
    return (yp, ys.reshape(NB, TS, D), pool_p, ret_p, pool_s, ret_s)
```

```python
import functools

import jax
import jax.numpy as jnp
from jax import lax
from jax.experimental import pallas as pl
from jax.experimental.pallas import tpu as pltpu

D_MODEL = 1024
POOL_WIDTH = D_MODEL // 2
POOL_WINDOWS = (2, 4, 8, 16)
POOL_GROUP_WIDTH = POOL_WIDTH // len(POOL_WINDOWS)
POOL_CTX = max(POOL_WINDOWS) - 1
POOL_PAD = POOL_CTX + 1
RET_WIDTH = D_MODEL - POOL_WIDTH
RET_HEADS = 4
RET_HEAD_DIM = RET_WIDTH // RET_HEADS
RET_CHUNK = 128
ROPE_BASE = 10000.0
D_FF = 4 * D_MODEL
RMS_EPS = 1e-6
GN_EPS = 1e-5
PAST_LEN = 16384

SUBLANES = 8
VMEM_LIMIT_BYTES = 60 * 1024 * 1024

PROMPT_TILE = 1024
PROMPT_SUB = 256
RET_STAGE_LAG = 2
PROJ_COLS = 256
FFN_TILE = 512
FFN_CHUNK = 1024

F32 = jnp.float32
BF16 = jnp.bfloat16


def _dot(a, b):
    return jnp.dot(a, b, preferred_element_type=F32)


def _dot_nt(a, b):
    return lax.dot_general(a, b, (((1,), (1,)), ((), ())), preferred_element_type=F32)


def _dot_tn(a, b):
    return lax.dot_general(a, b, (((0,), (0,)), ((), ())), preferred_element_type=F32)


def _rmsnorm(x, g):
    ms = jnp.mean(x * x, axis=-1, keepdims=True)
    return x * lax.rsqrt(ms + RMS_EPS) * g


def _rotate(x, cos, sin_signed):
    return x * cos + pltpu.roll(x, RET_HEAD_DIM // 2, 1) * sin_signed


def _head_norm_gate(o, gate, gn_g, gn_b):
    mu = jnp.mean(o, axis=-1, keepdims=True)
    d = o - mu
    var = jnp.mean(d * d, axis=-1, keepdims=True)
    on = d * lax.rsqrt(var + GN_EPS) * gn_g + gn_b
    return gate * jax.nn.sigmoid(gate) * on


def _head(hh):
    return slice(hh * RET_HEAD_DIM, (hh + 1) * RET_HEAD_DIM)


def _whole_spec(arr):
    zeros = (0,) * arr.ndim
    return pl.BlockSpec(arr.shape, lambda *_: zeros)


def _layer_spec(arr, layer):
    tail = (0,) * (arr.ndim - 1)
    return pl.BlockSpec((None,) + arr.shape[1:], lambda *_: (layer,) + tail)


def _cast_specs(src, layer, steps, step_of):
    _, rows, cols = src.shape
    assert rows % (steps * 2 * SUBLANES) == 0
    blk = rows // steps
    in_spec = pl.BlockSpec((None, blk, cols), lambda *g: (layer, step_of(*g), 0))
    out_spec = pl.BlockSpec((blk, cols), lambda *g: (step_of(*g), 0))
    return in_spec, out_spec, jax.ShapeDtypeStruct((rows, cols), BF16)


def _stacked_outputs(body, n_in, prev):
    prev = tuple(prev)

    def kernel(*refs):
        body(*refs[:n_in], *refs[n_in + len(prev):])

    specs = [pl.BlockSpec(memory_space=pl.ANY)] * len(prev)
    aliases = {n_in + k: 1 + k for k in range(len(prev))}
    return kernel, specs, aliases, prev


def _interleave(major, minor):
    out, taken = [], 0
    for n, item in enumerate(major):
        out.append(item)
        want = (n + 1) * len(minor) // len(major)
        out.extend(minor[taken:want])
        taken = want
    return out


def _mixer_kernel(x_ref, xs_ref, ctx_ref, s_in_ref,
                  g1_ref, win_ref, poolw_ref, pscale_ref, gng_ref, gnb_ref, wout_ref,
                  cq_ref, sq_ref, ck_ref, sk_ref, qd_ref, kd_ref, cd_ref, dmask_ref,
                  cqs_ref, sqs_ref, cks_ref, sks_ref, qds_ref, kds_ref, cds_ref, dmasks_ref,
                  wup32_ref, wdown32_ref,
                  x1_ref, pool_out_ref, s_ref, pools_out_ref, ss_out_ref, x1s_ref,
                  wup16_ref, wdown16_ref,
                  ubuf, h_scr, qb_scr, qdec_scr, kb_scr, kdec_scr, vb_scr, g_scr, mix_scr,
                  ue, qraw_s, kraw_s, v_s, g_s, qdec_s, kdec_s, o_s,
                  *, tile, sub, nb, seq):
    t = pl.program_id(1)
    P, R = POOL_WIDTH, RET_WIDTH
    rows_s = nb * seq
    last_r0 = tile - sub

    def cast_items(src, dst, pieces):
        blk = src.shape[0] // pieces

        def piece(n):
            def run():
                dst[n * blk:(n + 1) * blk, :] = src[n * blk:(n + 1) * blk, :].astype(BF16)
            return run

        return [piece(n) for n in range(pieces)]

    @pl.when(t == 0)
    def _():
        ubuf[0:POOL_PAD, :] = jnp.zeros((POOL_PAD, POOL_WIDTH), F32)
        s_ref[...] = jnp.zeros_like(s_ref)

    def project_items(r0):
        rows = slice(r0, r0 + sub)
        n_rows = sub + rows_s if r0 == 0 else sub

        def norm():
            h_scr[0:sub, :] = _rmsnorm(x_ref[0, rows, :], g1_ref[...]).astype(BF16)
            if r0 == 0:
                h_scr[sub:n_rows, :] = _rmsnorm(xs_ref[...], g1_ref[...]).astype(BF16)

        def column_group(lo, keep, put_sample):
            def half(off):
                def run():
                    z = _dot(h_scr[0:n_rows, :], win_ref[:, lo + off:lo + off + PROJ_COLS])
                    keep(z[0:sub], off)
                    if r0 == 0:
                        put_sample(z[sub:n_rows], off)
                return run
            return [half(off) for off in range(0, R, PROJ_COLS)]

        def keep_rows(dst, dst_rows, dtype):
            def run(z, off):
                dst[dst_rows, off:off + PROJ_COLS] = z.astype(dtype)
            return run

        def keep_rotated(cos_ref, sin_ref, decay_ref, plain, decayed):
            def run(z, off):
                for c in range(sub // RET_CHUNK):
                    zr = slice(c * RET_CHUNK, (c + 1) * RET_CHUNK)
                    tr = slice(r0 + c * RET_CHUNK, r0 + (c + 1) * RET_CHUNK)
                    for lo in range(0, PROJ_COLS, RET_HEAD_DIM):
                        ls = slice(off + lo, off + lo + RET_HEAD_DIM)
                        rot = _rotate(z[zr, lo:lo + RET_HEAD_DIM], cos_ref[tr, :], sin_ref[tr, :])
                        plain[tr, ls] = rot.astype(BF16)
                        decayed[tr, ls] = (rot * decay_ref[:, ls]).astype(BF16)
            return run

        def put_u(z, off):
            ue[:, POOL_PAD:POOL_PAD + seq, off:off + PROJ_COLS] = z.reshape(nb, seq, PROJ_COLS)

        def put(dst):
            def run(z, off):
                dst[:, off:off + PROJ_COLS] = z
            return run

        u_rows = slice(POOL_PAD + r0, POOL_PAD + r0 + sub)
        return ([norm]
                + column_group(0, keep_rows(ubuf, u_rows, F32), put_u)
                + column_group(P, keep_rotated(cq_ref, sq_ref, qd_ref, qb_scr, qdec_scr),
                               put(qraw_s))
                + column_group(P + R, keep_rotated(ck_ref, sk_ref, kd_ref, kb_scr, kdec_scr),
                               put(kraw_s))
                + column_group(P + 2 * R, keep_rows(vb_scr, rows, BF16), put(v_s))
                + column_group(P + 3 * R, keep_rows(g_scr, rows, F32), put(g_s)))

    def mix_items(r0):
        base = POOL_PAD + r0
        items = []

        def pool_group(gi, win):
            def run():
                sl = slice(gi * POOL_GROUP_WIDTH, (gi + 1) * POOL_GROUP_WIDTH)
                pos = t * tile + r0 + lax.broadcasted_iota(jnp.int32, (sub, 1), 0)
                s = ubuf[base - POOL_PAD:base + sub, sl]
                cur = s[POOL_PAD:]
                shift = 1
                while shift < win:
                    s = s + pltpu.roll(s, shift, 0)
                    shift *= 2
                s = s[POOL_PAD:]
                cnt = jnp.minimum(pos + 1, win).astype(F32)
                pooled = s / cnt - cur
                y = _dot(pooled.astype(BF16), poolw_ref[gi]) * pscale_ref[:, sl]
                mix_scr[r0:r0 + sub, sl] = y.astype(BF16)
            return run

        live = {}

        def scores(c, hh):
            def run():
                rows = slice(r0 + c * RET_CHUNK, r0 + (c + 1) * RET_CHUNK)
                ls = _head(hh)
                sc = _dot_nt(qb_scr[rows, ls], kb_scr[rows, ls]) * dmask_ref[hh]
                live[(c, hh)] = jnp.concatenate([sc.astype(BF16), qdec_scr[rows, ls]], axis=1)
            return run

        def outputs(c, hh):
            def run():
                rows = slice(r0 + c * RET_CHUNK, r0 + (c + 1) * RET_CHUNK)
                ls = _head(hh)
                lhs = live.pop((c, hh))
                vb = vb_scr[rows, ls]
                state = s_ref[0, hh]
                rhs = jnp.concatenate([vb, state.astype(BF16)], axis=0)
                live[(c, hh, "o")] = _dot(lhs, rhs)
                s_ref[0, hh] = state * cd_ref[:, ls] + _dot_tn(kdec_scr[rows, ls], vb)
            return run

        def normalise(c, hh):
            def run():
                rows = slice(r0 + c * RET_CHUNK, r0 + (c + 1) * RET_CHUNK)
                ls = _head(hh)
                o = live.pop((c, hh, "o"))
                ret = _head_norm_gate(o, g_scr[rows, ls], gng_ref[:, ls], gnb_ref[:, ls])
                mix_scr[rows, P + hh * RET_HEAD_DIM:P + (hh + 1) * RET_HEAD_DIM] = (
                    ret.astype(BF16))
            return run

        def out_project(lo, width):
            def run():
                n_rows = sub + rows_s if r0 == last_r0 else sub
                y = _dot(mix_scr[r0:r0 + n_rows, :], wout_ref[:, lo:lo + width])
                x1_ref[0, r0:r0 + sub, lo:lo + width] = (
                    x_ref[0, r0:r0 + sub, lo:lo + width] + y[0:sub])
                if r0 == last_r0:
                    x1s_ref[:, lo:lo + width] = xs_ref[:, lo:lo + width] + y[sub:n_rows]
            return run

        heads = [(c, hh) for c in range(sub // RET_CHUNK) for hh in range(RET_HEADS)]
        stages = tuple((n * RET_STAGE_LAG, stage)
                       for n, stage in enumerate((scores, outputs, normalise)))
        pools = [pool_group(gi, win) for gi, win in enumerate(POOL_WINDOWS)]
        for n in range(len(heads) + stages[-1][0]):
            for lag, stage in stages:
                if 0 <= n - lag < len(heads):
                    items.append(stage(*heads[n - lag]))
            if n % 2 == 1 and pools:
                items.append(pools.pop(0))
        items.extend(pools)
        half = D_MODEL // 2
        return items, [out_project(0, half), out_project(half, half)]

    def sample_items():
        srows = slice(tile, tile + rows_s)

        def stage_context():
            ue[:, 1:POOL_PAD, :] = ctx_ref[...]

        def pool_group(gi, win):
            def run():
                sl = slice(gi * POOL_GROUP_WIDTH, (gi + 1) * POOL_GROUP_WIDTH)
                pos = PAST_LEN + lax.broadcasted_iota(jnp.int32, (1, seq, 1), 1)
                cur = ue[:, POOL_PAD:POOL_PAD + seq, sl]
                s = cur
                for j in range(1, win):
                    s = s + ue[:, POOL_PAD - j:POOL_PAD - j + seq, sl]
                cnt = jnp.minimum(pos + 1, win).astype(F32)
                pooled = (s / cnt - cur).reshape(rows_s, POOL_GROUP_WIDTH)
                y = _dot(pooled.astype(BF16), poolw_ref[gi]) * pscale_ref[:, sl]
                mix_scr[srows, sl] = y.astype(BF16)
            return run

        def new_context():
            pools_out_ref[...] = ue[:, seq + 1:seq + POOL_PAD, :]

        def scores(hh):
            def run():
                ls = _head(hh)
                qr = _rotate(qraw_s[:, ls], cqs_ref[...], sqs_ref[...])
                kr = _rotate(kraw_s[:, ls], cks_ref[...], sks_ref[...])
                sc = _dot_nt(qr.astype(BF16), kr.astype(BF16)) * dmasks_ref[hh]
                o_s[:, ls] = _dot(sc.astype(BF16), v_s[:, ls].astype(BF16))
                qdec_s[:, ls] = qr * qds_ref[:, ls]
                kdec_s[:, ls] = kr * kds_ref[:, ls]
            return run

        def state(b):
            def run():
                r = slice(b * seq, (b + 1) * seq)
                for hh in range(RET_HEADS):
                    ls = _head(hh)
                    st = s_in_ref[b, hh]
                    vb = v_s[r, ls].astype(BF16)
                    o_s[r, ls] = o_s[r, ls] + _dot(qdec_s[r, ls].astype(BF16), st.astype(BF16))
                    ss_out_ref[b, hh] = (st * cds_ref[:, ls]
                                         + _dot_tn(kdec_s[r, ls].astype(BF16), vb))
            return run

        def normalise(hh):
            def run():
                ls = _head(hh)
                ret = _head_norm_gate(o_s[:, ls], g_s[:, ls], gng_ref[:, ls], gnb_ref[:, ls])
                mix_scr[srows, P + hh * RET_HEAD_DIM:P + (hh + 1) * RET_HEAD_DIM] = (
                    ret.astype(BF16))
            return run

        return ([stage_context]
                + [pool_group(gi, win) for gi, win in enumerate(POOL_WINDOWS)]
                + [new_context]
                + [scores(hh) for hh in range(RET_HEADS)]
                + [state(b) for b in range(nb)]
                + [normalise(hh) for hh in range(RET_HEADS)])

    first = project_items(0)
    middle, closing = [], []
    for r0 in range(0, tile, sub):
        nxt = project_items(r0 + sub) if r0 + sub < tile else []
        items, out_proj = mix_items(r0)
        if r0 == last_r0:
            middle += _interleave(items, nxt)
            closing = out_proj
        else:
            middle += _interleave(items + out_proj, nxt)
    side = sample_items() + cast_items(wup32_ref, wup16_ref, 4) + cast_items(
        wdown32_ref, wdown16_ref, 4)
    for item in first + _interleave(middle, side) + closing:
        item()

    pool_out_ref[0] = ubuf[tile + 1:tile + POOL_PAD, :]
    ubuf[0:POOL_PAD, :] = ubuf[tile:tile + POOL_PAD, :]


def _mixer(x, xs2d, state_pool, state_ret, w, big, layer, tabs, tabs_s, prev, ffn_w32):
    B, T, D = x.shape
    depth, n_seq = state_pool.shape[:2]
    tile, sub = PROMPT_TILE, PROMPT_SUB
    n_t = T // tile
    steps = B * n_t
    nb = n_seq // steps
    seq = xs2d.shape[0] // n_seq
    rows_s = nb * seq
    assert T % tile == 0 and tile % sub == 0 and sub % RET_CHUNK == 0
    assert n_seq == nb * steps and seq == SUBLANES and rows_s % (2 * SUBLANES) == 0
    step_of = lambda b, t: b * n_t + t
    tab_spec = pl.BlockSpec((tile, RET_HEAD_DIM), lambda b, t: (t, 0))
    in_specs = [
        pl.BlockSpec((1, tile, D), lambda b, t: (b, t, 0)),
        pl.BlockSpec((rows_s, D), lambda b, t: (step_of(b, t), 0)),
        pl.BlockSpec((None, nb, POOL_CTX, POOL_WIDTH), lambda b, t: (layer, step_of(b, t), 0, 0)),
        pl.BlockSpec((None, nb, RET_HEADS, RET_HEAD_DIM, RET_HEAD_DIM),
                     lambda b, t: (layer, step_of(b, t), 0, 0, 0)),
        _layer_spec(w["g1"], layer),
        _whole_spec(big["w_in"]),
        _layer_spec(w["pool_w"], layer),
        _layer_spec(w["pool_scale"], layer),
        _layer_spec(w["gn_g"], layer),
        _layer_spec(w["gn_b"], layer),
        _whole_spec(big["w_out"]),
        tab_spec, tab_spec, tab_spec, tab_spec,
        _whole_spec(tabs["qd"]), _whole_spec(tabs["kd"]), _whole_spec(tabs["cd"]),
        _whole_spec(tabs["dmask"]),
    ] + [_whole_spec(tabs_s[k]) for k in ("cq", "sq", "ck", "sk", "qd", "kd", "cd", "dmask")]
    casts = [_cast_specs(src, layer, steps, step_of) for src in ffn_w32]
    in_specs += [c[0] for c in casts]
    out_shape = (
        jax.ShapeDtypeStruct((B, T, D), F32),
        jax.ShapeDtypeStruct((depth, B, POOL_CTX, POOL_WIDTH), F32),
        jax.ShapeDtypeStruct((depth, B, RET_HEADS, RET_HEAD_DIM, RET_HEAD_DIM), F32),
        jax.ShapeDtypeStruct(state_pool.shape, F32),
        jax.ShapeDtypeStruct(state_ret.shape, F32),
        jax.ShapeDtypeStruct(xs2d.shape, F32),
    ) + tuple(c[2] for c in casts)
    out_specs = (
        pl.BlockSpec((1, tile, D), lambda b, t: (b, t, 0)),
        pl.BlockSpec((None, 1, POOL_CTX, POOL_WIDTH), lambda b, t: (layer, b, 0, 0)),
        pl.BlockSpec((None, 1, RET_HEADS, RET_HEAD_DIM, RET_HEAD_DIM),
                     lambda b, t: (layer, b, 0, 0, 0)),
        pl.BlockSpec((None, nb, POOL_CTX, POOL_WIDTH), lambda b, t: (layer, step_of(b, t), 0, 0)),
        pl.BlockSpec((None, nb, RET_HEADS, RET_HEAD_DIM, RET_HEAD_DIM),
                     lambda b, t: (layer, step_of(b, t), 0, 0, 0)),
        pl.BlockSpec((rows_s, D), lambda b, t: (step_of(b, t), 0)),
    ) + tuple(c[1] for c in casts)
    body = functools.partial(_mixer_kernel, tile=tile, sub=sub, nb=nb, seq=seq)
    kernel_fn, alias_specs, aliases, alias_args = _stacked_outputs(body, len(in_specs), prev)
    sample_tile = pltpu.VMEM((rows_s, RET_WIDTH), F32)
    scratch = [
        pltpu.VMEM((POOL_PAD + tile, POOL_WIDTH), F32),
        pltpu.VMEM((sub + rows_s, D_MODEL), BF16),
    ] + [pltpu.VMEM((tile, RET_WIDTH), BF16)] * 5 + [
        pltpu.VMEM((tile, RET_WIDTH), F32),
        pltpu.VMEM((tile + rows_s, D_MODEL), BF16),
        pltpu.VMEM((nb, POOL_PAD + seq, POOL_WIDTH), F32),
    ] + [sample_tile] * 7
    return pl.pallas_call(
        kernel_fn,
        grid=(B, n_t),
        in_specs=in_specs + alias_specs,
        out_specs=out_specs,
        out_shape=out_shape,
        scratch_shapes=scratch,
        input_output_aliases=aliases,
        compiler_params=pltpu.CompilerParams(
            dimension_semantics=("arbitrary", "arbitrary"),
            vmem_limit_bytes=VMEM_LIMIT_BYTES,
        ),
        name="mixer",
    )(x, xs2d, state_pool, state_ret, w["g1"], big["w_in"], w["pool_w"], w["pool_scale"],
      w["gn_g"], w["gn_b"], big["w_out"],
      tabs["cq"], tabs["sq"], tabs["ck"], tabs["sk"], tabs["qd"], tabs["kd"], tabs["cd"],
      tabs["dmask"],
      *[tabs_s[k] for k in ("cq", "sq", "ck", "sk", "qd", "kd", "cd", "dmask")],
      *ffn_w32, *alias_args)


def _ffn_kernel(xp_ref, xs_ref, g2_ref, wup_ref, wdown_ref, fg_ref, *rest, final_norm):
    if final_norm:
        op_ref, os_ref = rest
    else:
        win32_ref, wout32_ref, op_ref, os_ref, win16_ref, wout16_ref = rest
        win16_ref[...] = win32_ref[...].astype(BF16)
        wout16_ref[...] = wout32_ref[...].astype(BF16)
    n_p = xp_ref.shape[0]
    xp, xs = xp_ref[...], xs_ref[...]
    h = jnp.concatenate([_rmsnorm(xp, g2_ref[...]).astype(BF16),
                         _rmsnorm(xs, g2_ref[...]).astype(BF16)], axis=0)
    acc = None
    for c in range(D_FF // FFN_CHUNK):
        cols = slice(c * FFN_CHUNK, (c + 1) * FFN_CHUNK)
        a = jnp.square(jnp.maximum(_dot(h, wup_ref[:, cols]), 0.0)).astype(BF16)
        d = _dot(a, wdown_ref[cols, :])
        acc = d if acc is None else acc + d
    yp, ys = xp + acc[:n_p], xs + acc[n_p:]
    if final_norm:
        yp, ys = _rmsnorm(yp, fg_ref[...]), _rmsnorm(ys, fg_ref[...])
    op_ref[...] = yp
    os_ref[...] = ys


def _ffn(xp2d, xs2d, w, big, layer, final_g, mixer_w32):
    final_norm = not mixer_w32
    n, d = xp2d.shape
    tile = FFN_TILE
    steps = n // tile
    tile_s = xs2d.shape[0] // steps
    assert n % tile == 0 and xs2d.shape[0] == steps * tile_s and tile_s % (2 * SUBLANES) == 0
    const2 = lambda i: (0, 0)
    casts = [_cast_specs(src, layer + 1, steps, lambda i: i) for src in mixer_w32]
    return pl.pallas_call(
        functools.partial(_ffn_kernel, final_norm=final_norm),
        grid=(steps,),
        in_specs=[
            pl.BlockSpec((tile, d), lambda i: (i, 0)),
            pl.BlockSpec((tile_s, d), lambda i: (i, 0)),
            _layer_spec(w["g2"], layer),
            _whole_spec(big["w_up"]),
            _whole_spec(big["w_down"]),
            pl.BlockSpec((1, d), const2),
        ] + [c[0] for c in casts],
        out_specs=(pl.BlockSpec((tile, d), lambda i: (i, 0)),
                   pl.BlockSpec((tile_s, d), lambda i: (i, 0))) + tuple(c[1] for c in casts),
        out_shape=(jax.ShapeDtypeStruct((n, d), F32),
                   jax.ShapeDtypeStruct(xs2d.shape, F32)) + tuple(c[2] for c in casts),
        compiler_params=pltpu.CompilerParams(
            dimension_semantics=("arbitrary",),
            vmem_limit_bytes=VMEM_LIMIT_BYTES),
        name="ffn",
    )(xp2d, xs2d, w["g2"], big["w_up"], big["w_down"], final_g, *mixer_w32)


def _retention_tables(pos, chunk, reps):
    half = RET_HEAD_DIM // 2
    inv = ROPE_BASE ** (-jnp.arange(half, dtype=F32) / half)
    ang = pos.astype(F32)[:, None] * inv[None, :]
    cos, sin = jnp.cos(ang), jnp.sin(ang)
    cq = jnp.concatenate([cos, cos], axis=-1)
    sq = jnp.concatenate([-sin, sin], axis=-1)
    kscale = RET_HEAD_DIM ** -0.5

    gamma = 1.0 - jnp.exp2(-5.0 - jnp.arange(RET_HEADS, dtype=F32))
    lg = jnp.log(gamma)
    i = jnp.arange(chunk, dtype=F32)
    diff = i[:, None] - i[None, :]
    dmask = jnp.where(diff >= 0, jnp.exp(lg[:, None, None] * jnp.maximum(diff, 0.0)), 0.0)
    q_decay = jnp.exp(lg[:, None] * (i + 1.0))
    k_decay = jnp.exp(lg[:, None] * (chunk - 1.0 - i))
    c_decay = jnp.exp(lg * chunk)

    def lanes(a):
        a = jnp.repeat(a.T[:, :, None], RET_HEAD_DIM, axis=2).reshape(chunk, RET_WIDTH)
        return jnp.tile(a, (reps, 1))

    if reps > 1:
        eye = jnp.eye(reps, dtype=F32)
        dmask = jnp.einsum("ab,hij->haibj", eye, dmask).reshape(
            RET_HEADS, reps * chunk, reps * chunk)
    return {
        "cq": cq, "sq": sq, "ck": cq * kscale, "sk": sq * kscale,
        "qd": lanes(q_decay), "kd": lanes(k_decay),
        "cd": jnp.repeat(c_decay, RET_HEAD_DIM)[None, :],
        "dmask": dmask,
    }


def kernel(x_prompt, x_sample, state_pool, state_ret, norm1_g, w_in, pool_w, pool_scale,
           gn_g, gn_b, w_out, norm2_g, w_up, w_down, final_g):
    depth = w_in.shape[0]
    B, T, D = x_prompt.shape
    NB, TS, _ = x_sample.shape
    seqs_per_step = NB // (B * (T // PROMPT_TILE))

    tabs_p = _retention_tables(jnp.arange(T, dtype=jnp.int32), RET_CHUNK, 1)
    pos_s = PAST_LEN + jnp.arange(TS, dtype=jnp.int32)
    tabs_s = _retention_tables(jnp.tile(pos_s, seqs_per_step), TS, seqs_per_step)

    yp = x_prompt
    ys = x_sample.reshape(NB * TS, D)
    fg = final_g.reshape(1, D)
    w = {
        "g1": norm1_g.reshape(depth, 1, D),
        "pool_w": pool_w.astype(BF16),
        "pool_scale": pool_scale.reshape(depth, 1, POOL_WIDTH),
        "gn_g": gn_g.reshape(depth, 1, RET_WIDTH),
        "gn_b": gn_b.reshape(depth, 1, RET_WIDTH),
        "g2": norm2_g.reshape(depth, 1, D),
    }
    big = {"w_in": w_in[0].astype(BF16), "w_out": w_out[0].astype(BF16)}
    states = ()
    for l in range(depth):
        x1p, *states, x1s, big["w_up"], big["w_down"] = _mixer(
            yp, ys, state_pool, state_ret, w, big, l, tabs_p, tabs_s, states, (w_up, w_down))
        nxt = (w_in, w_out) if l + 1 < depth else ()
        yp, ys, *nxt16 = _ffn(x1p.reshape(B * T, D), x1s, w, big, l, fg, nxt)
        if nxt16:
            big["w_in"], big["w_out"] = nxt16
        yp = yp.reshape(B, T, D)
    pool_p, ret_p, pool_s, ret_s = states
    return (yp, ys.reshape(NB, TS, D), pool_p, ret_p, pool_s, ret_s)
```

```python
import functools

import jax
import jax.numpy as jnp
from jax import lax
from jax.experimental import pallas as pl
from jax.experimental.pallas import tpu as pltpu

D_MODEL = 1024
POOL_WIDTH = D_MODEL // 2
POOL_WINDOWS = (2, 4, 8, 16)
POOL_GROUP_WIDTH = POOL_WIDTH // len(POOL_WINDOWS)
POOL_CTX = max(POOL_WINDOWS) - 1
POOL_PAD = POOL_CTX + 1
RET_WIDTH = D_MODEL - POOL_WIDTH
RET_HEADS = 4
RET_HEAD_DIM = RET_WIDTH // RET_HEADS
RET_CHUNK = 128
ROPE_BASE = 10000.0
D_FF = 4 * D_MODEL
RMS_EPS = 1e-6
GN_EPS = 1e-5
PAST_LEN = 16384

SUBLANES = 8
VMEM_LIMIT_BYTES = 60 * 1024 * 1024

PROMPT_TILE = 1024
PROMPT_SUB = 512
RET_STAGE_LAG = 2
PROJ_COLS = 256
FFN_TILE = 512
FFN_CHUNK = 1024

TABLE_KEYS = ("cos", "sin", "qd", "kd", "cd", "dmask")

F32 = jnp.float32
BF16 = jnp.bfloat16


def _dot(a, b):
    return jnp.dot(a, b, preferred_element_type=F32)


def _dot_nt(a, b):
    return lax.dot_general(a, b, (((1,), (1,)), ((), ())), preferred_element_type=F32)


def _dot_tn(a, b):
    return lax.dot_general(a, b, (((0,), (0,)), ((), ())), preferred_element_type=F32)


def _rmsnorm(x, g):
    ms = jnp.mean(x * x, axis=-1, keepdims=True)
    return x * lax.rsqrt(ms + RMS_EPS) * g


def _rotate(x, cos, sin_signed):
    return x * cos + pltpu.roll(x, RET_HEAD_DIM // 2, 1) * sin_signed


def _head_norm_gate(o, gate, gn_g, gn_b):
    mu = jnp.mean(o, axis=-1, keepdims=True)
    d = o - mu
    var = jnp.mean(d * d, axis=-1, keepdims=True)
    on = d * lax.rsqrt(var + GN_EPS) * gn_g + gn_b
    return gate * jax.nn.sigmoid(gate) * on


def _head(hh):
    return slice(hh * RET_HEAD_DIM, (hh + 1) * RET_HEAD_DIM)


def _whole_spec(arr):
    zeros = (0,) * arr.ndim
    return pl.BlockSpec(arr.shape, lambda *_: zeros)


def _layer_spec(arr, layer):
    tail = (0,) * (arr.ndim - 1)
    return pl.BlockSpec((None,) + arr.shape[1:], lambda *_: (layer,) + tail)


def _cast_specs(src, layer, steps, step_of):
    _, rows, cols = src.shape
    assert rows % (steps * 2 * SUBLANES) == 0
    blk = rows // steps
    in_spec = pl.BlockSpec((None, blk, cols), lambda *g: (layer, step_of(*g), 0))
    out_spec = pl.BlockSpec((blk, cols), lambda *g: (step_of(*g), 0))
    return in_spec, out_spec, jax.ShapeDtypeStruct((rows, cols), BF16)


def _stacked_outputs(body, n_in, prev):
    prev = tuple(prev)

    def kernel(*refs):
        body(*refs[:n_in], *refs[n_in + len(prev):])

    specs = [pl.BlockSpec(memory_space=pl.ANY)] * len(prev)
    aliases = {n_in + k: 1 + k for k in range(len(prev))}
    return kernel, specs, aliases, prev


def _interleave(major, minor):
    out, taken = [], 0
    for n, item in enumerate(major):
        out.append(item)
        want = (n + 1) * len(minor) // len(major)
        out.extend(minor[taken:want])
        taken = want
    return out


def _mixer_kernel(x_ref, xs_ref, ctx_ref, s_in_ref,
                  g1_ref, win_ref, poolw_ref, pscale_ref, gng_ref, gnb_ref, wout_ref,
                  cos_ref, sin_ref, qd_ref, kd_ref, cd_ref, dmask_ref,
                  coss_ref, sins_ref, qds_ref, kds_ref, cds_ref, dmasks_ref,
                  wup32_ref, wdown32_ref,
                  x1_ref, pool_out_ref, s_ref, pools_out_ref, ss_out_ref, x1s_ref,
                  wup16_ref, wdown16_ref,
                  ubuf, h_scr, qb_scr, qdec_scr, kb_scr, kdec_scr, vb_scr, g_scr, mix_scr,
                  ue, qraw_s, kraw_s, v_s, g_s, qdec_s, kdec_s, o_s,
                  *, tile, sub, nb, seq):
    t = pl.program_id(1)
    P, R = POOL_WIDTH, RET_WIDTH
    rows_s = nb * seq
    last_r0 = tile - sub

    def cast_items(src, dst, pieces):
        blk = src.shape[0] // pieces

        def piece(n):
            def run():
                dst[n * blk:(n + 1) * blk, :] = src[n * blk:(n + 1) * blk, :].astype(BF16)
            return run

        return [piece(n) for n in range(pieces)]

    @pl.when(t == 0)
    def _():
        ubuf[0:POOL_PAD, :] = jnp.zeros((POOL_PAD, POOL_WIDTH), F32)
        s_ref[...] = jnp.zeros_like(s_ref)

    def project_items(r0):
        rows = slice(r0, r0 + sub)
        n_rows = sub + rows_s if r0 == 0 else sub

        def norm():
            h_scr[0:sub, :] = _rmsnorm(x_ref[0, rows, :], g1_ref[...]).astype(BF16)
            if r0 == 0:
                h_scr[sub:n_rows, :] = _rmsnorm(xs_ref[...], g1_ref[...]).astype(BF16)

        def column_group(lo, keep, put_sample):
            def half(off):
                def run():
                    z = _dot(h_scr[0:n_rows, :], win_ref[:, lo + off:lo + off + PROJ_COLS])
                    keep(z[0:sub], off)
                    if r0 == 0:
                        put_sample(z[sub:n_rows], off)
                return run
            return [half(off) for off in range(0, R, PROJ_COLS)]

        def keep_rows(dst, dst_rows, dtype):
            def run(z, off):
                dst[dst_rows, off:off + PROJ_COLS] = z.astype(dtype)
            return run

        def keep_rotated(decay_ref, plain, decayed):
            def run(z, off):
                for c in range(sub // RET_CHUNK):
                    zr = slice(c * RET_CHUNK, (c + 1) * RET_CHUNK)
                    tr = slice(r0 + c * RET_CHUNK, r0 + (c + 1) * RET_CHUNK)
                    for lo in range(0, PROJ_COLS, RET_HEAD_DIM):
                        ls = slice(off + lo, off + lo + RET_HEAD_DIM)
                        rot = _rotate(z[zr, lo:lo + RET_HEAD_DIM], cos_ref[tr, :], sin_ref[tr, :])
                        plain[tr, ls] = rot.astype(BF16)
                        decayed[tr, ls] = (rot * decay_ref[:, ls]).astype(BF16)
            return run

        def put_u(z, off):
            ue[:, POOL_PAD:POOL_PAD + seq, off:off + PROJ_COLS] = z.reshape(nb, seq, PROJ_COLS)

        def put(dst):
            def run(z, off):
                dst[:, off:off + PROJ_COLS] = z
            return run

        u_rows = slice(POOL_PAD + r0, POOL_PAD + r0 + sub)
        return ([norm]
                + column_group(0, keep_rows(ubuf, u_rows, F32), put_u)
                + column_group(P, keep_rotated(qd_ref, qb_scr, qdec_scr),
                               put(qraw_s))
                + column_group(P + R, keep_rotated(kd_ref, kb_scr, kdec_scr),
                               put(kraw_s))
                + column_group(P + 2 * R, keep_rows(vb_scr, rows, BF16), put(v_s))
                + column_group(P + 3 * R, keep_rows(g_scr, rows, F32), put(g_s)))

    def mix_items(r0):
        base = POOL_PAD + r0
        items = []

        def pool_group(gi, win):
            def run():
                sl = slice(gi * POOL_GROUP_WIDTH, (gi + 1) * POOL_GROUP_WIDTH)
                pos = t * tile + r0 + lax.broadcasted_iota(jnp.int32, (sub, 1), 0)
                s = ubuf[base - POOL_PAD:base + sub, sl]
                cur = s[POOL_PAD:]
                shift = 1
                while shift < win:
                    s = s + pltpu.roll(s, shift, 0)
                    shift *= 2
                s = s[POOL_PAD:]
                cnt = jnp.minimum(pos + 1, win).astype(F32)
                pooled = s / cnt - cur
                y = _dot(pooled.astype(BF16), poolw_ref[gi]) * pscale_ref[:, sl]
                mix_scr[r0:r0 + sub, sl] = y.astype(BF16)
            return run

        live = {}

        def scores(c, hh):
            def run():
                rows = slice(r0 + c * RET_CHUNK, r0 + (c + 1) * RET_CHUNK)
                ls = _head(hh)
                sc = _dot_nt(qb_scr[rows, ls], kb_scr[rows, ls]) * dmask_ref[hh]
                live[(c, hh)] = jnp.concatenate([sc.astype(BF16), qdec_scr[rows, ls]], axis=1)
            return run

        def outputs(c, hh):
            def run():
                rows = slice(r0 + c * RET_CHUNK, r0 + (c + 1) * RET_CHUNK)
                ls = _head(hh)
                lhs = live.pop((c, hh))
                vb = vb_scr[rows, ls]
                state = s_ref[0, hh]
                rhs = jnp.concatenate([vb, state.astype(BF16)], axis=0)
                live[(c, hh, "o")] = _dot(lhs, rhs)
                s_ref[0, hh] = state * cd_ref[:, ls] + _dot_tn(kdec_scr[rows, ls], vb)
            return run

        def normalise(c, hh):
            def run():
                rows = slice(r0 + c * RET_CHUNK, r0 + (c + 1) * RET_CHUNK)
                ls = _head(hh)
                o = live.pop((c, hh, "o"))
                ret = _head_norm_gate(o, g_scr[rows, ls], gng_ref[:, ls], gnb_ref[:, ls])
                mix_scr[rows, P + hh * RET_HEAD_DIM:P + (hh + 1) * RET_HEAD_DIM] = (
                    ret.astype(BF16))
            return run

        def out_project(lo, width):
            def run():
                n_rows = sub + rows_s if r0 == last_r0 else sub
                y = _dot(mix_scr[r0:r0 + n_rows, :], wout_ref[:, lo:lo + width])
                x1_ref[0, r0:r0 + sub, lo:lo + width] = (
                    x_ref[0, r0:r0 + sub, lo:lo + width] + y[0:sub])
                if r0 == last_r0:
                    x1s_ref[:, lo:lo + width] = xs_ref[:, lo:lo + width] + y[sub:n_rows]
            return run

        heads = [(c, hh) for c in range(sub // RET_CHUNK) for hh in range(RET_HEADS)]
        stages = tuple((n * RET_STAGE_LAG, stage)
                       for n, stage in enumerate((scores, outputs, normalise)))
        pools = [pool_group(gi, win) for gi, win in enumerate(POOL_WINDOWS)]
        for n in range(len(heads) + stages[-1][0]):
            for lag, stage in stages:
                if 0 <= n - lag < len(heads):
                    items.append(stage(*heads[n - lag]))
            if n % 2 == 1 and pools:
                items.append(pools.pop(0))
        items.extend(pools)
        half = D_MODEL // 2
        return items, [out_project(0, half), out_project(half, half)]

    def sample_items():
        srows = slice(tile, tile + rows_s)

        def stage_context():
            ue[:, 1:POOL_PAD, :] = ctx_ref[...]

        def pool_group(gi, win):
            def run():
                sl = slice(gi * POOL_GROUP_WIDTH, (gi + 1) * POOL_GROUP_WIDTH)
                pos = PAST_LEN + lax.broadcasted_iota(jnp.int32, (1, seq, 1), 1)
                cur = ue[:, POOL_PAD:POOL_PAD + seq, sl]
                s = cur
                for j in range(1, win):
                    s = s + ue[:, POOL_PAD - j:POOL_PAD - j + seq, sl]
                cnt = jnp.minimum(pos + 1, win).astype(F32)
                pooled = (s / cnt - cur).reshape(rows_s, POOL_GROUP_WIDTH)
                y = _dot(pooled.astype(BF16), poolw_ref[gi]) * pscale_ref[:, sl]
                mix_scr[srows, sl] = y.astype(BF16)
            return run

        def new_context():
            pools_out_ref[...] = ue[:, seq + 1:seq + POOL_PAD, :]

        def scores(hh):
            def run():
                ls = _head(hh)
                qr = _rotate(qraw_s[:, ls], coss_ref[...], sins_ref[...])
                kr = _rotate(kraw_s[:, ls], coss_ref[...], sins_ref[...])
                sc = _dot_nt(qr.astype(BF16), kr.astype(BF16)) * dmasks_ref[hh]
                o_s[:, ls] = _dot(sc.astype(BF16), v_s[:, ls].astype(BF16))
                qdec_s[:, ls] = qr * qds_ref[:, ls]
                kdec_s[:, ls] = kr * kds_ref[:, ls]
            return run

        def state(b):
            def run():
                r = slice(b * seq, (b + 1) * seq)
                for hh in range(RET_HEADS):
                    ls = _head(hh)
                    st = s_in_ref[b, hh]
                    vb = v_s[r, ls].astype(BF16)
                    o_s[r, ls] = o_s[r, ls] + _dot(qdec_s[r, ls].astype(BF16), st.astype(BF16))
                    ss_out_ref[b, hh] = (st * cds_ref[:, ls]
                                         + _dot_tn(kdec_s[r, ls].astype(BF16), vb))
            return run

        def normalise(hh):
            def run():
                ls = _head(hh)
                ret = _head_norm_gate(o_s[:, ls], g_s[:, ls], gng_ref[:, ls], gnb_ref[:, ls])
                mix_scr[srows, P + hh * RET_HEAD_DIM:P + (hh + 1) * RET_HEAD_DIM] = (
                    ret.astype(BF16))
            return run

        return ([stage_context]
                + [pool_group(gi, win) for gi, win in enumerate(POOL_WINDOWS)]
                + [new_context]
                + [scores(hh) for hh in range(RET_HEADS)]
                + [state(b) for b in range(nb)]
                + [normalise(hh) for hh in range(RET_HEADS)])

    first = project_items(0)
    middle, closing = [], []
    for r0 in range(0, tile, sub):
        nxt = project_items(r0 + sub) if r0 + sub < tile else []
        items, out_proj = mix_items(r0)
        if r0 == last_r0:
            middle += _interleave(items, nxt)
            closing = out_proj
        else:
            middle += _interleave(items + out_proj, nxt)
    side = sample_items() + cast_items(wup32_ref, wup16_ref, 4) + cast_items(
        wdown32_ref, wdown16_ref, 4)
    for item in first + _interleave(middle, side) + closing:
        item()

    pool_out_ref[0] = ubuf[tile + 1:tile + POOL_PAD, :]
    ubuf[0:POOL_PAD, :] = ubuf[tile:tile + POOL_PAD, :]


def _mixer(x, xs2d, state_pool, state_ret, w, big, layer, tabs, tabs_s, prev, ffn_w32):
    B, T, D = x.shape
    depth, n_seq = state_pool.shape[:2]
    tile, sub = PROMPT_TILE, PROMPT_SUB
    n_t = T // tile
    steps = B * n_t
    nb = n_seq // steps
    seq = xs2d.shape[0] // n_seq
    rows_s = nb * seq
    assert T % tile == 0 and tile % sub == 0 and sub % RET_CHUNK == 0
    assert n_seq == nb * steps and seq == SUBLANES and rows_s % (2 * SUBLANES) == 0
    step_of = lambda b, t: b * n_t + t
    tab_spec = pl.BlockSpec((tile, RET_HEAD_DIM), lambda b, t: (t, 0))
    in_specs = [
        pl.BlockSpec((1, tile, D), lambda b, t: (b, t, 0)),
        pl.BlockSpec((rows_s, D), lambda b, t: (step_of(b, t), 0)),
        pl.BlockSpec((None, nb, POOL_CTX, POOL_WIDTH), lambda b, t: (layer, step_of(b, t), 0, 0)),
        pl.BlockSpec((None, nb, RET_HEADS, RET_HEAD_DIM, RET_HEAD_DIM),
                     lambda b, t: (layer, step_of(b, t), 0, 0, 0)),
        _layer_spec(w["g1"], layer),
        _whole_spec(big["w_in"]),
        _layer_spec(w["pool_w"], layer),
        _layer_spec(w["pool_scale"], layer),
        _layer_spec(w["gn_g"], layer),
        _layer_spec(w["gn_b"], layer),
        _whole_spec(big["w_out"]),
        tab_spec, tab_spec,
        _whole_spec(tabs["qd"]), _whole_spec(tabs["kd"]), _whole_spec(tabs["cd"]),
        _whole_spec(tabs["dmask"]),
    ] + [_whole_spec(tabs_s[k]) for k in TABLE_KEYS]
    casts = [_cast_specs(src, layer, steps, step_of) for src in ffn_w32]
    in_specs += [c[0] for c in casts]
    out_shape = (
        jax.ShapeDtypeStruct((B, T, D), F32),
        jax.ShapeDtypeStruct((depth, B, POOL_CTX, POOL_WIDTH), F32),
        jax.ShapeDtypeStruct((depth, B, RET_HEADS, RET_HEAD_DIM, RET_HEAD_DIM), F32),
        jax.ShapeDtypeStruct(state_pool.shape, F32),
        jax.ShapeDtypeStruct(state_ret.shape, F32),
        jax.ShapeDtypeStruct(xs2d.shape, F32),
    ) + tuple(c[2] for c in casts)
    out_specs = (
        pl.BlockSpec((1, tile, D), lambda b, t: (b, t, 0)),
        pl.BlockSpec((None, 1, POOL_CTX, POOL_WIDTH), lambda b, t: (layer, b, 0, 0)),
        pl.BlockSpec((None, 1, RET_HEADS, RET_HEAD_DIM, RET_HEAD_DIM),
                     lambda b, t: (layer, b, 0, 0, 0)),
        pl.BlockSpec((None, nb, POOL_CTX, POOL_WIDTH), lambda b, t: (layer, step_of(b, t), 0, 0)),
        pl.BlockSpec((None, nb, RET_HEADS, RET_HEAD_DIM, RET_HEAD_DIM),
                     lambda b, t: (layer, step_of(b, t), 0, 0, 0)),
        pl.BlockSpec((rows_s, D), lambda b, t: (step_of(b, t), 0)),
    ) + tuple(c[1] for c in casts)
    body = functools.partial(_mixer_kernel, tile=tile, sub=sub, nb=nb, seq=seq)
    kernel_fn, alias_specs, aliases, alias_args = _stacked_outputs(body, len(in_specs), prev)
    sample_tile = pltpu.VMEM((rows_s, RET_WIDTH), F32)
    scratch = [
        pltpu.VMEM((POOL_PAD + tile, POOL_WIDTH), F32),
        pltpu.VMEM((sub + rows_s, D_MODEL), BF16),
    ] + [pltpu.VMEM((tile, RET_WIDTH), BF16)] * 5 + [
        pltpu.VMEM((tile, RET_WIDTH), F32),
        pltpu.VMEM((tile + rows_s, D_MODEL), BF16),
        pltpu.VMEM((nb, POOL_PAD + seq, POOL_WIDTH), F32),
    ] + [sample_tile] * 7
    return pl.pallas_call(
        kernel_fn,
        grid=(B, n_t),
        in_specs=in_specs + alias_specs,
        out_specs=out_specs,
        out_shape=out_shape,
        scratch_shapes=scratch,
        input_output_aliases=aliases,
        compiler_params=pltpu.CompilerParams(
            dimension_semantics=("arbitrary", "arbitrary"),
            vmem_limit_bytes=VMEM_LIMIT_BYTES,
        ),
        name="mixer",
    )(x, xs2d, state_pool, state_ret, w["g1"], big["w_in"], w["pool_w"], w["pool_scale"],
      w["gn_g"], w["gn_b"], big["w_out"],
      *[tabs[k] for k in TABLE_KEYS], *[tabs_s[k] for k in TABLE_KEYS],
      *ffn_w32, *alias_args)


def _ffn_kernel(xp_ref, xs_ref, g2_ref, wup_ref, wdown_ref, fg_ref, *rest, final_norm):
    if final_norm:
        op_ref, os_ref = rest
    else:
        win32_ref, wout32_ref, op_ref, os_ref, win16_ref, wout16_ref = rest
        win16_ref[...] = win32_ref[...].astype(BF16)
        wout16_ref[...] = wout32_ref[...].astype(BF16)
    n_p = xp_ref.shape[0]
    xp, xs = xp_ref[...], xs_ref[...]
    h = jnp.concatenate([_rmsnorm(xp, g2_ref[...]).astype(BF16),
                         _rmsnorm(xs, g2_ref[...]).astype(BF16)], axis=0)
    acc = None
    for c in range(D_FF // FFN_CHUNK):
        cols = slice(c * FFN_CHUNK, (c + 1) * FFN_CHUNK)
        a = jnp.square(jnp.maximum(_dot(h, wup_ref[:, cols]), 0.0)).astype(BF16)
        d = _dot(a, wdown_ref[cols, :])
        acc = d if acc is None else acc + d
    yp, ys = xp + acc[:n_p], xs + acc[n_p:]
    if final_norm:
        yp, ys = _rmsnorm(yp, fg_ref[...]), _rmsnorm(ys, fg_ref[...])
    op_ref[...] = yp
    os_ref[...] = ys


def _ffn(xp2d, xs2d, w, big, layer, final_g, mixer_w32):
    final_norm = not mixer_w32
    n, d = xp2d.shape
    tile = FFN_TILE
    steps = n // tile
    tile_s = xs2d.shape[0] // steps
    assert n % tile == 0 and xs2d.shape[0] == steps * tile_s and tile_s % (2 * SUBLANES) == 0
    const2 = lambda i: (0, 0)
    casts = [_cast_specs(src, layer + 1, steps, lambda i: i) for src in mixer_w32]
    return pl.pallas_call(
        functools.partial(_ffn_kernel, final_norm=final_norm),
        grid=(steps,),
        in_specs=[
            pl.BlockSpec((tile, d), lambda i: (i, 0)),
            pl.BlockSpec((tile_s, d), lambda i: (i, 0)),
            _layer_spec(w["g2"], layer),
            _whole_spec(big["w_up"]),
            _whole_spec(big["w_down"]),
            pl.BlockSpec((1, d), const2),
        ] + [c[0] for c in casts],
        out_specs=(pl.BlockSpec((tile, d), lambda i: (i, 0)),
                   pl.BlockSpec((tile_s, d), lambda i: (i, 0))) + tuple(c[1] for c in casts),
        out_shape=(jax.ShapeDtypeStruct((n, d), F32),
                   jax.ShapeDtypeStruct(xs2d.shape, F32)) + tuple(c[2] for c in casts),
        compiler_params=pltpu.CompilerParams(
            dimension_semantics=("arbitrary",),
            vmem_limit_bytes=VMEM_LIMIT_BYTES),
        name="ffn",
    )(xp2d, xs2d, w["g2"], big["w_up"], big["w_down"], final_g, *mixer_w32)


def _retention_tables(pos, chunk, reps):
    half = RET_HEAD_DIM // 2
    inv = ROPE_BASE ** (-jnp.arange(half, dtype=F32) / half)
    ang = pos.astype(F32)[:, None] * inv[None, :]
    cos, sin = jnp.cos(ang), jnp.sin(ang)
    cos2 = jnp.concatenate([cos, cos], axis=-1)
    sin2 = jnp.concatenate([-sin, sin], axis=-1)
    kscale = RET_HEAD_DIM ** -0.5

    gamma = 1.0 - jnp.exp2(-5.0 - jnp.arange(RET_HEADS, dtype=F32))
    lg = jnp.log(gamma)
    i = jnp.arange(chunk, dtype=F32)
    diff = i[:, None] - i[None, :]
    dmask = jnp.where(diff >= 0, jnp.exp(lg[:, None, None] * jnp.maximum(diff, 0.0)), 0.0)
    q_decay = jnp.exp(lg[:, None] * (i + 1.0))
    k_decay = jnp.exp(lg[:, None] * (chunk - 1.0 - i))
    c_decay = jnp.exp(lg * chunk)

    def lanes(a):
        a = jnp.repeat(a.T[:, :, None], RET_HEAD_DIM, axis=2).reshape(chunk, RET_WIDTH)
        return jnp.tile(a, (reps, 1))

    if reps > 1:
        eye = jnp.eye(reps, dtype=F32)
        dmask = jnp.einsum("ab,hij->haibj", eye, dmask).reshape(
            RET_HEADS, reps * chunk, reps * chunk)
    return {
        "cos": cos2, "sin": sin2,
        "qd": lanes(q_decay), "kd": lanes(k_decay) * kscale,
        "cd": jnp.repeat(c_decay, RET_HEAD_DIM)[None, :],
        "dmask": dmask * kscale,
    }


def kernel(x_prompt, x_sample, state_pool, state_ret, norm1_g, w_in, pool_w, pool_scale,
           gn_g, gn_b, w_out, norm2_g, w_up, w_down, final_g):
    depth = w_in.shape[0]
    B, T, D = x_prompt.shape
    NB, TS, _ = x_sample.shape
    seqs_per_step = NB // (B * (T // PROMPT_TILE))

    tabs_p = _retention_tables(jnp.arange(T, dtype=jnp.int32), RET_CHUNK, 1)
    pos_s = PAST_LEN + jnp.arange(TS, dtype=jnp.int32)
    tabs_s = _retention_tables(jnp.tile(pos_s, seqs_per_step), TS, seqs_per_step)

    yp = x_prompt
    ys = x_sample.reshape(NB * TS, D)
    fg = final_g.reshape(1, D)
    w = {
        "g1": norm1_g.reshape(depth, 1, D),
        "pool_w": pool_w.astype(BF16),
        "pool_scale": pool_scale.reshape(depth, 1, POOL_WIDTH),
        "gn_g": gn_g.reshape(depth, 1, RET_WIDTH),
        "gn_b": gn_b.reshape(depth, 1, RET_WIDTH),
        "g2": norm2_g.reshape(depth, 1, D),
    }
    big = {"w_in": w_in[0].astype(BF16), "w_out": w_out[0].astype(BF16)}
    states = ()
    for l in range(depth):
        x1p, *states, x1s, big["w_up"], big["w_down"] = _mixer(
            yp, ys, state_pool, state_ret, w, big, l, tabs_p, tabs_s, states, (w_up, w_down))
        nxt = (w_in, w_out) if l + 1 < depth else ()
        yp, ys, *nxt16 = _ffn(x1p.reshape(B * T, D), x1s, w, big, l, fg, nxt)
        if nxt16:
            big["w_in"], big["w_out"] = nxt16
        yp = yp.reshape(B, T, D)
    pool_p, ret_p, pool_s, ret_s = states
    return (yp, ys.reshape(NB, TS, D), pool_p, ret_p, pool_s, ret_s)
```

```python
import functools

import jax
import jax.numpy as jnp
from jax import lax
from jax.experimental import pallas as pl
from jax.experimental.pallas import tpu as pltpu

D_MODEL = 1024
POOL_WIDTH = D_MODEL // 2
POOL_WINDOWS = (2, 4, 8, 16)
POOL_GROUP_WIDTH = POOL_WIDTH // len(POOL_WINDOWS)
POOL_CTX = max(POOL_WINDOWS) - 1
POOL_PAD = POOL_CTX + 1
RET_WIDTH = D_MODEL - POOL_WIDTH
RET_HEADS = 4
RET_HEAD_DIM = RET_WIDTH // RET_HEADS
RET_CHUNK = 128
ROPE_BASE = 10000.0
D_FF = 4 * D_MODEL
RMS_EPS = 1e-6
GN_EPS = 1e-5
PAST_LEN = 16384

SUBLANES = 8
VMEM_LIMIT_BYTES = 60 * 1024 * 1024

PROMPT_TILE = 1024
PROMPT_SUB = 512
RET_STAGE_LAG = 2
PROJ_COLS = 256
FFN_TILE = 512
FFN_CHUNK = 1024

TABLE_KEYS = ("cos", "sin", "qd", "kd", "cd", "dmask")

F32 = jnp.float32
BF16 = jnp.bfloat16


def _dot(a, b):
    return jnp.dot(a, b, preferred_element_type=F32)


def _dot_nt(a, b):
    return lax.dot_general(a, b, (((1,), (1,)), ((), ())), preferred_element_type=F32)


def _dot_tn(a, b):
    return lax.dot_general(a, b, (((0,), (0,)), ((), ())), preferred_element_type=F32)


def _rmsnorm(x, g):
    ms = jnp.mean(x * x, axis=-1, keepdims=True)
    return x * lax.rsqrt(ms + RMS_EPS) * g


def _rotate(x, cos, sin_signed):
    return x * cos + pltpu.roll(x, RET_HEAD_DIM // 2, 1) * sin_signed


def _head_norm_gate(o, gate, gn_g, gn_b):
    mu = jnp.mean(o, axis=-1, keepdims=True)
    d = o - mu
    var = jnp.mean(d * d, axis=-1, keepdims=True)
    on = d * lax.rsqrt(var + GN_EPS) * gn_g + gn_b
    return gate * jax.nn.sigmoid(gate) * on


def _head(hh):
    return slice(hh * RET_HEAD_DIM, (hh + 1) * RET_HEAD_DIM)


def _whole_spec(arr):
    zeros = (0,) * arr.ndim
    return pl.BlockSpec(arr.shape, lambda *_: zeros)


def _cast_specs(src, layer, steps, step_of):
    _, rows, cols = src.shape
    assert rows % (steps * 2 * SUBLANES) == 0
    blk = rows // steps
    in_spec = pl.BlockSpec((None, blk, cols), lambda *g: (layer, step_of(*g), 0))
    out_spec = pl.BlockSpec((blk, cols), lambda *g: (step_of(*g), 0))
    return in_spec, out_spec, jax.ShapeDtypeStruct((rows, cols), BF16)


def _stacked_outputs(body, n_in, prev):
    prev = tuple(prev)

    def kernel(*refs):
        body(*refs[:n_in], *refs[n_in + len(prev):])

    specs = [pl.BlockSpec(memory_space=pl.ANY)] * len(prev)
    aliases = {n_in + k: 1 + k for k in range(len(prev))}
    return kernel, specs, aliases, prev


def _interleave(major, minor):
    out, taken = [], 0
    for n, item in enumerate(major):
        out.append(item)
        want = (n + 1) * len(minor) // len(major)
        out.extend(minor[taken:want])
        taken = want
    return out


def _mixer_kernel(x_ref, xs_ref, ctx_ref, s_in_ref,
                  g1_ref, win_ref, poolw_ref, pscale_ref, gng_ref, gnb_ref, wout_ref,
                  cos_ref, sin_ref, qd_ref, kd_ref, cd_ref, dmask_ref,
                  coss_ref, sins_ref, qds_ref, kds_ref, cds_ref, dmasks_ref,
                  wup32_ref, wdown32_ref,
                  x1_ref, pool_out_ref, s_ref, pools_out_ref, ss_out_ref, x1s_ref,
                  wup16_ref, wdown16_ref,
                  ubuf, h_scr, qb_scr, qdec_scr, kb_scr, kdec_scr, vb_scr, g_scr, mix_scr,
                  ue, qraw_s, kraw_s, v_s, g_s, qdec_s, kdec_s, o_s,
                  *, layer, tile, sub, nb, seq):
    t = pl.program_id(1)
    this_layer = slice(layer, layer + 1)
    P, R = POOL_WIDTH, RET_WIDTH
    rows_s = nb * seq
    last_r0 = tile - sub

    def cast_items(src, dst, pieces):
        blk = src.shape[0] // pieces

        def piece(n):
            def run():
                dst[n * blk:(n + 1) * blk, :] = src[n * blk:(n + 1) * blk, :].astype(BF16)
            return run

        return [piece(n) for n in range(pieces)]

    @pl.when(t == 0)
    def _():
        ubuf[0:POOL_PAD, :] = jnp.zeros((POOL_PAD, POOL_WIDTH), F32)
        s_ref[...] = jnp.zeros_like(s_ref)

    def project_items(r0):
        rows = slice(r0, r0 + sub)
        n_rows = sub + rows_s if r0 == 0 else sub

        def norm():
            h_scr[0:sub, :] = _rmsnorm(x_ref[0, rows, :], g1_ref[this_layer, :]).astype(BF16)
            if r0 == 0:
                h_scr[sub:n_rows, :] = _rmsnorm(xs_ref[...], g1_ref[this_layer, :]).astype(BF16)

        def column_group(lo, keep, put_sample):
            def half(off):
                def run():
                    z = _dot(h_scr[0:n_rows, :], win_ref[:, lo + off:lo + off + PROJ_COLS])
                    keep(z[0:sub], off)
                    if r0 == 0:
                        put_sample(z[sub:n_rows], off)
                return run
            return [half(off) for off in range(0, R, PROJ_COLS)]

        def keep_rows(dst, dst_rows, dtype):
            def run(z, off):
                dst[dst_rows, off:off + PROJ_COLS] = z.astype(dtype)
            return run

        def keep_rotated(decay_ref, plain, decayed):
            def run(z, off):
                for c in range(sub // RET_CHUNK):
                    zr = slice(c * RET_CHUNK, (c + 1) * RET_CHUNK)
                    tr = slice(r0 + c * RET_CHUNK, r0 + (c + 1) * RET_CHUNK)
                    for lo in range(0, PROJ_COLS, RET_HEAD_DIM):
                        ls = slice(off + lo, off + lo + RET_HEAD_DIM)
                        rot = _rotate(z[zr, lo:lo + RET_HEAD_DIM], cos_ref[tr, :], sin_ref[tr, :])
                        plain[tr, ls] = rot.astype(BF16)
                        decayed[tr, ls] = (rot * decay_ref[:, ls]).astype(BF16)
            return run

        def put_u(z, off):
            ue[:, POOL_PAD:POOL_PAD + seq, off:off + PROJ_COLS] = z.reshape(nb, seq, PROJ_COLS)

        def put(dst):
            def run(z, off):
                dst[:, off:off + PROJ_COLS] = z
            return run

        u_rows = slice(POOL_PAD + r0, POOL_PAD + r0 + sub)
        return ([norm]
                + column_group(0, keep_rows(ubuf, u_rows, F32), put_u)
                + column_group(P, keep_rotated(qd_ref, qb_scr, qdec_scr),
                               put(qraw_s))
                + column_group(P + R, keep_rotated(kd_ref, kb_scr, kdec_scr),
                               put(kraw_s))
                + column_group(P + 2 * R, keep_rows(vb_scr, rows, BF16), put(v_s))
                + column_group(P + 3 * R, keep_rows(g_scr, rows, F32), put(g_s)))

    def mix_items(r0):
        base = POOL_PAD + r0
        items = []

        def pool_group(gi, win):
            def run():
                sl = slice(gi * POOL_GROUP_WIDTH, (gi + 1) * POOL_GROUP_WIDTH)
                pos = t * tile + r0 + lax.broadcasted_iota(jnp.int32, (sub, 1), 0)
                s = ubuf[base - POOL_PAD:base + sub, sl]
                cur = s[POOL_PAD:]
                shift = 1
                while shift < win:
                    s = s + pltpu.roll(s, shift, 0)
                    shift *= 2
                s = s[POOL_PAD:]
                cnt = jnp.minimum(pos + 1, win).astype(F32)
                pooled = s / cnt - cur
                y = _dot(pooled.astype(BF16), poolw_ref[layer, gi].astype(BF16)) * pscale_ref[this_layer, sl]
                mix_scr[r0:r0 + sub, sl] = y.astype(BF16)
            return run

        live = {}

        def scores(c, hh):
            def run():
                rows = slice(r0 + c * RET_CHUNK, r0 + (c + 1) * RET_CHUNK)
                ls = _head(hh)
                sc = _dot_nt(qb_scr[rows, ls], kb_scr[rows, ls]) * dmask_ref[hh]
                live[(c, hh)] = jnp.concatenate([sc.astype(BF16), qdec_scr[rows, ls]], axis=1)
            return run

        def outputs(c, hh):
            def run():
                rows = slice(r0 + c * RET_CHUNK, r0 + (c + 1) * RET_CHUNK)
                ls = _head(hh)
                lhs = live.pop((c, hh))
                vb = vb_scr[rows, ls]
                state = s_ref[0, hh]
                rhs = jnp.concatenate([vb, state.astype(BF16)], axis=0)
                live[(c, hh, "o")] = _dot(lhs, rhs)
                s_ref[0, hh] = state * cd_ref[:, ls] + _dot_tn(kdec_scr[rows, ls], vb)
            return run

        def normalise(c, hh):
            def run():
                rows = slice(r0 + c * RET_CHUNK, r0 + (c + 1) * RET_CHUNK)
                ls = _head(hh)
                o = live.pop((c, hh, "o"))
                ret = _head_norm_gate(o, g_scr[rows, ls], gng_ref[this_layer, ls], gnb_ref[this_layer, ls])
                mix_scr[rows, P + hh * RET_HEAD_DIM:P + (hh + 1) * RET_HEAD_DIM] = (
                    ret.astype(BF16))
            return run

        def out_project(lo, width):
            def run():
                n_rows = sub + rows_s if r0 == last_r0 else sub
                y = _dot(mix_scr[r0:r0 + n_rows, :], wout_ref[:, lo:lo + width])
                x1_ref[0, r0:r0 + sub, lo:lo + width] = (
                    x_ref[0, r0:r0 + sub, lo:lo + width] + y[0:sub])
                if r0 == last_r0:
                    x1s_ref[:, lo:lo + width] = xs_ref[:, lo:lo + width] + y[sub:n_rows]
            return run

        heads = [(c, hh) for c in range(sub // RET_CHUNK) for hh in range(RET_HEADS)]
        stages = tuple((n * RET_STAGE_LAG, stage)
                       for n, stage in enumerate((scores, outputs, normalise)))
        pools = [pool_group(gi, win) for gi, win in enumerate(POOL_WINDOWS)]
        for n in range(len(heads) + stages[-1][0]):
            for lag, stage in stages:
                if 0 <= n - lag < len(heads):
                    items.append(stage(*heads[n - lag]))
            if n % 2 == 1 and pools:
                items.append(pools.pop(0))
        items.extend(pools)
        half = D_MODEL // 2
        return items, [out_project(0, half), out_project(half, half)]

    def sample_items():
        srows = slice(tile, tile + rows_s)

        def stage_context():
            for b in range(nb):
                ue[b, 1:POOL_PAD, :] = ctx_ref[:, b, :]

        def pool_group(gi, win):
            def run():
                sl = slice(gi * POOL_GROUP_WIDTH, (gi + 1) * POOL_GROUP_WIDTH)
                pos = PAST_LEN + lax.broadcasted_iota(jnp.int32, (1, seq, 1), 1)
                cur = ue[:, POOL_PAD:POOL_PAD + seq, sl]
                s = cur
                for j in range(1, win):
                    s = s + ue[:, POOL_PAD - j:POOL_PAD - j + seq, sl]
                cnt = jnp.minimum(pos + 1, win).astype(F32)
                pooled = (s / cnt - cur).reshape(rows_s, POOL_GROUP_WIDTH)
                y = _dot(pooled.astype(BF16), poolw_ref[layer, gi].astype(BF16)) * pscale_ref[this_layer, sl]
                mix_scr[srows, sl] = y.astype(BF16)
            return run

        def new_context():
            for b in range(nb):
                pools_out_ref[:, b, :] = ue[b, seq + 1:seq + POOL_PAD, :]

        def scores(hh):
            def run():
                ls = _head(hh)
                qr = _rotate(qraw_s[:, ls], coss_ref[...], sins_ref[...])
                kr = _rotate(kraw_s[:, ls], coss_ref[...], sins_ref[...])
                sc = _dot_nt(qr.astype(BF16), kr.astype(BF16)) * dmasks_ref[hh]
                o_s[:, ls] = _dot(sc.astype(BF16), v_s[:, ls].astype(BF16))
                qdec_s[:, ls] = qr * qds_ref[:, ls]
                kdec_s[:, ls] = kr * kds_ref[:, ls]
            return run

        def state(b):
            def run():
                r = slice(b * seq, (b + 1) * seq)
                for hh in range(RET_HEADS):
                    ls = _head(hh)
                    st = s_in_ref[b, hh]
                    vb = v_s[r, ls].astype(BF16)
                    o_s[r, ls] = o_s[r, ls] + _dot(qdec_s[r, ls].astype(BF16), st.astype(BF16))
                    ss_out_ref[b, hh] = (st * cds_ref[:, ls]
                                         + _dot_tn(kdec_s[r, ls].astype(BF16), vb))
            return run

        def normalise(hh):
            def run():
                ls = _head(hh)
                ret = _head_norm_gate(o_s[:, ls], g_s[:, ls], gng_ref[this_layer, ls], gnb_ref[this_layer, ls])
                mix_scr[srows, P + hh * RET_HEAD_DIM:P + (hh + 1) * RET_HEAD_DIM] = (
                    ret.astype(BF16))
            return run

        return ([stage_context]
                + [pool_group(gi, win) for gi, win in enumerate(POOL_WINDOWS)]
                + [new_context]
                + [scores(hh) for hh in range(RET_HEADS)]
                + [state(b) for b in range(nb)]
                + [normalise(hh) for hh in range(RET_HEADS)])

    first = project_items(0)
    middle, closing = [], []
    for r0 in range(0, tile, sub):
        nxt = project_items(r0 + sub) if r0 + sub < tile else []
        items, out_proj = mix_items(r0)
        if r0 == last_r0:
            middle += _interleave(items, nxt)
            closing = out_proj
        else:
            middle += _interleave(items + out_proj, nxt)
    side = sample_items() + cast_items(wup32_ref, wup16_ref, 4) + cast_items(
        wdown32_ref, wdown16_ref, 4)
    for item in first + _interleave(middle, side) + closing:
        item()

    pool_out_ref[0] = ubuf[tile + 1:tile + POOL_PAD, :]
    ubuf[0:POOL_PAD, :] = ubuf[tile:tile + POOL_PAD, :]


def _mixer(x, xs2d, state_pool_t, state_ret, w, big, layer, tabs, tabs_s, prev, ffn_w32):
    B, T, D = x.shape
    depth, n_seq = state_ret.shape[:2]
    assert w["g1"].shape[0] == depth
    tile, sub = PROMPT_TILE, PROMPT_SUB
    n_t = T // tile
    steps = B * n_t
    nb = n_seq // steps
    seq = xs2d.shape[0] // n_seq
    rows_s = nb * seq
    assert T % tile == 0 and tile % sub == 0 and sub % RET_CHUNK == 0
    assert n_seq == nb * steps and seq == SUBLANES and rows_s % (2 * SUBLANES) == 0
    step_of = lambda b, t: b * n_t + t
    tab_spec = pl.BlockSpec((tile, RET_HEAD_DIM), lambda b, t: (t, 0))
    in_specs = [
        pl.BlockSpec((1, tile, D), lambda b, t: (b, t, 0)),
        pl.BlockSpec((rows_s, D), lambda b, t: (step_of(b, t), 0)),
        pl.BlockSpec((None, POOL_CTX, nb, POOL_WIDTH), lambda b, t: (layer, 0, step_of(b, t), 0)),
        pl.BlockSpec((None, nb, RET_HEADS, RET_HEAD_DIM, RET_HEAD_DIM),
                     lambda b, t: (layer, step_of(b, t), 0, 0, 0)),
        _whole_spec(w["g1"]),
        _whole_spec(big["w_in"]),
        _whole_spec(w["pool_w"]),
        _whole_spec(w["pool_scale"]),
        _whole_spec(w["gn_g"]),
        _whole_spec(w["gn_b"]),
        _whole_spec(big["w_out"]),
        tab_spec, tab_spec,
        _whole_spec(tabs["qd"]), _whole_spec(tabs["kd"]), _whole_spec(tabs["cd"]),
        _whole_spec(tabs["dmask"]),
    ] + [_whole_spec(tabs_s[k]) for k in TABLE_KEYS]
    casts = [_cast_specs(src, layer, steps, step_of) for src in ffn_w32]
    in_specs += [c[0] for c in casts]
    out_shape = (
        jax.ShapeDtypeStruct((B, T, D), F32),
        jax.ShapeDtypeStruct((depth, B, POOL_CTX, POOL_WIDTH), F32),
        jax.ShapeDtypeStruct((depth, B, RET_HEADS, RET_HEAD_DIM, RET_HEAD_DIM), F32),
        jax.ShapeDtypeStruct(state_pool_t.shape, F32),
        jax.ShapeDtypeStruct(state_ret.shape, F32),
        jax.ShapeDtypeStruct(xs2d.shape, F32),
    ) + tuple(c[2] for c in casts)
    out_specs = (
        pl.BlockSpec((1, tile, D), lambda b, t: (b, t, 0)),
        pl.BlockSpec((None, 1, POOL_CTX, POOL_WIDTH), lambda b, t: (layer, b, 0, 0)),
        pl.BlockSpec((None, 1, RET_HEADS, RET_HEAD_DIM, RET_HEAD_DIM),
                     lambda b, t: (layer, b, 0, 0, 0)),
        pl.BlockSpec((None, POOL_CTX, nb, POOL_WIDTH), lambda b, t: (layer, 0, step_of(b, t), 0)),
        pl.BlockSpec((None, nb, RET_HEADS, RET_HEAD_DIM, RET_HEAD_DIM),
                     lambda b, t: (layer, step_of(b, t), 0, 0, 0)),
        pl.BlockSpec((rows_s, D), lambda b, t: (step_of(b, t), 0)),
    ) + tuple(c[1] for c in casts)
    body = functools.partial(_mixer_kernel, layer=layer, tile=tile, sub=sub, nb=nb, seq=seq)
    kernel_fn, alias_specs, aliases, alias_args = _stacked_outputs(body, len(in_specs), prev)
    sample_tile = pltpu.VMEM((rows_s, RET_WIDTH), F32)
    scratch = [
        pltpu.VMEM((POOL_PAD + tile, POOL_WIDTH), F32),
        pltpu.VMEM((sub + rows_s, D_MODEL), BF16),
    ] + [pltpu.VMEM((tile, RET_WIDTH), BF16)] * 5 + [
        pltpu.VMEM((tile, RET_WIDTH), F32),
        pltpu.VMEM((tile + rows_s, D_MODEL), BF16),
        pltpu.VMEM((nb, POOL_PAD + seq, POOL_WIDTH), F32),
    ] + [sample_tile] * 7
    return pl.pallas_call(
        kernel_fn,
        grid=(B, n_t),
        in_specs=in_specs + alias_specs,
        out_specs=out_specs,
        out_shape=out_shape,
        scratch_shapes=scratch,
        input_output_aliases=aliases,
        compiler_params=pltpu.CompilerParams(
            dimension_semantics=("arbitrary", "arbitrary"),
            vmem_limit_bytes=VMEM_LIMIT_BYTES,
        ),
        name="mixer",
    )(x, xs2d, state_pool_t, state_ret, w["g1"], big["w_in"], w["pool_w"], w["pool_scale"],
      w["gn_g"], w["gn_b"], big["w_out"],
      *[tabs[k] for k in TABLE_KEYS], *[tabs_s[k] for k in TABLE_KEYS],
      *ffn_w32, *alias_args)


def _ffn_kernel(xp_ref, xs_ref, g2_ref, wup_ref, wdown_ref, fg_ref, *rest, layer, final_norm):
    if final_norm:
        op_ref, os_ref = rest
    else:
        win32_ref, wout32_ref, op_ref, os_ref, win16_ref, wout16_ref = rest
        win16_ref[...] = win32_ref[...].astype(BF16)
        wout16_ref[...] = wout32_ref[...].astype(BF16)
    n_p = xp_ref.shape[0]
    xp, xs = xp_ref[...], xs_ref[...]
    g2 = g2_ref[layer:layer + 1, :]
    h = jnp.concatenate([_rmsnorm(xp, g2).astype(BF16), _rmsnorm(xs, g2).astype(BF16)], axis=0)
    acc = None
    for c in range(D_FF // FFN_CHUNK):
        cols = slice(c * FFN_CHUNK, (c + 1) * FFN_CHUNK)
        a = jnp.square(jnp.maximum(_dot(h, wup_ref[:, cols]), 0.0)).astype(BF16)
        d = _dot(a, wdown_ref[cols, :])
        acc = d if acc is None else acc + d
    yp, ys = xp + acc[:n_p], xs + acc[n_p:]
    if final_norm:
        yp, ys = _rmsnorm(yp, fg_ref[...]), _rmsnorm(ys, fg_ref[...])
    op_ref[...] = yp
    os_ref[...] = ys


def _ffn(xp2d, xs2d, w, big, layer, final_g, mixer_w32):
    final_norm = not mixer_w32
    n, d = xp2d.shape
    tile = FFN_TILE
    steps = n // tile
    tile_s = xs2d.shape[0] // steps
    assert n % tile == 0 and xs2d.shape[0] == steps * tile_s and tile_s % (2 * SUBLANES) == 0
    const2 = lambda i: (0, 0)
    casts = [_cast_specs(src, layer + 1, steps, lambda i: i) for src in mixer_w32]
    return pl.pallas_call(
        functools.partial(_ffn_kernel, layer=layer, final_norm=final_norm),
        grid=(steps,),
        in_specs=[
            pl.BlockSpec((tile, d), lambda i: (i, 0)),
            pl.BlockSpec((tile_s, d), lambda i: (i, 0)),
            _whole_spec(w["g2"]),
            _whole_spec(big["w_up"]),
            _whole_spec(big["w_down"]),
            pl.BlockSpec((1, d), const2),
        ] + [c[0] for c in casts],
        out_specs=(pl.BlockSpec((tile, d), lambda i: (i, 0)),
                   pl.BlockSpec((tile_s, d), lambda i: (i, 0))) + tuple(c[1] for c in casts),
        out_shape=(jax.ShapeDtypeStruct((n, d), F32),
                   jax.ShapeDtypeStruct(xs2d.shape, F32)) + tuple(c[2] for c in casts),
        compiler_params=pltpu.CompilerParams(
            dimension_semantics=("arbitrary",),
            vmem_limit_bytes=VMEM_LIMIT_BYTES),
        name="ffn",
    )(xp2d, xs2d, w["g2"], big["w_up"], big["w_down"], final_g, *mixer_w32)


def _retention_tables(pos, chunk, reps):
    half = RET_HEAD_DIM // 2
    inv = ROPE_BASE ** (-jnp.arange(half, dtype=F32) / half)
    ang = pos.astype(F32)[:, None] * inv[None, :]
    cos, sin = jnp.cos(ang), jnp.sin(ang)
    cos2 = jnp.concatenate([cos, cos], axis=-1)
    sin2 = jnp.concatenate([-sin, sin], axis=-1)
    kscale = RET_HEAD_DIM ** -0.5

    gamma = 1.0 - jnp.exp2(-5.0 - jnp.arange(RET_HEADS, dtype=F32))
    lg = jnp.log(gamma)
    i = jnp.arange(chunk, dtype=F32)
    diff = i[:, None] - i[None, :]
    dmask = jnp.where(diff >= 0, jnp.exp(lg[:, None, None] * jnp.maximum(diff, 0.0)), 0.0)
    q_decay = jnp.exp(lg[:, None] * (i + 1.0))
    k_decay = jnp.exp(lg[:, None] * (chunk - 1.0 - i))
    c_decay = jnp.exp(lg * chunk)

    def lanes(a):
        a = jnp.repeat(a.T[:, :, None], RET_HEAD_DIM, axis=2).reshape(chunk, RET_WIDTH)
        return jnp.tile(a, (reps, 1))

    if reps > 1:
        eye = jnp.eye(reps, dtype=F32)
        dmask = jnp.einsum("ab,hij->haibj", eye, dmask).reshape(
            RET_HEADS, reps * chunk, reps * chunk)
    return {
        "cos": cos2, "sin": sin2,
        "qd": lanes(q_decay), "kd": lanes(k_decay) * kscale,
        "cd": jnp.repeat(c_decay, RET_HEAD_DIM)[None, :],
        "dmask": dmask * kscale,
    }


def kernel(x_prompt, x_sample, state_pool, state_ret, norm1_g, w_in, pool_w, pool_scale,
           gn_g, gn_b, w_out, norm2_g, w_up, w_down, final_g):
    depth = w_in.shape[0]
    B, T, D = x_prompt.shape
    NB, TS, _ = x_sample.shape
    seqs_per_step = NB // (B * (T // PROMPT_TILE))

    tabs_p = _retention_tables(jnp.arange(T, dtype=jnp.int32), RET_CHUNK, 1)
    pos_s = PAST_LEN + jnp.arange(TS, dtype=jnp.int32)
    tabs_s = _retention_tables(jnp.tile(pos_s, seqs_per_step), TS, seqs_per_step)

    state_pool_t = jnp.transpose(state_pool, (0, 2, 1, 3))
    yp = x_prompt
    ys = x_sample.reshape(NB * TS, D)
    fg = final_g.reshape(1, D)
    w = {"g1": norm1_g, "pool_w": pool_w, "pool_scale": pool_scale, "gn_g": gn_g, "gn_b": gn_b,
         "g2": norm2_g}
    big = {"w_in": w_in[0].astype(BF16), "w_out": w_out[0].astype(BF16)}
    states = ()
    for l in range(depth):
        x1p, *states, x1s, big["w_up"], big["w_down"] = _mixer(
            yp, ys, state_pool_t, state_ret, w, big, l, tabs_p, tabs_s, states, (w_up, w_down))
        nxt = (w_in, w_out) if l + 1 < depth else ()
        yp, ys, *nxt16 = _ffn(x1p.reshape(B * T, D), x1s, w, big, l, fg, nxt)
        if nxt16:
            big["w_in"], big["w_out"] = nxt16
        yp = yp.reshape(B, T, D)
    pool_p, ret_p, pool_s, ret_s = states
    return (yp, ys.reshape(NB, TS, D), pool_p, ret_p, jnp.transpose(pool_s, (0, 2, 1, 3)), ret_s)
```

```python
import functools

import jax
import jax.numpy as jnp
from jax import lax
from jax.experimental import pallas as pl
from jax.experimental.pallas import tpu as pltpu

D_MODEL = 1024
POOL_WIDTH = D_MODEL // 2
POOL_WINDOWS = (2, 4, 8, 16)
POOL_GROUP_WIDTH = POOL_WIDTH // len(POOL_WINDOWS)
POOL_CTX = max(POOL_WINDOWS) - 1
POOL_PAD = POOL_CTX + 1
RET_WIDTH = D_MODEL - POOL_WIDTH
RET_HEADS = 4
RET_HEAD_DIM = RET_WIDTH // RET_HEADS
RET_CHUNK = 128
ROPE_BASE = 10000.0
D_FF = 4 * D_MODEL
RMS_EPS = 1e-6
GN_EPS = 1e-5
PAST_LEN = 16384

SUBLANES = 8
VMEM_LIMIT_BYTES = 60 * 1024 * 1024

PROMPT_TILE = 1024
PROMPT_SUB = 512
RET_STAGE_LAG = 2
PROJ_COLS = 256
FFN_TILE = 512
FFN_CHUNK = 1024

TABLE_KEYS = ("cos", "sin", "qd", "kd", "cd", "dmask")

F32 = jnp.float32
BF16 = jnp.bfloat16


def _dot(a, b):
    return jnp.dot(a, b, preferred_element_type=F32)


def _dot_nt(a, b):
    return lax.dot_general(a, b, (((1,), (1,)), ((), ())), preferred_element_type=F32)


def _dot_tn(a, b):
    return lax.dot_general(a, b, (((0,), (0,)), ((), ())), preferred_element_type=F32)


def _rmsnorm(x, g):
    ms = jnp.mean(x * x, axis=-1, keepdims=True)
    return x * lax.rsqrt(ms + RMS_EPS) * g


def _rotate(x, cos, sin_signed):
    return x * cos + pltpu.roll(x, RET_HEAD_DIM // 2, 1) * sin_signed


def _head_norm_gate(o, gate, gn_g, gn_b):
    mu = jnp.mean(o, axis=-1, keepdims=True)
    d = o - mu
    var = jnp.mean(d * d, axis=-1, keepdims=True)
    on = d * lax.rsqrt(var + GN_EPS) * gn_g + gn_b
    return gate * jax.nn.sigmoid(gate) * on


def _head(hh):
    return slice(hh * RET_HEAD_DIM, (hh + 1) * RET_HEAD_DIM)


def _whole_spec(arr):
    zeros = (0,) * arr.ndim
    return pl.BlockSpec(arr.shape, lambda *_: zeros)


def _cast_specs(src, layer, steps, step_of):
    _, rows, cols = src.shape
    assert rows % (steps * 2 * SUBLANES) == 0
    blk = rows // steps
    in_spec = pl.BlockSpec((None, blk, cols), lambda *g: (layer, step_of(*g), 0))
    out_spec = pl.BlockSpec((blk, cols), lambda *g: (step_of(*g), 0))
    return in_spec, out_spec, jax.ShapeDtypeStruct((rows, cols), BF16)


def _stacked_outputs(body, n_in, prev):
    prev = tuple(prev)

    def kernel(*refs):
        body(*refs[:n_in], *refs[n_in + len(prev):])

    specs = [pl.BlockSpec(memory_space=pl.ANY)] * len(prev)
    aliases = {n_in + k: 1 + k for k in range(len(prev))}
    return kernel, specs, aliases, prev


def _interleave(major, minor):
    out, taken = [], 0
    for n, item in enumerate(major):
        out.append(item)
        want = (n + 1) * len(minor) // len(major)
        out.extend(minor[taken:want])
        taken = want
    return out


def _mixer_kernel(x_ref, xs_ref, ctx_ref, s_in_ref,
                  g1_ref, win_ref, poolw_ref, pscale_ref, gng_ref, gnb_ref, wout_ref,
                  cos_ref, sin_ref, qd_ref, kd_ref, cd_ref, dmask_ref,
                  coss_ref, sins_ref, qds_ref, kds_ref, cds_ref, dmasks_ref,
                  wup32_ref, wdown32_ref,
                  x1_ref, pool_out_ref, s_ref, pools_out_ref, ss_out_ref, x1s_ref,
                  wup16_ref, wdown16_ref,
                  ubuf, h_scr, qb_scr, qdec_scr, kbt_scr, kdect_scr, vb_scr, g_scr, mix_scr,
                  ue, qraw_s, kraw_s, v_s, g_s, qdec_s, kdec_s, o_s,
                  *, layer, tile, sub, nb, seq):
    t = pl.program_id(1)
    this_layer = slice(layer, layer + 1)
    P, R = POOL_WIDTH, RET_WIDTH
    rows_s = nb * seq
    last_r0 = tile - sub

    def cast_items(src, dst, pieces):
        blk = src.shape[0] // pieces

        def piece(n):
            def run():
                dst[n * blk:(n + 1) * blk, :] = src[n * blk:(n + 1) * blk, :].astype(BF16)
            return run

        return [piece(n) for n in range(pieces)]

    @pl.when(t == 0)
    def _():
        ubuf[0:POOL_PAD, :] = jnp.zeros((POOL_PAD, POOL_WIDTH), F32)
        s_ref[...] = jnp.zeros_like(s_ref)

    def project_items(r0):
        rows = slice(r0, r0 + sub)
        n_rows = sub + rows_s if r0 == 0 else sub

        def norm():
            h_scr[0:sub, :] = _rmsnorm(x_ref[0, rows, :], g1_ref[this_layer, :]).astype(BF16)
            if r0 == 0:
                h_scr[sub:n_rows, :] = _rmsnorm(xs_ref[...], g1_ref[this_layer, :]).astype(BF16)

        def column_group(lo, keep, put_sample):
            def half(off):
                def run():
                    z = _dot(h_scr[0:n_rows, :], win_ref[:, lo + off:lo + off + PROJ_COLS])
                    keep(z[0:sub], off)
                    if r0 == 0:
                        put_sample(z[sub:n_rows], off)
                return run
            return [half(off) for off in range(0, R, PROJ_COLS)]

        def keep_rows(dst, dst_rows, dtype):
            def run(z, off):
                dst[dst_rows, off:off + PROJ_COLS] = z.astype(dtype)
            return run

        def keep_rotated(decay_ref, plain, decayed, transposed):
            def run(z, off):
                for c in range(sub // RET_CHUNK):
                    zr = slice(c * RET_CHUNK, (c + 1) * RET_CHUNK)
                    tr = slice(r0 + c * RET_CHUNK, r0 + (c + 1) * RET_CHUNK)
                    for lo in range(0, PROJ_COLS, RET_HEAD_DIM):
                        ls = slice(off + lo, off + lo + RET_HEAD_DIM)
                        rot = _rotate(z[zr, lo:lo + RET_HEAD_DIM], cos_ref[tr, :], sin_ref[tr, :])
                        dec = rot * decay_ref[:, ls]
                        if transposed:
                            rot, dec = rot.T, dec.T
                        plain[tr, ls] = rot.astype(BF16)
                        decayed[tr, ls] = dec.astype(BF16)
            return run

        def put_u(z, off):
            ue[:, POOL_PAD:POOL_PAD + seq, off:off + PROJ_COLS] = z.reshape(nb, seq, PROJ_COLS)

        def put(dst):
            def run(z, off):
                dst[:, off:off + PROJ_COLS] = z
            return run

        u_rows = slice(POOL_PAD + r0, POOL_PAD + r0 + sub)
        return ([norm]
                + column_group(0, keep_rows(ubuf, u_rows, F32), put_u)
                + column_group(P, keep_rotated(qd_ref, qb_scr, qdec_scr, False),
                               put(qraw_s))
                + column_group(P + R, keep_rotated(kd_ref, kbt_scr, kdect_scr, True),
                               put(kraw_s))
                + column_group(P + 2 * R, keep_rows(vb_scr, rows, BF16), put(v_s))
                + column_group(P + 3 * R, keep_rows(g_scr, rows, F32), put(g_s)))

    def mix_items(r0):
        base = POOL_PAD + r0
        items = []

        def pool_group(gi, win):
            def run():
                sl = slice(gi * POOL_GROUP_WIDTH, (gi + 1) * POOL_GROUP_WIDTH)
                pos = t * tile + r0 + lax.broadcasted_iota(jnp.int32, (sub, 1), 0)
                s = ubuf[base - POOL_PAD:base + sub, sl]
                cur = s[POOL_PAD:]
                shift = 1
                while shift < win:
                    s = s + pltpu.roll(s, shift, 0)
                    shift *= 2
                s = s[POOL_PAD:]
                cnt = jnp.minimum(pos + 1, win).astype(F32)
                pooled = s / cnt - cur
                y = _dot(pooled.astype(BF16), poolw_ref[layer, gi].astype(BF16)) * pscale_ref[this_layer, sl]
                mix_scr[r0:r0 + sub, sl] = y.astype(BF16)
            return run

        live = {}

        def scores(c, hh):
            def run():
                rows = slice(r0 + c * RET_CHUNK, r0 + (c + 1) * RET_CHUNK)
                ls = _head(hh)
                sc = _dot(qb_scr[rows, ls], kbt_scr[rows, ls]) * dmask_ref[hh]
                live[(c, hh)] = jnp.concatenate([sc.astype(BF16), qdec_scr[rows, ls]], axis=1)
            return run

        def outputs(c, hh):
            def run():
                rows = slice(r0 + c * RET_CHUNK, r0 + (c + 1) * RET_CHUNK)
                ls = _head(hh)
                lhs = live.pop((c, hh))
                vb = vb_scr[rows, ls]
                state = s_ref[0, hh]
                rhs = jnp.concatenate([vb, state.astype(BF16)], axis=0)
                live[(c, hh, "o")] = _dot(lhs, rhs)
                s_ref[0, hh] = state * cd_ref[:, ls] + _dot(kdect_scr[rows, ls], vb)
            return run

        def normalise(c, hh):
            def run():
                rows = slice(r0 + c * RET_CHUNK, r0 + (c + 1) * RET_CHUNK)
                ls = _head(hh)
                o = live.pop((c, hh, "o"))
                ret = _head_norm_gate(o, g_scr[rows, ls], gng_ref[this_layer, ls], gnb_ref[this_layer, ls])
                mix_scr[rows, P + hh * RET_HEAD_DIM:P + (hh + 1) * RET_HEAD_DIM] = (
                    ret.astype(BF16))
            return run

        def out_project(lo, width):
            def run():
                n_rows = sub + rows_s if r0 == last_r0 else sub
                y = _dot(mix_scr[r0:r0 + n_rows, :], wout_ref[:, lo:lo + width])
                x1_ref[0, r0:r0 + sub, lo:lo + width] = (
                    x_ref[0, r0:r0 + sub, lo:lo + width] + y[0:sub])
                if r0 == last_r0:
                    x1s_ref[:, lo:lo + width] = xs_ref[:, lo:lo + width] + y[sub:n_rows]
            return run

        heads = [(c, hh) for c in range(sub // RET_CHUNK) for hh in range(RET_HEADS)]
        stages = tuple((n * RET_STAGE_LAG, stage)
                       for n, stage in enumerate((scores, outputs, normalise)))
        pools = [pool_group(gi, win) for gi, win in enumerate(POOL_WINDOWS)]
        for n in range(len(heads) + stages[-1][0]):
            for lag, stage in stages:
                if 0 <= n - lag < len(heads):
                    items.append(stage(*heads[n - lag]))
            if n % 2 == 1 and pools:
                items.append(pools.pop(0))
        items.extend(pools)
        half = D_MODEL // 2
        return items, [out_project(0, half), out_project(half, half)]

    def sample_items():
        srows = slice(tile, tile + rows_s)

        def stage_context():
            for b in range(nb):
                ue[b, 1:POOL_PAD, :] = ctx_ref[:, b, :]

        def pool_group(gi, win):
            def run():
                sl = slice(gi * POOL_GROUP_WIDTH, (gi + 1) * POOL_GROUP_WIDTH)
                pos = PAST_LEN + lax.broadcasted_iota(jnp.int32, (1, seq, 1), 1)
                cur = ue[:, POOL_PAD:POOL_PAD + seq, sl]
                s = cur
                for j in range(1, win):
                    s = s + ue[:, POOL_PAD - j:POOL_PAD - j + seq, sl]
                cnt = jnp.minimum(pos + 1, win).astype(F32)
                pooled = (s / cnt - cur).reshape(rows_s, POOL_GROUP_WIDTH)
                y = _dot(pooled.astype(BF16), poolw_ref[layer, gi].astype(BF16)) * pscale_ref[this_layer, sl]
                mix_scr[srows, sl] = y.astype(BF16)
            return run

        def new_context():
            for b in range(nb):
                pools_out_ref[:, b, :] = ue[b, seq + 1:seq + POOL_PAD, :]

        def scores(hh):
            def run():
                ls = _head(hh)
                qr = _rotate(qraw_s[:, ls], coss_ref[...], sins_ref[...])
                kr = _rotate(kraw_s[:, ls], coss_ref[...], sins_ref[...])
                sc = _dot_nt(qr.astype(BF16), kr.astype(BF16)) * dmasks_ref[hh]
                o_s[:, ls] = _dot(sc.astype(BF16), v_s[:, ls].astype(BF16))
                qdec_s[:, ls] = qr * qds_ref[:, ls]
                kdec_s[:, ls] = kr * kds_ref[:, ls]
            return run

        def state(b):
            def run():
                r = slice(b * seq, (b + 1) * seq)
                for hh in range(RET_HEADS):
                    ls = _head(hh)
                    st = s_in_ref[b, hh]
                    vb = v_s[r, ls].astype(BF16)
                    o_s[r, ls] = o_s[r, ls] + _dot(qdec_s[r, ls].astype(BF16), st.astype(BF16))
                    ss_out_ref[b, hh] = (st * cds_ref[:, ls]
                                         + _dot_tn(kdec_s[r, ls].astype(BF16), vb))
            return run

        def normalise(hh):
            def run():
                ls = _head(hh)
                ret = _head_norm_gate(o_s[:, ls], g_s[:, ls], gng_ref[this_layer, ls], gnb_ref[this_layer, ls])
                mix_scr[srows, P + hh * RET_HEAD_DIM:P + (hh + 1) * RET_HEAD_DIM] = (
                    ret.astype(BF16))
            return run

        return ([stage_context]
                + [pool_group(gi, win) for gi, win in enumerate(POOL_WINDOWS)]
                + [new_context]
                + [scores(hh) for hh in range(RET_HEADS)]
                + [state(b) for b in range(nb)]
                + [normalise(hh) for hh in range(RET_HEADS)])

    first = project_items(0)
    middle, closing = [], []
    for r0 in range(0, tile, sub):
        nxt = project_items(r0 + sub) if r0 + sub < tile else []
        items, out_proj = mix_items(r0)
        if r0 == last_r0:
            middle += _interleave(items, nxt)
            closing = out_proj
        else:
            middle += _interleave(items + out_proj, nxt)
    side = sample_items() + cast_items(wup32_ref, wup16_ref, 4) + cast_items(
        wdown32_ref, wdown16_ref, 4)
    for item in first + _interleave(middle, side) + closing:
        item()

    pool_out_ref[0] = ubuf[tile + 1:tile + POOL_PAD, :]
    ubuf[0:POOL_PAD, :] = ubuf[tile:tile + POOL_PAD, :]


def _mixer(x, xs2d, state_pool_t, state_ret, w, big, layer, tabs, tabs_s, prev, ffn_w32):
    B, T, D = x.shape
    depth, n_seq = state_ret.shape[:2]
    assert w["g1"].shape[0] == depth
    tile, sub = PROMPT_TILE, PROMPT_SUB
    n_t = T // tile
    steps = B * n_t
    nb = n_seq // steps
    seq = xs2d.shape[0] // n_seq
    rows_s = nb * seq
    assert T % tile == 0 and tile % sub == 0 and sub % RET_CHUNK == 0
    assert n_seq == nb * steps and seq == SUBLANES and rows_s % (2 * SUBLANES) == 0
    step_of = lambda b, t: b * n_t + t
    tab_spec = pl.BlockSpec((tile, RET_HEAD_DIM), lambda b, t: (t, 0))
    in_specs = [
        pl.BlockSpec((1, tile, D), lambda b, t: (b, t, 0)),
        pl.BlockSpec((rows_s, D), lambda b, t: (step_of(b, t), 0)),
        pl.BlockSpec((None, POOL_CTX, nb, POOL_WIDTH), lambda b, t: (layer, 0, step_of(b, t), 0)),
        pl.BlockSpec((None, nb, RET_HEADS, RET_HEAD_DIM, RET_HEAD_DIM),
                     lambda b, t: (layer, step_of(b, t), 0, 0, 0)),
        _whole_spec(w["g1"]),
        _whole_spec(big["w_in"]),
        _whole_spec(w["pool_w"]),
        _whole_spec(w["pool_scale"]),
        _whole_spec(w["gn_g"]),
        _whole_spec(w["gn_b"]),
        _whole_spec(big["w_out"]),
        tab_spec, tab_spec,
        _whole_spec(tabs["qd"]), _whole_spec(tabs["kd"]), _whole_spec(tabs["cd"]),
        _whole_spec(tabs["dmask"]),
    ] + [_whole_spec(tabs_s[k]) for k in TABLE_KEYS]
    casts = [_cast_specs(src, layer, steps, step_of) for src in ffn_w32]
    in_specs += [c[0] for c in casts]
    out_shape = (
        jax.ShapeDtypeStruct((B, T, D), F32),
        jax.ShapeDtypeStruct((depth, B, POOL_CTX, POOL_WIDTH), F32),
        jax.ShapeDtypeStruct((depth, B, RET_HEADS, RET_HEAD_DIM, RET_HEAD_DIM), F32),
        jax.ShapeDtypeStruct(state_pool_t.shape, F32),
        jax.ShapeDtypeStruct(state_ret.shape, F32),
        jax.ShapeDtypeStruct(xs2d.shape, F32),
    ) + tuple(c[2] for c in casts)
    out_specs = (
        pl.BlockSpec((1, tile, D), lambda b, t: (b, t, 0)),
        pl.BlockSpec((None, 1, POOL_CTX, POOL_WIDTH), lambda b, t: (layer, b, 0, 0)),
        pl.BlockSpec((None, 1, RET_HEADS, RET_HEAD_DIM, RET_HEAD_DIM),
                     lambda b, t: (layer, b, 0, 0, 0)),
        pl.BlockSpec((None, POOL_CTX, nb, POOL_WIDTH), lambda b, t: (layer, 0, step_of(b, t), 0)),
        pl.BlockSpec((None, nb, RET_HEADS, RET_HEAD_DIM, RET_HEAD_DIM),
                     lambda b, t: (layer, step_of(b, t), 0, 0, 0)),
        pl.BlockSpec((rows_s, D), lambda b, t: (step_of(b, t), 0)),
    ) + tuple(c[1] for c in casts)
    body = functools.partial(_mixer_kernel, layer=layer, tile=tile, sub=sub, nb=nb, seq=seq)
    kernel_fn, alias_specs, aliases, alias_args = _stacked_outputs(body, len(in_specs), prev)
    sample_tile = pltpu.VMEM((rows_s, RET_WIDTH), F32)
    scratch = [
        pltpu.VMEM((POOL_PAD + tile, POOL_WIDTH), F32),
        pltpu.VMEM((sub + rows_s, D_MODEL), BF16),
    ] + [pltpu.VMEM((tile, RET_WIDTH), BF16)] * 5 + [
        pltpu.VMEM((tile, RET_WIDTH), F32),
        pltpu.VMEM((tile + rows_s, D_MODEL), BF16),
        pltpu.VMEM((nb, POOL_PAD + seq, POOL_WIDTH), F32),
    ] + [sample_tile] * 7
    return pl.pallas_call(
        kernel_fn,
        grid=(B, n_t),
        in_specs=in_specs + alias_specs,
        out_specs=out_specs,
        out_shape=out_shape,
        scratch_shapes=scratch,
        input_output_aliases=aliases,
        compiler_params=pltpu.CompilerParams(
            dimension_semantics=("arbitrary", "arbitrary"),
            vmem_limit_bytes=VMEM_LIMIT_BYTES,
        ),
        name="mixer",
    )(x, xs2d, state_pool_t, state_ret, w["g1"], big["w_in"], w["pool_w"], w["pool_scale"],
      w["gn_g"], w["gn_b"], big["w_out"],
      *[tabs[k] for k in TABLE_KEYS], *[tabs_s[k] for k in TABLE_KEYS],
      *ffn_w32, *alias_args)


def _ffn_kernel(xp_ref, xs_ref, g2_ref, wup_ref, wdown_ref, fg_ref, *rest, layer, final_norm):
    if final_norm:
        op_ref, os_ref = rest
    else:
        win32_ref, wout32_ref, op_ref, os_ref, win16_ref, wout16_ref = rest
        win16_ref[...] = win32_ref[...].astype(BF16)
        wout16_ref[...] = wout32_ref[...].astype(BF16)
    n_p = xp_ref.shape[0]
    xp, xs = xp_ref[...], xs_ref[...]
    g2 = g2_ref[layer:layer + 1, :]
    h = jnp.concatenate([_rmsnorm(xp, g2).astype(BF16), _rmsnorm(xs, g2).astype(BF16)], axis=0)
    acc = None
    for c in range(D_FF // FFN_CHUNK):
        cols = slice(c * FFN_CHUNK, (c + 1) * FFN_CHUNK)
        a = jnp.square(jnp.maximum(_dot(h, wup_ref[:, cols]), 0.0)).astype(BF16)
        d = _dot(a, wdown_ref[cols, :])
        acc = d if acc is None else acc + d
    yp, ys = xp + acc[:n_p], xs + acc[n_p:]
    if final_norm:
        yp, ys = _rmsnorm(yp, fg_ref[...]), _rmsnorm(ys, fg_ref[...])
    op_ref[...] = yp
    os_ref[...] = ys


def _ffn(xp2d, xs2d, w, big, layer, final_g, mixer_w32):
    final_norm = not mixer_w32
    n, d = xp2d.shape
    tile = FFN_TILE
    steps = n // tile
    tile_s = xs2d.shape[0] // steps
    assert n % tile == 0 and xs2d.shape[0] == steps * tile_s and tile_s % (2 * SUBLANES) == 0
    const2 = lambda i: (0, 0)
    casts = [_cast_specs(src, layer + 1, steps, lambda i: i) for src in mixer_w32]
    return pl.pallas_call(
        functools.partial(_ffn_kernel, layer=layer, final_norm=final_norm),
        grid=(steps,),
        in_specs=[
            pl.BlockSpec((tile, d), lambda i: (i, 0)),
            pl.BlockSpec((tile_s, d), lambda i: (i, 0)),
            _whole_spec(w["g2"]),
            _whole_spec(big["w_up"]),
            _whole_spec(big["w_down"]),
            pl.BlockSpec((1, d), const2),
        ] + [c[0] for c in casts],
        out_specs=(pl.BlockSpec((tile, d), lambda i: (i, 0)),
                   pl.BlockSpec((tile_s, d), lambda i: (i, 0))) + tuple(c[1] for c in casts),
        out_shape=(jax.ShapeDtypeStruct((n, d), F32),
                   jax.ShapeDtypeStruct(xs2d.shape, F32)) + tuple(c[2] for c in casts),
        compiler_params=pltpu.CompilerParams(
            dimension_semantics=("arbitrary",),
            vmem_limit_bytes=VMEM_LIMIT_BYTES),
        name="ffn",
    )(xp2d, xs2d, w["g2"], big["w_up"], big["w_down"], final_g, *mixer_w32)


def _retention_tables(pos, chunk, reps):
    half = RET_HEAD_DIM // 2
    inv = ROPE_BASE ** (-jnp.arange(half, dtype=F32) / half)
    ang = pos.astype(F32)[:, None] * inv[None, :]
    cos, sin = jnp.cos(ang), jnp.sin(ang)
    cos2 = jnp.concatenate([cos, cos], axis=-1)
    sin2 = jnp.concatenate([-sin, sin], axis=-1)
    kscale = RET_HEAD_DIM ** -0.5

    gamma = 1.0 - jnp.exp2(-5.0 - jnp.arange(RET_HEADS, dtype=F32))
    lg = jnp.log(gamma)
    i = jnp.arange(chunk, dtype=F32)
    diff = i[:, None] - i[None, :]
    dmask = jnp.where(diff >= 0, jnp.exp(lg[:, None, None] * jnp.maximum(diff, 0.0)), 0.0)
    q_decay = jnp.exp(lg[:, None] * (i + 1.0))
    k_decay = jnp.exp(lg[:, None] * (chunk - 1.0 - i))
    c_decay = jnp.exp(lg * chunk)

    def lanes(a):
        a = jnp.repeat(a.T[:, :, None], RET_HEAD_DIM, axis=2).reshape(chunk, RET_WIDTH)
        return jnp.tile(a, (reps, 1))

    if reps > 1:
        eye = jnp.eye(reps, dtype=F32)
        dmask = jnp.einsum("ab,hij->haibj", eye, dmask).reshape(
            RET_HEADS, reps * chunk, reps * chunk)
    return {
        "cos": cos2, "sin": sin2,
        "qd": lanes(q_decay), "kd": lanes(k_decay) * kscale,
        "cd": jnp.repeat(c_decay, RET_HEAD_DIM)[None, :],
        "dmask": dmask * kscale,
    }


def kernel(x_prompt, x_sample, state_pool, state_ret, norm1_g, w_in, pool_w, pool_scale,
           gn_g, gn_b, w_out, norm2_g, w_up, w_down, final_g):
    depth = w_in.shape[0]
    B, T, D = x_prompt.shape
    NB, TS, _ = x_sample.shape
    seqs_per_step = NB // (B * (T // PROMPT_TILE))

    tabs_p = _retention_tables(jnp.arange(T, dtype=jnp.int32), RET_CHUNK, 1)
    pos_s = PAST_LEN + jnp.arange(TS, dtype=jnp.int32)
    tabs_s = _retention_tables(jnp.tile(pos_s, seqs_per_step), TS, seqs_per_step)

    state_pool_t = jnp.transpose(state_pool, (0, 2, 1, 3))
    yp = x_prompt
    ys = x_sample.reshape(NB * TS, D)
    fg = final_g.reshape(1, D)
    w = {"g1": norm1_g, "pool_w": pool_w, "pool_scale": pool_scale, "gn_g": gn_g, "gn_b": gn_b,
         "g2": norm2_g}
    big = {"w_in": w_in[0].astype(BF16), "w_out": w_out[0].astype(BF16)}
    states = ()
    for l in range(depth):
        x1p, *states, x1s, big["w_up"], big["w_down"] = _mixer(
            yp, ys, state_pool_t, state_ret, w, big, l, tabs_p, tabs_s, states, (w_up, w_down))
        nxt = (w_in, w_out) if l + 1 < depth else ()
        yp, ys, *nxt16 = _ffn(x1p.reshape(B * T, D), x1s, w, big, l, fg, nxt)
        if nxt16:
            big["w_in"], big["w_out"] = nxt16
        yp = yp.reshape(B, T, D)
    pool_p, ret_p, pool_s, ret_s = states
    return (yp, ys.reshape(NB, TS, D), pool_p, ret_p, jnp.transpose(pool_s, (0, 2, 1, 3)), ret_s)
```

```python
import functools
import math

import jax
import jax.numpy as jnp
from jax import lax
from jax.experimental import pallas as pl
from jax.experimental.pallas import tpu as pltpu

D_MODEL = 1024
POOL_WIDTH = D_MODEL // 2
POOL_WINDOWS = (2, 4, 8, 16)
POOL_GROUP_WIDTH = POOL_WIDTH // len(POOL_WINDOWS)
POOL_CTX = max(POOL_WINDOWS) - 1
POOL_PAD = POOL_CTX + 1
RET_WIDTH = D_MODEL - POOL_WIDTH
RET_HEADS = 4
RET_HEAD_DIM = RET_WIDTH // RET_HEADS
RET_CHUNK = 128
ROPE_BASE = 10000.0
D_FF = 4 * D_MODEL
RMS_EPS = 1e-6
GN_EPS = 1e-5
PAST_LEN = 16384

SUBLANES = 8
VMEM_LIMIT_BYTES = 60 * 1024 * 1024

PROMPT_TILE = 1024
PROMPT_SUB = 512
RET_STAGE_LAG = 2
PROJ_COLS = 256
FFN_TILE = 512
FFN_CHUNK = 1024
PREP_STEPS = 8

TABLE_KEYS = ("cos", "sin", "qd", "kd", "cd", "dmask")

F32 = jnp.float32
BF16 = jnp.bfloat16


def _dot(a, b):
    return jnp.dot(a, b, preferred_element_type=F32)


def _dot_nt(a, b):
    return lax.dot_general(a, b, (((1,), (1,)), ((), ())), preferred_element_type=F32)


def _dot_tn(a, b):
    return lax.dot_general(a, b, (((0,), (0,)), ((), ())), preferred_element_type=F32)


def _rmsnorm(x, g):
    ms = jnp.mean(x * x, axis=-1, keepdims=True)
    return x * lax.rsqrt(ms + RMS_EPS) * g


def _rotate(x, cos, sin_signed):
    return x * cos + pltpu.roll(x, RET_HEAD_DIM // 2, 1) * sin_signed


def _head_norm_gate(o, gate, gn_g, gn_b):
    mu = jnp.mean(o, axis=-1, keepdims=True)
    d = o - mu
    var = jnp.mean(d * d, axis=-1, keepdims=True)
    on = d * lax.rsqrt(var + GN_EPS) * gn_g + gn_b
    return gate * jax.nn.sigmoid(gate) * on


def _head(hh):
    return slice(hh * RET_HEAD_DIM, (hh + 1) * RET_HEAD_DIM)


def _whole_spec(arr):
    zeros = (0,) * arr.ndim
    return pl.BlockSpec(arr.shape, lambda *_: zeros)


def _cast_specs(src, layer, steps, step_of):
    _, rows, cols = src.shape
    assert rows % (steps * 2 * SUBLANES) == 0
    blk = rows // steps
    in_spec = pl.BlockSpec((None, blk, cols), lambda *g: (layer, step_of(*g), 0))
    out_spec = pl.BlockSpec((blk, cols), lambda *g: (step_of(*g), 0))
    return in_spec, out_spec, jax.ShapeDtypeStruct((rows, cols), BF16)


def _stacked_outputs(body, n_in, prev):
    prev = tuple(prev)

    def kernel(*refs):
        body(*refs[:n_in], *refs[n_in + len(prev):])

    specs = [pl.BlockSpec(memory_space=pl.ANY)] * len(prev)
    aliases = {n_in + k: 1 + k for k in range(len(prev))}
    return kernel, specs, aliases, prev


def _interleave(major, minor):
    out, taken = [], 0
    for n, item in enumerate(major):
        out.append(item)
        want = (n + 1) * len(minor) // len(major)
        out.extend(minor[taken:want])
        taken = want
    return out


def _mixer_kernel(x_ref, xs_ref, ctx_ref, s_in_ref,
                  g1_ref, win_ref, poolw_ref, pscale_ref, gng_ref, gnb_ref, wout_ref,
                  cos_ref, sin_ref, qd_ref, kd_ref, cd_ref, dmask_ref,
                  coss_ref, sins_ref, qds_ref, kds_ref, cds_ref, dmasks_ref,
                  wup32_ref, wdown32_ref,
                  x1_ref, pool_out_ref, s_ref, pools_out_ref, ss_out_ref, x1s_ref,
                  wup16_ref, wdown16_ref,
                  ubuf, h_scr, qb_scr, qdec_scr, kbt_scr, kdect_scr, vb_scr, g_scr, mix_scr,
                  ue, qraw_s, kraw_s, v_s, g_s, qdec_s, kdec_s, o_s,
                  *, layer, tile, sub, nb, seq):
    t = pl.program_id(1)
    this_layer = slice(layer, layer + 1)
    P, R = POOL_WIDTH, RET_WIDTH
    rows_s = nb * seq
    last_r0 = tile - sub

    def cast_items(src, dst, pieces):
        blk = src.shape[0] // pieces

        def piece(n):
            def run():
                dst[n * blk:(n + 1) * blk, :] = src[n * blk:(n + 1) * blk, :].astype(BF16)
            return run

        return [piece(n) for n in range(pieces)]

    @pl.when(t == 0)
    def _():
        ubuf[0:POOL_PAD, :] = jnp.zeros((POOL_PAD, POOL_WIDTH), F32)
        s_ref[...] = jnp.zeros_like(s_ref)

    def project_items(r0):
        rows = slice(r0, r0 + sub)
        n_rows = sub + rows_s if r0 == 0 else sub

        def norm():
            h_scr[0:sub, :] = _rmsnorm(x_ref[0, rows, :], g1_ref[this_layer, :]).astype(BF16)
            if r0 == 0:
                h_scr[sub:n_rows, :] = _rmsnorm(xs_ref[...], g1_ref[this_layer, :]).astype(BF16)

        def column_group(lo, keep, put_sample):
            def half(off):
                def run():
                    z = _dot(h_scr[0:n_rows, :], win_ref[:, lo + off:lo + off + PROJ_COLS])
                    keep(z[0:sub], off)
                    if r0 == 0:
                        put_sample(z[sub:n_rows], off)
                return run
            return [half(off) for off in range(0, R, PROJ_COLS)]

        def keep_rows(dst, dst_rows, dtype):
            def run(z, off):
                dst[dst_rows, off:off + PROJ_COLS] = z.astype(dtype)
            return run

        def keep_rotated(decay_ref, plain, decayed, transposed):
            def run(z, off):
                for c in range(sub // RET_CHUNK):
                    zr = slice(c * RET_CHUNK, (c + 1) * RET_CHUNK)
                    tr = slice(r0 + c * RET_CHUNK, r0 + (c + 1) * RET_CHUNK)
                    for lo in range(0, PROJ_COLS, RET_HEAD_DIM):
                        ls = slice(off + lo, off + lo + RET_HEAD_DIM)
                        rot = _rotate(z[zr, lo:lo + RET_HEAD_DIM], cos_ref[tr, :], sin_ref[tr, :])
                        dec = rot * decay_ref[:, ls]
                        if transposed:
                            rot, dec = rot.T, dec.T
                        plain[tr, ls] = rot.astype(BF16)
                        decayed[tr, ls] = dec.astype(BF16)
            return run

        def put_u(z, off):
            ue[:, POOL_PAD:POOL_PAD + seq, off:off + PROJ_COLS] = z.reshape(nb, seq, PROJ_COLS)

        def put(dst):
            def run(z, off):
                dst[:, off:off + PROJ_COLS] = z
            return run

        u_rows = slice(POOL_PAD + r0, POOL_PAD + r0 + sub)
        return ([norm]
                + column_group(0, keep_rows(ubuf, u_rows, F32), put_u)
                + column_group(P, keep_rotated(qd_ref, qb_scr, qdec_scr, False),
                               put(qraw_s))
                + column_group(P + R, keep_rotated(kd_ref, kbt_scr, kdect_scr, True),
                               put(kraw_s))
                + column_group(P + 2 * R, keep_rows(vb_scr, rows, BF16), put(v_s))
                + column_group(P + 3 * R, keep_rows(g_scr, rows, F32), put(g_s)))

    def mix_items(r0):
        base = POOL_PAD + r0
        items = []

        def pool_group(gi, win):
            def run():
                sl = slice(gi * POOL_GROUP_WIDTH, (gi + 1) * POOL_GROUP_WIDTH)
                pos = t * tile + r0 + lax.broadcasted_iota(jnp.int32, (sub, 1), 0)
                s = ubuf[base - POOL_PAD:base + sub, sl]
                cur = s[POOL_PAD:]
                shift = 1
                while shift < win:
                    s = s + pltpu.roll(s, shift, 0)
                    shift *= 2
                s = s[POOL_PAD:]
                cnt = jnp.minimum(pos + 1, win).astype(F32)
                pooled = s / cnt - cur
                y = _dot(pooled.astype(BF16), poolw_ref[layer, gi].astype(BF16)) * pscale_ref[this_layer, sl]
                mix_scr[r0:r0 + sub, sl] = y.astype(BF16)
            return run

        live = {}

        def scores(c, hh):
            def run():
                rows = slice(r0 + c * RET_CHUNK, r0 + (c + 1) * RET_CHUNK)
                ls = _head(hh)
                sc = _dot(qb_scr[rows, ls], kbt_scr[rows, ls]) * dmask_ref[hh]
                live[(c, hh)] = jnp.concatenate([sc.astype(BF16), qdec_scr[rows, ls]], axis=1)
            return run

        def outputs(c, hh):
            def run():
                rows = slice(r0 + c * RET_CHUNK, r0 + (c + 1) * RET_CHUNK)
                ls = _head(hh)
                lhs = live.pop((c, hh))
                vb = vb_scr[rows, ls]
                state = s_ref[0, hh]
                rhs = jnp.concatenate([vb, state.astype(BF16)], axis=0)
                live[(c, hh, "o")] = _dot(lhs, rhs)
                s_ref[0, hh] = state * cd_ref[:, ls] + _dot(kdect_scr[rows, ls], vb)
            return run

        def normalise(c, hh):
            def run():
                rows = slice(r0 + c * RET_CHUNK, r0 + (c + 1) * RET_CHUNK)
                ls = _head(hh)
                o = live.pop((c, hh, "o"))
                ret = _head_norm_gate(o, g_scr[rows, ls], gng_ref[this_layer, ls], gnb_ref[this_layer, ls])
                mix_scr[rows, P + hh * RET_HEAD_DIM:P + (hh + 1) * RET_HEAD_DIM] = (
                    ret.astype(BF16))
            return run

        def out_project(lo, width):
            def run():
                n_rows = sub + rows_s if r0 == last_r0 else sub
                y = _dot(mix_scr[r0:r0 + n_rows, :], wout_ref[:, lo:lo + width])
                x1_ref[0, r0:r0 + sub, lo:lo + width] = (
                    x_ref[0, r0:r0 + sub, lo:lo + width] + y[0:sub])
                if r0 == last_r0:
                    x1s_ref[:, lo:lo + width] = xs_ref[:, lo:lo + width] + y[sub:n_rows]
            return run

        heads = [(c, hh) for c in range(sub // RET_CHUNK) for hh in range(RET_HEADS)]
        stages = tuple((n * RET_STAGE_LAG, stage)
                       for n, stage in enumerate((scores, outputs, normalise)))
        pools = [pool_group(gi, win) for gi, win in enumerate(POOL_WINDOWS)]
        for n in range(len(heads) + stages[-1][0]):
            for lag, stage in stages:
                if 0 <= n - lag < len(heads):
                    items.append(stage(*heads[n - lag]))
            if n % 2 == 1 and pools:
                items.append(pools.pop(0))
        items.extend(pools)
        return items, [out_project(lo, PROJ_COLS) for lo in range(0, D_MODEL, PROJ_COLS)]

    def sample_items():
        srows = slice(tile, tile + rows_s)

        def stage_context():
            for b in range(nb):
                ue[b, 1:POOL_PAD, :] = ctx_ref[:, b, :]

        def pool_group(gi, win):
            def run():
                sl = slice(gi * POOL_GROUP_WIDTH, (gi + 1) * POOL_GROUP_WIDTH)
                pos = PAST_LEN + lax.broadcasted_iota(jnp.int32, (1, seq, 1), 1)
                cur = ue[:, POOL_PAD:POOL_PAD + seq, sl]
                s = cur
                for j in range(1, win):
                    s = s + ue[:, POOL_PAD - j:POOL_PAD - j + seq, sl]
                cnt = jnp.minimum(pos + 1, win).astype(F32)
                pooled = (s / cnt - cur).reshape(rows_s, POOL_GROUP_WIDTH)
                y = _dot(pooled.astype(BF16), poolw_ref[layer, gi].astype(BF16)) * pscale_ref[this_layer, sl]
                mix_scr[srows, sl] = y.astype(BF16)
            return run

        def new_context():
            for b in range(nb):
                pools_out_ref[:, b, :] = ue[b, seq + 1:seq + POOL_PAD, :]

        def scores(hh):
            def run():
                ls = _head(hh)
                qr = _rotate(qraw_s[:, ls], coss_ref[...], sins_ref[...])
                kr = _rotate(kraw_s[:, ls], coss_ref[...], sins_ref[...])
                sc = _dot_nt(qr.astype(BF16), kr.astype(BF16)) * dmasks_ref[hh]
                o_s[:, ls] = _dot(sc.astype(BF16), v_s[:, ls].astype(BF16))
                qdec_s[:, ls] = qr * qds_ref[:, ls]
                kdec_s[:, ls] = kr * kds_ref[:, ls]
            return run

        def state(b):
            def run():
                r = slice(b * seq, (b + 1) * seq)
                for hh in range(RET_HEADS):
                    ls = _head(hh)
                    st = s_in_ref[b, hh]
                    vb = v_s[r, ls].astype(BF16)
                    o_s[r, ls] = o_s[r, ls] + _dot(qdec_s[r, ls].astype(BF16), st.astype(BF16))
                    ss_out_ref[b, hh] = (st * cds_ref[:, ls]
                                         + _dot_tn(kdec_s[r, ls].astype(BF16), vb))
            return run

        def normalise(hh):
            def run():
                ls = _head(hh)
                ret = _head_norm_gate(o_s[:, ls], g_s[:, ls], gng_ref[this_layer, ls], gnb_ref[this_layer, ls])
                mix_scr[srows, P + hh * RET_HEAD_DIM:P + (hh + 1) * RET_HEAD_DIM] = (
                    ret.astype(BF16))
            return run

        return ([stage_context]
                + [pool_group(gi, win) for gi, win in enumerate(POOL_WINDOWS)]
                + [new_context]
                + [scores(hh) for hh in range(RET_HEADS)]
                + [state(b) for b in range(nb)]
                + [normalise(hh) for hh in range(RET_HEADS)])

    first = project_items(0)
    middle, closing = [], []
    for r0 in range(0, tile, sub):
        nxt = project_items(r0 + sub) if r0 + sub < tile else []
        items, out_proj = mix_items(r0)
        if r0 == last_r0:
            middle += _interleave(items, nxt)
            closing = out_proj
        else:
            middle += _interleave(items + out_proj, nxt)
    side = sample_items() + cast_items(wup32_ref, wup16_ref, 4) + cast_items(
        wdown32_ref, wdown16_ref, 4)
    for item in first + _interleave(middle, side) + closing:
        item()

    pool_out_ref[0] = ubuf[tile + 1:tile + POOL_PAD, :]
    ubuf[0:POOL_PAD, :] = ubuf[tile:tile + POOL_PAD, :]


def _mixer(x, xs2d, state_pool_t, state_ret, w, big, layer, tabs, tabs_s, prev, ffn_w32):
    B, T, D = x.shape
    depth, n_seq = state_ret.shape[:2]
    assert w["g1"].shape[0] == depth
    tile, sub = PROMPT_TILE, PROMPT_SUB
    n_t = T // tile
    steps = B * n_t
    nb = n_seq // steps
    seq = xs2d.shape[0] // n_seq
    rows_s = nb * seq
    assert T % tile == 0 and tile % sub == 0 and sub % RET_CHUNK == 0
    assert n_seq == nb * steps and seq == SUBLANES and rows_s % (2 * SUBLANES) == 0
    step_of = lambda b, t: b * n_t + t
    tab_spec = pl.BlockSpec((tile, RET_HEAD_DIM), lambda b, t: (t, 0))
    in_specs = [
        pl.BlockSpec((1, tile, D), lambda b, t: (b, t, 0)),
        pl.BlockSpec((rows_s, D), lambda b, t: (step_of(b, t), 0)),
        pl.BlockSpec((None, POOL_CTX, nb, POOL_WIDTH), lambda b, t: (layer, 0, step_of(b, t), 0)),
        pl.BlockSpec((None, nb, RET_HEADS, RET_HEAD_DIM, RET_HEAD_DIM),
                     lambda b, t: (layer, step_of(b, t), 0, 0, 0)),
        _whole_spec(w["g1"]),
        _whole_spec(big["w_in"]),
        _whole_spec(w["pool_w"]),
        _whole_spec(w["pool_scale"]),
        _whole_spec(w["gn_g"]),
        _whole_spec(w["gn_b"]),
        _whole_spec(big["w_out"]),
        tab_spec, tab_spec,
        _whole_spec(tabs["qd"]), _whole_spec(tabs["kd"]), _whole_spec(tabs["cd"]),
        _whole_spec(tabs["dmask"]),
    ] + [_whole_spec(tabs_s[k]) for k in TABLE_KEYS]
    casts = [_cast_specs(src, layer, steps, step_of) for src in ffn_w32]
    in_specs += [c[0] for c in casts]
    out_shape = (
        jax.ShapeDtypeStruct((B, T, D), F32),
        jax.ShapeDtypeStruct((depth, B, POOL_CTX, POOL_WIDTH), F32),
        jax.ShapeDtypeStruct((depth, B, RET_HEADS, RET_HEAD_DIM, RET_HEAD_DIM), F32),
        jax.ShapeDtypeStruct(state_pool_t.shape, F32),
        jax.ShapeDtypeStruct(state_ret.shape, F32),
        jax.ShapeDtypeStruct(xs2d.shape, F32),
    ) + tuple(c[2] for c in casts)
    out_specs = (
        pl.BlockSpec((1, tile, D), lambda b, t: (b, t, 0)),
        pl.BlockSpec((None, 1, POOL_CTX, POOL_WIDTH), lambda b, t: (layer, b, 0, 0)),
        pl.BlockSpec((None, 1, RET_HEADS, RET_HEAD_DIM, RET_HEAD_DIM),
                     lambda b, t: (layer, b, 0, 0, 0)),
        pl.BlockSpec((None, POOL_CTX, nb, POOL_WIDTH), lambda b, t: (layer, 0, step_of(b, t), 0)),
        pl.BlockSpec((None, nb, RET_HEADS, RET_HEAD_DIM, RET_HEAD_DIM),
                     lambda b, t: (layer, step_of(b, t), 0, 0, 0)),
        pl.BlockSpec((rows_s, D), lambda b, t: (step_of(b, t), 0)),
    ) + tuple(c[1] for c in casts)
    body = functools.partial(_mixer_kernel, layer=layer, tile=tile, sub=sub, nb=nb, seq=seq)
    kernel_fn, alias_specs, aliases, alias_args = _stacked_outputs(body, len(in_specs), prev)
    sample_tile = pltpu.VMEM((rows_s, RET_WIDTH), F32)
    scratch = [
        pltpu.VMEM((POOL_PAD + tile, POOL_WIDTH), F32),
        pltpu.VMEM((sub + rows_s, D_MODEL), BF16),
    ] + [pltpu.VMEM((tile, RET_WIDTH), BF16)] * 5 + [
        pltpu.VMEM((tile, RET_WIDTH), F32),
        pltpu.VMEM((tile + rows_s, D_MODEL), BF16),
        pltpu.VMEM((nb, POOL_PAD + seq, POOL_WIDTH), F32),
    ] + [sample_tile] * 7
    return pl.pallas_call(
        kernel_fn,
        grid=(B, n_t),
        in_specs=in_specs + alias_specs,
        out_specs=out_specs,
        out_shape=out_shape,
        scratch_shapes=scratch,
        input_output_aliases=aliases,
        compiler_params=pltpu.CompilerParams(
            dimension_semantics=("arbitrary", "arbitrary"),
            vmem_limit_bytes=VMEM_LIMIT_BYTES,
        ),
        name="mixer",
    )(x, xs2d, state_pool_t, state_ret, w["g1"], big["w_in"], w["pool_w"], w["pool_scale"],
      w["gn_g"], w["gn_b"], big["w_out"],
      *[tabs[k] for k in TABLE_KEYS], *[tabs_s[k] for k in TABLE_KEYS],
      *ffn_w32, *alias_args)


def _ffn_kernel(xp_ref, xs_ref, g2_ref, wup_ref, wdown_ref, fg_ref, *rest, layer, final_norm):
    if final_norm:
        op_ref, os_ref = rest
    else:
        win32_ref, wout32_ref, op_ref, os_ref, win16_ref, wout16_ref = rest
        win16_ref[...] = win32_ref[...].astype(BF16)
        wout16_ref[...] = wout32_ref[...].astype(BF16)
    n_p = xp_ref.shape[0]
    xp, xs = xp_ref[...], xs_ref[...]
    g2 = g2_ref[layer:layer + 1, :]
    h = jnp.concatenate([_rmsnorm(xp, g2).astype(BF16), _rmsnorm(xs, g2).astype(BF16)], axis=0)
    acc = None
    for c in range(D_FF // FFN_CHUNK):
        cols = slice(c * FFN_CHUNK, (c + 1) * FFN_CHUNK)
        a = jnp.square(jnp.maximum(_dot(h, wup_ref[:, cols]), 0.0)).astype(BF16)
        d = _dot(a, wdown_ref[cols, :])
        acc = d if acc is None else acc + d
    yp, ys = xp + acc[:n_p], xs + acc[n_p:]
    if final_norm:
        yp, ys = _rmsnorm(yp, fg_ref[...]), _rmsnorm(ys, fg_ref[...])
    op_ref[...] = yp
    os_ref[...] = ys


def _ffn(xp2d, xs2d, w, big, layer, final_g, mixer_w32):
    final_norm = not mixer_w32
    n, d = xp2d.shape
    tile = FFN_TILE
    steps = n // tile
    tile_s = xs2d.shape[0] // steps
    assert n % tile == 0 and xs2d.shape[0] == steps * tile_s and tile_s % (2 * SUBLANES) == 0
    const2 = lambda i: (0, 0)
    casts = [_cast_specs(src, layer + 1, steps, lambda i: i) for src in mixer_w32]
    return pl.pallas_call(
        functools.partial(_ffn_kernel, layer=layer, final_norm=final_norm),
        grid=(steps,),
        in_specs=[
            pl.BlockSpec((tile, d), lambda i: (i, 0)),
            pl.BlockSpec((tile_s, d), lambda i: (i, 0)),
            _whole_spec(w["g2"]),
            _whole_spec(big["w_up"]),
            _whole_spec(big["w_down"]),
            pl.BlockSpec((1, d), const2),
        ] + [c[0] for c in casts],
        out_specs=(pl.BlockSpec((tile, d), lambda i: (i, 0)),
                   pl.BlockSpec((tile_s, d), lambda i: (i, 0))) + tuple(c[1] for c in casts),
        out_shape=(jax.ShapeDtypeStruct((n, d), F32),
                   jax.ShapeDtypeStruct(xs2d.shape, F32)) + tuple(c[2] for c in casts),
        compiler_params=pltpu.CompilerParams(
            dimension_semantics=("arbitrary",),
            vmem_limit_bytes=VMEM_LIMIT_BYTES),
        name="ffn",
    )(xp2d, xs2d, w["g2"], big["w_up"], big["w_down"], final_g, *mixer_w32)


LOG_GAMMA = tuple(math.log(1.0 - 2.0 ** (-5.0 - h)) for h in range(RET_HEADS))
KEY_SCALE = RET_HEAD_DIM ** -0.5


def _fill_phases(cos_ref, sin_ref, *, first_row, pos_base, chunk, wrap):
    half = RET_HEAD_DIM // 2
    shape = cos_ref.shape
    lane = lax.broadcasted_iota(jnp.int32, shape, 1)
    row = first_row + lax.broadcasted_iota(jnp.int32, shape, 0)
    inv = jnp.exp((lane & (half - 1)).astype(F32) * (-math.log(ROPE_BASE) / half))
    pos = pos_base + ((row & (chunk - 1)) if wrap else row)
    ang = pos.astype(F32) * inv
    sin = jnp.sin(ang)
    cos_ref[...] = jnp.cos(ang)
    sin_ref[...] = jnp.where(lane < half, -sin, sin)


def _fill_decays(qd_ref, kd_ref, cd_ref, dmask_ref, *, chunk):
    assert chunk & (chunk - 1) == 0
    shift = chunk.bit_length() - 1
    n = qd_ref.shape[0]
    i = (lax.broadcasted_iota(jnp.int32, (n, RET_HEAD_DIM), 0) & (chunk - 1)).astype(F32)
    ri = lax.broadcasted_iota(jnp.int32, (n, n), 0)
    ci = lax.broadcasted_iota(jnp.int32, (n, n), 1)
    diff = ((ri & (chunk - 1)) - (ci & (chunk - 1))).astype(F32)
    same_chunk = (ri >> shift) == (ci >> shift)
    for hh in range(RET_HEADS):
        ls, lg = _head(hh), LOG_GAMMA[hh]
        qd_ref[:, ls] = jnp.exp(lg * (i + 1.0))
        kd_ref[:, ls] = jnp.exp(lg * (chunk - 1.0 - i)) * KEY_SCALE
        cd_ref[:, ls] = jnp.full((1, RET_HEAD_DIM), math.exp(lg * chunk), F32)
        dmask_ref[hh] = jnp.where(same_chunk & (diff >= 0),
                                  jnp.exp(lg * jnp.maximum(diff, 0.0)), 0.0) * KEY_SCALE


def _prepare_kernel(win32_ref, wout32_ref, win16_ref, wout16_ref, *tables, chunk_s):
    win16_ref[...] = win32_ref[...].astype(BF16)
    wout16_ref[...] = wout32_ref[...].astype(BF16)

    step = pl.program_id(0)
    n = len(TABLE_KEYS)
    prompt, sample = tables[:n], tables[n:]
    _fill_phases(*prompt[:2], first_row=step * prompt[0].shape[0], pos_base=0,
                 chunk=RET_CHUNK, wrap=False)

    @pl.when(step == 0)
    def _():
        _fill_decays(*prompt[2:], chunk=RET_CHUNK)
        _fill_phases(*sample[:2], first_row=0, pos_base=PAST_LEN, chunk=chunk_s, wrap=True)
        _fill_decays(*sample[2:], chunk=chunk_s)


def _prepare(w_in, w_out, n_pos, chunk_s, seqs_per_step):
    steps = PREP_STEPS
    casts = [_cast_specs(src, 0, steps, lambda i: i) for src in (w_in, w_out)]

    def table_shapes(rows_pos, rows_dec):
        return {"cos": (rows_pos, RET_HEAD_DIM), "sin": (rows_pos, RET_HEAD_DIM),
                "qd": (rows_dec, RET_WIDTH), "kd": (rows_dec, RET_WIDTH),
                "cd": (1, RET_WIDTH), "dmask": (RET_HEADS, rows_dec, rows_dec)}

    rows_s = chunk_s * seqs_per_step
    shapes = [table_shapes(n_pos, RET_CHUNK)[k] for k in TABLE_KEYS]
    shapes += [table_shapes(rows_s, rows_s)[k] for k in TABLE_KEYS]
    assert n_pos % (steps * SUBLANES) == 0
    table_specs = [pl.BlockSpec(sh, functools.partial(lambda nd, i: (0,) * nd, len(sh)))
                   for sh in shapes]
    table_specs[:2] = [pl.BlockSpec((n_pos // steps, RET_HEAD_DIM), lambda i: (i, 0))] * 2
    outs = pl.pallas_call(
        functools.partial(_prepare_kernel, chunk_s=chunk_s),
        grid=(steps,),
        in_specs=[c[0] for c in casts],
        out_specs=tuple(c[1] for c in casts) + tuple(table_specs),
        out_shape=tuple(c[2] for c in casts) + tuple(
            jax.ShapeDtypeStruct(sh, F32) for sh in shapes),
        compiler_params=pltpu.CompilerParams(
            dimension_semantics=("arbitrary",),
            vmem_limit_bytes=VMEM_LIMIT_BYTES),
        name="prepare",
    )(w_in, w_out)
    n = len(TABLE_KEYS)
    return (outs[0], outs[1], dict(zip(TABLE_KEYS, outs[2:2 + n])),
            dict(zip(TABLE_KEYS, outs[2 + n:])))


def kernel(x_prompt, x_sample, state_pool, state_ret, norm1_g, w_in, pool_w, pool_scale,
           gn_g, gn_b, w_out, norm2_g, w_up, w_down, final_g):
    depth = w_in.shape[0]
    B, T, D = x_prompt.shape
    NB, TS, _ = x_sample.shape
    seqs_per_step = NB // (B * (T // PROMPT_TILE))

    w_in16, w_out16, tabs_p, tabs_s = _prepare(w_in, w_out, T, TS, seqs_per_step)

    state_pool_t = jnp.transpose(state_pool, (0, 2, 1, 3))
    yp = x_prompt
    ys = x_sample.reshape(NB * TS, D)
    fg = final_g.reshape(1, D)
    w = {"g1": norm1_g, "pool_w": pool_w, "pool_scale": pool_scale, "gn_g": gn_g, "gn_b": gn_b,
         "g2": norm2_g}
    big = {"w_in": w_in16, "w_out": w_out16}
    states = ()
    for l in range(depth):
        x1p, *states, x1s, big["w_up"], big["w_down"] = _mixer(
            yp, ys, state_pool_t, state_ret, w, big, l, tabs_p, tabs_s, states, (w_up, w_down))
        nxt = (w_in, w_out) if l + 1 < depth else ()
        yp, ys, *nxt16 = _ffn(x1p.reshape(B * T, D), x1s, w, big, l, fg, nxt)
        if nxt16:
            big["w_in"], big["w_out"] = nxt16
        yp = yp.reshape(B, T, D)
    pool_p, ret_p, pool_s, ret_s = states
    return (yp, ys.reshape(NB, TS, D), pool_p, ret_p, jnp.transpose(pool_s, (0, 2, 1, 3)), ret_s)
```

```python
import functools
import math

import jax
import jax.numpy as jnp
from jax import lax
from jax.experimental import pallas as pl
from jax.experimental.pallas import tpu as pltpu

D_MODEL = 1024
POOL_WIDTH = D_MODEL // 2
POOL_WINDOWS = (2, 4, 8, 16)
POOL_GROUP_WIDTH = POOL_WIDTH // len(POOL_WINDOWS)
POOL_CTX = max(POOL_WINDOWS) - 1
POOL_PAD = POOL_CTX + 1
RET_WIDTH = D_MODEL - POOL_WIDTH
RET_HEADS = 4
RET_HEAD_DIM = RET_WIDTH // RET_HEADS
ROPE_BASE = 10000.0
D_FF = 4 * D_MODEL
RMS_EPS = 1e-6
GN_EPS = 1e-5
PAST_LEN = 16384

SUBLANES = 8
VMEM_LIMIT_BYTES = 60 * 1024 * 1024

PROMPT_TILE = 1024
PROMPT_SUB = 512
PROMPT_CHUNK = 128
RET_STAGE_LAG = 2
PROJ_COLS = 256
FFN_TILE = 512
FFN_CHUNK = 1024

TABLE_KEYS = ("cos", "sin", "qd", "kd", "cd", "dmask")

F32 = jnp.float32
BF16 = jnp.bfloat16


def _dot(a, b):
    return jnp.dot(a, b, preferred_element_type=F32)


def _dot_nt(a, b):
    return lax.dot_general(a, b, (((1,), (1,)), ((), ())), preferred_element_type=F32)


def _dot_tn(a, b):
    return lax.dot_general(a, b, (((0,), (0,)), ((), ())), preferred_element_type=F32)


def _rmsnorm(x, g):
    ms = jnp.mean(x * x, axis=-1, keepdims=True)
    return x * lax.rsqrt(ms + RMS_EPS) * g


def _rotate(x, cos, sin_signed):
    return x * cos + pltpu.roll(x, RET_HEAD_DIM // 2, 1) * sin_signed


def _head_norm_gate(o, gate, gn_g, gn_b):
    mu = jnp.mean(o, axis=-1, keepdims=True)
    d = o - mu
    var = jnp.mean(d * d, axis=-1, keepdims=True)
    on = d * lax.rsqrt(var + GN_EPS) * gn_g + gn_b
    return gate * jax.nn.sigmoid(gate) * on


def _head(hh):
    return slice(hh * RET_HEAD_DIM, (hh + 1) * RET_HEAD_DIM)


def _whole_spec(arr):
    zeros = (0,) * arr.ndim
    return pl.BlockSpec(arr.shape, lambda *_: zeros)


def _cast_specs(src, layer, steps, step_of):
    _, rows, cols = src.shape
    assert rows % (steps * 2 * SUBLANES) == 0
    blk = rows // steps
    in_spec = pl.BlockSpec((None, blk, cols), lambda *g: (layer, step_of(*g), 0))
    out_spec = pl.BlockSpec((blk, cols), lambda *g: (step_of(*g), 0))
    return in_spec, out_spec, jax.ShapeDtypeStruct((rows, cols), BF16)


def _stacked_outputs(body, n_in, prev):
    prev = tuple(prev)

    def kernel(*refs):
        body(*refs[:n_in], *refs[n_in + len(prev):])

    specs = [pl.BlockSpec(memory_space=pl.ANY)] * len(prev)
    aliases = {n_in + k: 1 + k for k in range(len(prev))}
    return kernel, specs, aliases, prev


def _interleave(major, minor):
    out, taken = [], 0
    for n, item in enumerate(major):
        out.append(item)
        want = (n + 1) * len(minor) // len(major)
        out.extend(minor[taken:want])
        taken = want
    return out


def _mixer_kernel(x_ref, xs_ref, ctx_ref, s_in_ref,
                  g1_ref, win_ref, gng_ref, gnb_ref, wout_ref,
                  cos_ref, sin_ref, qd_ref, kd_ref, cd_ref, dmask_ref,
                  coss_ref, sins_ref, qds_ref, kds_ref, cds_ref, dmasks_ref,
                  wup32_ref, wdown32_ref,
                  x1_ref, pool_out_ref, s_ref, pools_out_ref, ss_out_ref, x1s_ref,
                  wup16_ref, wdown16_ref,
                  ubuf, h_scr, qb_scr, qdec_scr, kbt_scr, kdect_scr, vb_scr, g_scr, mix_scr,
                  ue, qraw_s, kraw_s, v_s, g_s, qdec_s, kdec_s, o_s,
                  *, layer, tile, sub, nb, seq):
    t = pl.program_id(1)
    this_layer = slice(layer, layer + 1)
    P, R = POOL_WIDTH, RET_WIDTH
    rows_s = nb * seq
    last_r0 = tile - sub

    def cast_items(src, dst, pieces):
        blk = src.shape[0] // pieces

        def piece(n):
            def run():
                dst[n * blk:(n + 1) * blk, :] = src[n * blk:(n + 1) * blk, :].astype(BF16)
            return run

        return [piece(n) for n in range(pieces)]

    @pl.when(t == 0)
    def _():
        ubuf[0:POOL_PAD, :] = jnp.zeros((POOL_PAD, POOL_WIDTH), F32)
        s_ref[...] = jnp.zeros_like(s_ref)

    def project_items(r0):
        rows = slice(r0, r0 + sub)
        n_rows = sub + rows_s if r0 == 0 else sub

        def norm():
            h_scr[0:sub, :] = _rmsnorm(x_ref[0, rows, :], g1_ref[this_layer, :]).astype(BF16)
            if r0 == 0:
                h_scr[sub:n_rows, :] = _rmsnorm(xs_ref[...], g1_ref[this_layer, :]).astype(BF16)

        def column_group(lo, keep, put_sample):
            def half(off):
                def run():
                    z = _dot(h_scr[0:n_rows, :], win_ref[:, lo + off:lo + off + PROJ_COLS])
                    keep(z[0:sub], off)
                    if r0 == 0:
                        put_sample(z[sub:n_rows], off)
                return run
            return [half(off) for off in range(0, R, PROJ_COLS)]

        def keep_rows(dst, dst_rows, dtype):
            def run(z, off):
                dst[dst_rows, off:off + PROJ_COLS] = z.astype(dtype)
            return run

        def keep_rotated(decay_ref, plain, decayed, transposed):
            def run(z, off):
                for c in range(sub // PROMPT_CHUNK):
                    zr = slice(c * PROMPT_CHUNK, (c + 1) * PROMPT_CHUNK)
                    tr = slice(r0 + c * PROMPT_CHUNK, r0 + (c + 1) * PROMPT_CHUNK)
                    for lo in range(0, PROJ_COLS, RET_HEAD_DIM):
                        ls = slice(off + lo, off + lo + RET_HEAD_DIM)
                        rot = _rotate(z[zr, lo:lo + RET_HEAD_DIM], cos_ref[tr, :], sin_ref[tr, :])
                        dec = rot * decay_ref[:, ls]
                        if transposed:
                            ci = (r0 + c * PROMPT_CHUNK) // PROMPT_CHUNK
                            plain[ci, ls, :] = rot.T.astype(BF16)
                            decayed[ci, ls, :] = dec.T.astype(BF16)
                        else:
                            plain[tr, ls] = rot.astype(BF16)
                            decayed[tr, ls] = dec.astype(BF16)
            return run

        def put_u(z, off):
            ue[:, POOL_PAD:POOL_PAD + seq, off:off + PROJ_COLS] = z.reshape(nb, seq, PROJ_COLS)

        def put(dst):
            def run(z, off):
                dst[:, off:off + PROJ_COLS] = z
            return run

        u_rows = slice(POOL_PAD + r0, POOL_PAD + r0 + sub)
        return ([norm]
                + column_group(0, keep_rows(ubuf, u_rows, F32), put_u)
                + column_group(P, keep_rotated(qd_ref, qb_scr, qdec_scr, False),
                               put(qraw_s))
                + column_group(P + R, keep_rotated(kd_ref, kbt_scr, kdect_scr, True),
                               put(kraw_s))
                + column_group(P + 2 * R, keep_rows(vb_scr, rows, BF16), put(v_s))
                + column_group(P + 3 * R, keep_rows(g_scr, rows, F32), put(g_s)))

    def mix_items(r0):
        base = POOL_PAD + r0
        items = []

        def pool_group(gi, win):
            def run():
                sl = slice(gi * POOL_GROUP_WIDTH, (gi + 1) * POOL_GROUP_WIDTH)
                pos = t * tile + r0 + lax.broadcasted_iota(jnp.int32, (sub, 1), 0)
                s = ubuf[base - POOL_PAD:base + sub, sl]
                cur = s[POOL_PAD:]
                shift = 1
                while shift < win:
                    s = s + pltpu.roll(s, shift, 0)
                    shift *= 2
                s = s[POOL_PAD:]
                cnt = jnp.minimum(pos + 1, win).astype(F32)
                mix_scr[r0:r0 + sub, sl] = (s / cnt - cur).astype(BF16)
            return run

        live = {}

        def scores(c, hh):
            def run():
                rows = slice(r0 + c * PROMPT_CHUNK, r0 + (c + 1) * PROMPT_CHUNK)
                ls = _head(hh)
                ci = (r0 + c * PROMPT_CHUNK) // PROMPT_CHUNK
                sc = _dot(qb_scr[rows, ls], kbt_scr[ci, ls, :]) * dmask_ref[hh]
                live[(c, hh)] = jnp.concatenate([sc.astype(BF16), qdec_scr[rows, ls]], axis=1)
            return run

        def outputs(c, hh):
            def run():
                rows = slice(r0 + c * PROMPT_CHUNK, r0 + (c + 1) * PROMPT_CHUNK)
                ls = _head(hh)
                lhs = live.pop((c, hh))
                vb = vb_scr[rows, ls]
                state = s_ref[0, hh]
                rhs = jnp.concatenate([vb, state.astype(BF16)], axis=0)
                live[(c, hh, "o")] = _dot(lhs, rhs)
                ci = (r0 + c * PROMPT_CHUNK) // PROMPT_CHUNK
                s_ref[0, hh] = state * cd_ref[:, ls] + _dot(kdect_scr[ci, ls, :], vb)
            return run

        def normalise(c, hh):
            def run():
                rows = slice(r0 + c * PROMPT_CHUNK, r0 + (c + 1) * PROMPT_CHUNK)
                ls = _head(hh)
                o = live.pop((c, hh, "o"))
                ret = _head_norm_gate(o, g_scr[rows, ls], gng_ref[this_layer, ls], gnb_ref[this_layer, ls])
                mix_scr[rows, P + hh * RET_HEAD_DIM:P + (hh + 1) * RET_HEAD_DIM] = (
                    ret.astype(BF16))
            return run

        def out_project(lo, width):
            def run():
                n_rows = sub + rows_s if r0 == last_r0 else sub
                y = _dot(mix_scr[r0:r0 + n_rows, :], wout_ref[:, lo:lo + width])
                x1_ref[0, r0:r0 + sub, lo:lo + width] = (
                    x_ref[0, r0:r0 + sub, lo:lo + width] + y[0:sub])
                if r0 == last_r0:
                    x1s_ref[:, lo:lo + width] = xs_ref[:, lo:lo + width] + y[sub:n_rows]
            return run

        heads = [(c, hh) for c in range(sub // PROMPT_CHUNK) for hh in range(RET_HEADS)]
        stages = tuple((n * RET_STAGE_LAG, stage)
                       for n, stage in enumerate((scores, outputs, normalise)))
        pools = [pool_group(gi, win) for gi, win in enumerate(POOL_WINDOWS)]
        for n in range(len(heads) + stages[-1][0]):
            for lag, stage in stages:
                if 0 <= n - lag < len(heads):
                    items.append(stage(*heads[n - lag]))
            if n % 2 == 1 and pools:
                items.append(pools.pop(0))
        items.extend(pools)
        return items, [out_project(lo, PROJ_COLS) for lo in range(0, D_MODEL, PROJ_COLS)]

    def sample_items():
        srows = slice(tile, tile + rows_s)

        def stage_context():
            for b in range(nb):
                ue[b, 1:POOL_PAD, :] = ctx_ref[:, b, :]

        def pool_group(gi, win):
            def run():
                sl = slice(gi * POOL_GROUP_WIDTH, (gi + 1) * POOL_GROUP_WIDTH)
                pos = PAST_LEN + lax.broadcasted_iota(jnp.int32, (1, seq, 1), 1)
                cur = ue[:, POOL_PAD:POOL_PAD + seq, sl]
                s = cur
                for j in range(1, win):
                    s = s + ue[:, POOL_PAD - j:POOL_PAD - j + seq, sl]
                cnt = jnp.minimum(pos + 1, win).astype(F32)
                pooled = (s / cnt - cur).reshape(rows_s, POOL_GROUP_WIDTH)
                mix_scr[srows, sl] = pooled.astype(BF16)
            return run

        def new_context():
            for b in range(nb):
                pools_out_ref[:, b, :] = ue[b, seq + 1:seq + POOL_PAD, :]

        def scores(hh):
            def run():
                ls = _head(hh)
                qr = _rotate(qraw_s[:, ls], coss_ref[...], sins_ref[...])
                kr = _rotate(kraw_s[:, ls], coss_ref[...], sins_ref[...])
                sc = _dot_nt(qr.astype(BF16), kr.astype(BF16)) * dmasks_ref[hh]
                o_s[:, ls] = _dot(sc.astype(BF16), v_s[:, ls].astype(BF16))
                qdec_s[:, ls] = qr * qds_ref[:, ls]
                kdec_s[:, ls] = kr * kds_ref[:, ls]
            return run

        def state(b):
            def run():
                r = slice(b * seq, (b + 1) * seq)
                for hh in range(RET_HEADS):
                    ls = _head(hh)
                    st = s_in_ref[b, hh]
                    vb = v_s[r, ls].astype(BF16)
                    o_s[r, ls] = o_s[r, ls] + _dot(qdec_s[r, ls].astype(BF16), st.astype(BF16))
                    ss_out_ref[b, hh] = (st * cds_ref[:, ls]
                                         + _dot_tn(kdec_s[r, ls].astype(BF16), vb))
            return run

        def normalise(hh):
            def run():
                ls = _head(hh)
                ret = _head_norm_gate(o_s[:, ls], g_s[:, ls], gng_ref[this_layer, ls], gnb_ref[this_layer, ls])
                mix_scr[srows, P + hh * RET_HEAD_DIM:P + (hh + 1) * RET_HEAD_DIM] = (
                    ret.astype(BF16))
            return run

        return ([stage_context]
                + [pool_group(gi, win) for gi, win in enumerate(POOL_WINDOWS)]
                + [new_context]
                + [scores(hh) for hh in range(RET_HEADS)]
                + [state(b) for b in range(nb)]
                + [normalise(hh) for hh in range(RET_HEADS)])

    first = project_items(0)
    middle, closing = [], []
    for r0 in range(0, tile, sub):
        nxt = project_items(r0 + sub) if r0 + sub < tile else []
        items, out_proj = mix_items(r0)
        if r0 == last_r0:
            middle += _interleave(items, nxt)
            closing = out_proj
        else:
            middle += _interleave(items + out_proj, nxt)
    side = sample_items() + cast_items(wup32_ref, wup16_ref, 4) + cast_items(
        wdown32_ref, wdown16_ref, 4)
    for item in first + _interleave(middle, side) + closing:
        item()

    pool_out_ref[0] = ubuf[tile + 1:tile + POOL_PAD, :]
    ubuf[0:POOL_PAD, :] = ubuf[tile:tile + POOL_PAD, :]


def _mixer(x, xs2d, state_pool_t, state_ret, w, big, layer, tabs, tabs_s, prev, ffn_w32):
    B, T, D = x.shape
    depth, n_seq = state_ret.shape[:2]
    assert w["g1"].shape[0] == depth
    tile, sub = PROMPT_TILE, PROMPT_SUB
    n_t = T // tile
    steps = B * n_t
    nb = n_seq // steps
    seq = xs2d.shape[0] // n_seq
    rows_s = nb * seq
    assert T % tile == 0 and tile % sub == 0 and sub % PROMPT_CHUNK == 0
    assert n_seq == nb * steps and seq == SUBLANES and rows_s % (2 * SUBLANES) == 0
    step_of = lambda b, t: b * n_t + t
    tab_spec = pl.BlockSpec((tile, RET_HEAD_DIM), lambda b, t: (t, 0))
    in_specs = [
        pl.BlockSpec((1, tile, D), lambda b, t: (b, t, 0)),
        pl.BlockSpec((rows_s, D), lambda b, t: (step_of(b, t), 0)),
        pl.BlockSpec((None, POOL_CTX, nb, POOL_WIDTH), lambda b, t: (layer, 0, step_of(b, t), 0)),
        pl.BlockSpec((None, nb, RET_HEADS, RET_HEAD_DIM, RET_HEAD_DIM),
                     lambda b, t: (layer, step_of(b, t), 0, 0, 0)),
        _whole_spec(w["g1"]),
        _whole_spec(big["w_in"]),
        _whole_spec(w["gn_g"]),
        _whole_spec(w["gn_b"]),
        pl.BlockSpec((None,) + big["w_out"].shape[1:], lambda b, t: (layer, 0, 0)),
        tab_spec, tab_spec,
        _whole_spec(tabs["qd"]), _whole_spec(tabs["kd"]), _whole_spec(tabs["cd"]),
        _whole_spec(tabs["dmask"]),
    ] + [_whole_spec(tabs_s[k]) for k in TABLE_KEYS]
    casts = [_cast_specs(src, layer, steps, step_of) for src in ffn_w32]
    in_specs += [c[0] for c in casts]
    out_shape = (
        jax.ShapeDtypeStruct((B, T, D), F32),
        jax.ShapeDtypeStruct((depth, B, POOL_CTX, POOL_WIDTH), F32),
        jax.ShapeDtypeStruct((depth, B, RET_HEADS, RET_HEAD_DIM, RET_HEAD_DIM), F32),
        jax.ShapeDtypeStruct(state_pool_t.shape, F32),
        jax.ShapeDtypeStruct(state_ret.shape, F32),
        jax.ShapeDtypeStruct(xs2d.shape, F32),
    ) + tuple(c[2] for c in casts)
    out_specs = (
        pl.BlockSpec((1, tile, D), lambda b, t: (b, t, 0)),
        pl.BlockSpec((None, 1, POOL_CTX, POOL_WIDTH), lambda b, t: (layer, b, 0, 0)),
        pl.BlockSpec((None, 1, RET_HEADS, RET_HEAD_DIM, RET_HEAD_DIM),
                     lambda b, t: (layer, b, 0, 0, 0)),
        pl.BlockSpec((None, POOL_CTX, nb, POOL_WIDTH), lambda b, t: (layer, 0, step_of(b, t), 0)),
        pl.BlockSpec((None, nb, RET_HEADS, RET_HEAD_DIM, RET_HEAD_DIM),
                     lambda b, t: (layer, step_of(b, t), 0, 0, 0)),
        pl.BlockSpec((rows_s, D), lambda b, t: (step_of(b, t), 0)),
    ) + tuple(c[1] for c in casts)
    body = functools.partial(_mixer_kernel, layer=layer, tile=tile, sub=sub, nb=nb, seq=seq)
    kernel_fn, alias_specs, aliases, alias_args = _stacked_outputs(body, len(in_specs), prev)
    sample_tile = pltpu.VMEM((rows_s, RET_WIDTH), F32)
    row_major = pltpu.VMEM((tile, RET_WIDTH), BF16)
    key_tiles = pltpu.VMEM((tile // PROMPT_CHUNK, RET_WIDTH, PROMPT_CHUNK), BF16)
    scratch = [
        pltpu.VMEM((POOL_PAD + tile, POOL_WIDTH), F32),
        pltpu.VMEM((sub + rows_s, D_MODEL), BF16),
    ] + [row_major, row_major, key_tiles, key_tiles, row_major] + [
        pltpu.VMEM((tile, RET_WIDTH), F32),
        pltpu.VMEM((tile + rows_s, D_MODEL), BF16),
        pltpu.VMEM((nb, POOL_PAD + seq, POOL_WIDTH), F32),
    ] + [sample_tile] * 7
    return pl.pallas_call(
        kernel_fn,
        grid=(B, n_t),
        in_specs=in_specs + alias_specs,
        out_specs=out_specs,
        out_shape=out_shape,
        scratch_shapes=scratch,
        input_output_aliases=aliases,
        compiler_params=pltpu.CompilerParams(
            dimension_semantics=("arbitrary", "arbitrary"),
            vmem_limit_bytes=VMEM_LIMIT_BYTES,
        ),
        name="mixer",
    )(x, xs2d, state_pool_t, state_ret, w["g1"], big["w_in"], w["gn_g"], w["gn_b"], big["w_out"],
      *[tabs[k] for k in TABLE_KEYS], *[tabs_s[k] for k in TABLE_KEYS],
      *ffn_w32, *alias_args)


def _ffn_kernel(xp_ref, xs_ref, g2_ref, wup_ref, wdown_ref, fg_ref, *rest, layer, final_norm):
    n_cast = (len(rest) - 2) // 2
    op_ref, os_ref = rest[n_cast:n_cast + 2]
    for src, dst in zip(rest[:n_cast], rest[n_cast + 2:]):
        dst[...] = src[...].astype(BF16)
    n_p = xp_ref.shape[0]
    xp, xs = xp_ref[...], xs_ref[...]
    g2 = g2_ref[layer:layer + 1, :]
    h = jnp.concatenate([_rmsnorm(xp, g2).astype(BF16), _rmsnorm(xs, g2).astype(BF16)], axis=0)
    acc = None
    for c in range(D_FF // FFN_CHUNK):
        cols = slice(c * FFN_CHUNK, (c + 1) * FFN_CHUNK)
        a = jnp.square(jnp.maximum(_dot(h, wup_ref[:, cols]), 0.0)).astype(BF16)
        d = _dot(a, wdown_ref[cols, :])
        acc = d if acc is None else acc + d
    yp, ys = xp + acc[:n_p], xs + acc[n_p:]
    if final_norm:
        yp, ys = _rmsnorm(yp, fg_ref[...]), _rmsnorm(ys, fg_ref[...])
    op_ref[...] = yp
    os_ref[...] = ys


def _ffn(xp2d, xs2d, w, big, layer, final_g, mixer_w32):
    final_norm = not mixer_w32
    n, d = xp2d.shape
    tile = FFN_TILE
    steps = n // tile
    tile_s = xs2d.shape[0] // steps
    assert n % tile == 0 and xs2d.shape[0] == steps * tile_s and tile_s % (2 * SUBLANES) == 0
    const2 = lambda i: (0, 0)
    casts = [_cast_specs(src, layer + 1, steps, lambda i: i) for src in mixer_w32]
    return pl.pallas_call(
        functools.partial(_ffn_kernel, layer=layer, final_norm=final_norm),
        grid=(steps,),
        in_specs=[
            pl.BlockSpec((tile, d), lambda i: (i, 0)),
            pl.BlockSpec((tile_s, d), lambda i: (i, 0)),
            _whole_spec(w["g2"]),
            _whole_spec(big["w_up"]),
            _whole_spec(big["w_down"]),
            pl.BlockSpec((1, d), const2),
        ] + [c[0] for c in casts],
        out_specs=(pl.BlockSpec((tile, d), lambda i: (i, 0)),
                   pl.BlockSpec((tile_s, d), lambda i: (i, 0))) + tuple(c[1] for c in casts),
        out_shape=(jax.ShapeDtypeStruct((n, d), F32),
                   jax.ShapeDtypeStruct(xs2d.shape, F32)) + tuple(c[2] for c in casts),
        compiler_params=pltpu.CompilerParams(
            dimension_semantics=("arbitrary",),
            vmem_limit_bytes=VMEM_LIMIT_BYTES),
        name="ffn",
    )(xp2d, xs2d, w["g2"], big["w_up"], big["w_down"], final_g, *mixer_w32)


LOG_GAMMA = tuple(math.log(1.0 - 2.0 ** (-5.0 - h)) for h in range(RET_HEADS))
KEY_SCALE = RET_HEAD_DIM ** -0.5


def _fill_phases(cos_ref, sin_ref, *, first_row, pos_base, chunk, wrap):
    half = RET_HEAD_DIM // 2
    shape = cos_ref.shape
    lane = lax.broadcasted_iota(jnp.int32, shape, 1)
    row = first_row + lax.broadcasted_iota(jnp.int32, shape, 0)
    inv = jnp.exp((lane & (half - 1)).astype(F32) * (-math.log(ROPE_BASE) / half))
    pos = pos_base + ((row & (chunk - 1)) if wrap else row)
    ang = pos.astype(F32) * inv
    sin = jnp.sin(ang)
    cos_ref[...] = jnp.cos(ang)
    sin_ref[...] = jnp.where(lane < half, -sin, sin)


def _fill_decays(qd_ref, kd_ref, cd_ref, dmask_ref, *, chunk):
    assert chunk & (chunk - 1) == 0
    shift = chunk.bit_length() - 1
    n = qd_ref.shape[0]
    i = (lax.broadcasted_iota(jnp.int32, (n, RET_HEAD_DIM), 0) & (chunk - 1)).astype(F32)
    ri = lax.broadcasted_iota(jnp.int32, (n, n), 0)
    ci = lax.broadcasted_iota(jnp.int32, (n, n), 1)
    diff = ((ri & (chunk - 1)) - (ci & (chunk - 1))).astype(F32)
    same_chunk = (ri >> shift) == (ci >> shift)
    for hh in range(RET_HEADS):
        ls, lg = _head(hh), LOG_GAMMA[hh]
        qd_ref[:, ls] = jnp.exp(lg * (i + 1.0))
        kd_ref[:, ls] = jnp.exp(lg * (chunk - 1.0 - i)) * KEY_SCALE
        cd_ref[:, ls] = jnp.full((1, RET_HEAD_DIM), math.exp(lg * chunk), F32)
        dmask_ref[hh] = jnp.where(same_chunk & (diff >= 0),
                                  jnp.exp(lg * jnp.maximum(diff, 0.0)), 0.0) * KEY_SCALE


def _prepare_kernel(win32_ref, wout32_ref, poolw_ref, pscale_ref, win16_ref, wout16_ref,
                    *tables, chunk_s, depth):
    step = pl.program_id(0)
    win16_ref[...] = win32_ref[...].astype(BF16)

    for l in range(depth):
        @pl.when(step == 2 * l)
        def _(l=l):
            for g in range(len(POOL_WINDOWS)):
                rows = slice(g * POOL_GROUP_WIDTH, (g + 1) * POOL_GROUP_WIDTH)
                group_w = poolw_ref[l, g] * pscale_ref[l:l + 1, rows]
                wout16_ref[rows, :] = _dot(group_w.astype(BF16),
                                           wout32_ref[rows, :].astype(BF16)).astype(BF16)

    @pl.when(step % 2 == 1)
    def _():
        wout16_ref[...] = wout32_ref[...].astype(BF16)

    n = len(TABLE_KEYS)
    prompt, sample = tables[:n], tables[n:]
    _fill_phases(*prompt[:2], first_row=step * prompt[0].shape[0], pos_base=0,
                 chunk=PROMPT_CHUNK, wrap=False)

    @pl.when(step == 0)
    def _():
        _fill_decays(*prompt[2:], chunk=PROMPT_CHUNK)
        _fill_phases(*sample[:2], first_row=0, pos_base=PAST_LEN, chunk=chunk_s, wrap=True)
        _fill_decays(*sample[2:], chunk=chunk_s)


def _prepare(w_in, w_out, pool_w, pool_scale, n_pos, chunk_s, seqs_per_step):
    depth = w_out.shape[0]
    assert POOL_WIDTH * 2 == w_out.shape[1]
    steps = depth * 2
    win_specs = _cast_specs(w_in, 0, steps, lambda i: i)
    wout_block = pl.BlockSpec((None, POOL_WIDTH, D_MODEL), lambda i: (i // 2, i % 2, 0))

    def table_shapes(rows_pos, rows_dec):
        return {"cos": (rows_pos, RET_HEAD_DIM), "sin": (rows_pos, RET_HEAD_DIM),
                "qd": (rows_dec, RET_WIDTH), "kd": (rows_dec, RET_WIDTH),
                "cd": (1, RET_WIDTH), "dmask": (RET_HEADS, rows_dec, rows_dec)}

    rows_s = chunk_s * seqs_per_step
    shapes = [table_shapes(n_pos, PROMPT_CHUNK)[k] for k in TABLE_KEYS]
    shapes += [table_shapes(rows_s, rows_s)[k] for k in TABLE_KEYS]
    assert n_pos % (steps * SUBLANES) == 0
    table_specs = [pl.BlockSpec(sh, functools.partial(lambda nd, i: (0,) * nd, len(sh)))
                   for sh in shapes]
    table_specs[:2] = [pl.BlockSpec((n_pos // steps, RET_HEAD_DIM), lambda i: (i, 0))] * 2
    outs = pl.pallas_call(
        functools.partial(_prepare_kernel, chunk_s=chunk_s, depth=depth),
        grid=(steps,),
        in_specs=[win_specs[0], wout_block, _whole_spec(pool_w), _whole_spec(pool_scale)],
        out_specs=(win_specs[1], wout_block) + tuple(table_specs),
        out_shape=(win_specs[2], jax.ShapeDtypeStruct(w_out.shape, BF16)) + tuple(
            jax.ShapeDtypeStruct(sh, F32) for sh in shapes),
        compiler_params=pltpu.CompilerParams(
            dimension_semantics=("arbitrary",),
            vmem_limit_bytes=VMEM_LIMIT_BYTES),
        name="prepare",
    )(w_in, w_out, pool_w, pool_scale)
    n = len(TABLE_KEYS)
    return (outs[0], outs[1], dict(zip(TABLE_KEYS, outs[2:2 + n])),
            dict(zip(TABLE_KEYS, outs[2 + n:])))


def kernel(x_prompt, x_sample, state_pool, state_ret, norm1_g, w_in, pool_w, pool_scale,
           gn_g, gn_b, w_out, norm2_g, w_up, w_down, final_g):
    depth = w_in.shape[0]
    B, T, D = x_prompt.shape
    NB, TS, _ = x_sample.shape
    seqs_per_step = NB // (B * (T // PROMPT_TILE))

    w_in16, w_out16, tabs_p, tabs_s = _prepare(w_in, w_out, pool_w, pool_scale, T, TS,
                                               seqs_per_step)

    state_pool_t = jnp.transpose(state_pool, (0, 2, 1, 3))
    yp = x_prompt
    ys = x_sample.reshape(NB * TS, D)
    fg = final_g.reshape(1, D)
    w = {"g1": norm1_g, "gn_g": gn_g, "gn_b": gn_b, "g2": norm2_g}
    big = {"w_in": w_in16, "w_out": w_out16}
    states = ()
    for l in range(depth):
        x1p, *states, x1s, big["w_up"], big["w_down"] = _mixer(
            yp, ys, state_pool_t, state_ret, w, big, l, tabs_p, tabs_s, states, (w_up, w_down))
        nxt = (w_in,) if l + 1 < depth else ()
        yp, ys, *nxt16 = _ffn(x1p.reshape(B * T, D), x1s, w, big, l, fg, nxt)
        if nxt16:
            big["w_in"], = nxt16
        yp = yp.reshape(B, T, D)
    pool_p, ret_p, pool_s, ret_s = states
    return (yp, ys.reshape(NB, TS, D), pool_p, ret_p, jnp.transpose(pool_s, (0, 2, 1, 3)), ret_s)
```

```python
import functools
import math

import jax
import jax.numpy as jnp
from jax import lax
from jax.experimental import pallas as pl
from jax.experimental.pallas import tpu as pltpu

D_MODEL = 1024
POOL_WIDTH = D_MODEL // 2
POOL_WINDOWS = (2, 4, 8, 16)
POOL_GROUP_WIDTH = POOL_WIDTH // len(POOL_WINDOWS)
POOL_CTX = max(POOL_WINDOWS) - 1
POOL_PAD = POOL_CTX + 1
RET_WIDTH = D_MODEL - POOL_WIDTH
RET_HEADS = 4
RET_HEAD_DIM = RET_WIDTH // RET_HEADS
ROPE_BASE = 10000.0
D_FF = 4 * D_MODEL
RMS_EPS = 1e-6
GN_EPS = 1e-5
PAST_LEN = 16384

SUBLANES = 8
VMEM_LIMIT_BYTES = 60 * 1024 * 1024

PROMPT_TILE = 1024
PROMPT_SUB = 512
PROMPT_CHUNK = 128
RET_STAGE_LAG = 3
PROJ_COLS = 256
FFN_TILE = 512
FFN_CHUNK = 1024

TABLE_KEYS = ("cos", "sin", "qd", "kd", "cd", "dmask")

F32 = jnp.float32
BF16 = jnp.bfloat16


def _dot(a, b):
    return jnp.dot(a, b, preferred_element_type=F32)


def _dot_nt(a, b):
    return lax.dot_general(a, b, (((1,), (1,)), ((), ())), preferred_element_type=F32)


def _dot_tn(a, b):
    return lax.dot_general(a, b, (((0,), (0,)), ((), ())), preferred_element_type=F32)


def _rmsnorm(x, g):
    ms = jnp.mean(x * x, axis=-1, keepdims=True)
    return x * lax.rsqrt(ms + RMS_EPS) * g


def _rotate(x, cos, sin_signed):
    return x * cos + pltpu.roll(x, RET_HEAD_DIM // 2, 1) * sin_signed


def _head_norm_gate(o, gate, gn_g, gn_b):
    mu = jnp.mean(o, axis=-1, keepdims=True)
    d = o - mu
    var = jnp.mean(d * d, axis=-1, keepdims=True)
    on = d * lax.rsqrt(var + GN_EPS) * gn_g + gn_b
    return gate * jax.nn.sigmoid(gate) * on


def _head(hh):
    return slice(hh * RET_HEAD_DIM, (hh + 1) * RET_HEAD_DIM)


def _whole_spec(arr):
    zeros = (0,) * arr.ndim
    return pl.BlockSpec(arr.shape, lambda *_: zeros)


def _cast_specs(src, layer, steps, step_of):
    _, rows, cols = src.shape
    assert rows % (steps * 2 * SUBLANES) == 0
    blk = rows // steps
    in_spec = pl.BlockSpec((None, blk, cols), lambda *g: (layer, step_of(*g), 0))
    out_spec = pl.BlockSpec((blk, cols), lambda *g: (step_of(*g), 0))
    return in_spec, out_spec, jax.ShapeDtypeStruct((rows, cols), BF16)


def _stacked_outputs(body, n_in, prev):
    prev = tuple(prev)

    def kernel(*refs):
        body(*refs[:n_in], *refs[n_in + len(prev):])

    specs = [pl.BlockSpec(memory_space=pl.ANY)] * len(prev)
    aliases = {n_in + k: 1 + k for k in range(len(prev))}
    return kernel, specs, aliases, prev


def _interleave(major, minor):
    out, taken = [], 0
    for n, item in enumerate(major):
        out.append(item)
        want = (n + 1) * len(minor) // len(major)
        out.extend(minor[taken:want])
        taken = want
    return out


def _mixer_kernel(x_ref, xs_ref, ctx_ref, s_in_ref,
                  g1_ref, win_ref, gng_ref, gnb_ref, wout_ref,
                  cos_ref, sin_ref, qd_ref, kd_ref, cd_ref, dmask_ref,
                  coss_ref, sins_ref, qds_ref, kds_ref, cds_ref, dmasks_ref,
                  wup32_ref, wdown32_ref,
                  x1_ref, pool_out_ref, s_ref, pools_out_ref, ss_out_ref, x1s_ref,
                  wup16_ref, wdown16_ref,
                  ubuf, h_scr, qb_scr, qdec_scr, kbt_scr, kdect_scr, vb_scr, g_scr, mix_scr,
                  ue, qraw_s, kraw_s, v_s, g_s, qdec_s, kdec_s, o_s,
                  *, layer, tile, sub, nb, seq):
    t = pl.program_id(1)
    this_layer = slice(layer, layer + 1)
    P, R = POOL_WIDTH, RET_WIDTH
    rows_s = nb * seq
    last_r0 = tile - sub

    def cast_items(src, dst, pieces):
        blk = src.shape[0] // pieces

        def piece(n):
            def run():
                dst[n * blk:(n + 1) * blk, :] = src[n * blk:(n + 1) * blk, :].astype(BF16)
            return run

        return [piece(n) for n in range(pieces)]

    @pl.when(t == 0)
    def _():
        ubuf[0:POOL_PAD, :] = jnp.zeros((POOL_PAD, POOL_WIDTH), F32)
        s_ref[...] = jnp.zeros_like(s_ref)

    def project_items(r0):
        rows = slice(r0, r0 + sub)
        n_rows = sub + rows_s if r0 == 0 else sub

        def norm():
            h_scr[0:sub, :] = _rmsnorm(x_ref[0, rows, :], g1_ref[this_layer, :]).astype(BF16)
            if r0 == 0:
                h_scr[sub:n_rows, :] = _rmsnorm(xs_ref[...], g1_ref[this_layer, :]).astype(BF16)

        def column_group(lo, keep, put_sample):
            def half(off):
                def run():
                    z = _dot(h_scr[0:n_rows, :], win_ref[:, lo + off:lo + off + PROJ_COLS])
                    keep(z[0:sub], off)
                    if r0 == 0:
                        put_sample(z[sub:n_rows], off)
                return run
            return [half(off) for off in range(0, R, PROJ_COLS)]

        def keep_rows(dst, dst_rows, dtype):
            def run(z, off):
                dst[dst_rows, off:off + PROJ_COLS] = z.astype(dtype)
            return run

        def keep_rotated(decay_ref, plain, decayed, transposed):
            def run(z, off):
                for c in range(sub // PROMPT_CHUNK):
                    zr = slice(c * PROMPT_CHUNK, (c + 1) * PROMPT_CHUNK)
                    tr = slice(r0 + c * PROMPT_CHUNK, r0 + (c + 1) * PROMPT_CHUNK)
                    for lo in range(0, PROJ_COLS, RET_HEAD_DIM):
                        ls = slice(off + lo, off + lo + RET_HEAD_DIM)
                        rot = _rotate(z[zr, lo:lo + RET_HEAD_DIM], cos_ref[tr, :], sin_ref[tr, :])
                        dec = rot * decay_ref[:, ls]
                        if transposed:
                            ci = (r0 + c * PROMPT_CHUNK) // PROMPT_CHUNK
                            plain[ci, ls, :] = rot.T.astype(BF16)
                            decayed[ci, ls, :] = dec.T.astype(BF16)
                        else:
                            plain[tr, ls] = rot.astype(BF16)
                            decayed[tr, ls] = dec.astype(BF16)
            return run

        def put_u(z, off):
            ue[:, POOL_PAD:POOL_PAD + seq, off:off + PROJ_COLS] = z.reshape(nb, seq, PROJ_COLS)

        def put(dst):
            def run(z, off):
                dst[:, off:off + PROJ_COLS] = z
            return run

        u_rows = slice(POOL_PAD + r0, POOL_PAD + r0 + sub)
        return ([norm]
                + column_group(0, keep_rows(ubuf, u_rows, F32), put_u)
                + column_group(P, keep_rotated(qd_ref, qb_scr, qdec_scr, False),
                               put(qraw_s))
                + column_group(P + R, keep_rotated(kd_ref, kbt_scr, kdect_scr, True),
                               put(kraw_s))
                + column_group(P + 2 * R, keep_rows(vb_scr, rows, BF16), put(v_s))
                + column_group(P + 3 * R, keep_rows(g_scr, rows, F32), put(g_s)))

    def mix_items(r0):
        base = POOL_PAD + r0
        items = []

        def pool_group(gi, win):
            def run():
                sl = slice(gi * POOL_GROUP_WIDTH, (gi + 1) * POOL_GROUP_WIDTH)
                pos = t * tile + r0 + lax.broadcasted_iota(jnp.int32, (sub, 1), 0)
                s = ubuf[base - POOL_PAD:base + sub, sl]
                cur = s[POOL_PAD:]
                shift = 1
                while shift < win:
                    s = s + pltpu.roll(s, shift, 0)
                    shift *= 2
                s = s[POOL_PAD:]
                cnt = jnp.minimum(pos + 1, win).astype(F32)
                mix_scr[r0:r0 + sub, sl] = (s / cnt - cur).astype(BF16)
            return run

        live = {}

        def scores(c, hh):
            def run():
                rows = slice(r0 + c * PROMPT_CHUNK, r0 + (c + 1) * PROMPT_CHUNK)
                ls = _head(hh)
                ci = (r0 + c * PROMPT_CHUNK) // PROMPT_CHUNK
                sc = _dot(qb_scr[rows, ls], kbt_scr[ci, ls, :]) * dmask_ref[hh]
                live[(c, hh)] = jnp.concatenate([sc.astype(BF16), qdec_scr[rows, ls]], axis=1)
            return run

        def outputs(c, hh):
            def run():
                rows = slice(r0 + c * PROMPT_CHUNK, r0 + (c + 1) * PROMPT_CHUNK)
                ls = _head(hh)
                lhs = live.pop((c, hh))
                vb = vb_scr[rows, ls]
                state = s_ref[0, hh]
                rhs = jnp.concatenate([vb, state.astype(BF16)], axis=0)
                live[(c, hh, "o")] = _dot(lhs, rhs)
                ci = (r0 + c * PROMPT_CHUNK) // PROMPT_CHUNK
                s_ref[0, hh] = state * cd_ref[:, ls] + _dot(kdect_scr[ci, ls, :], vb)
            return run

        def normalise(c, hh):
            def run():
                rows = slice(r0 + c * PROMPT_CHUNK, r0 + (c + 1) * PROMPT_CHUNK)
                ls = _head(hh)
                o = live.pop((c, hh, "o"))
                ret = _head_norm_gate(o, g_scr[rows, ls], gng_ref[this_layer, ls], gnb_ref[this_layer, ls])
                mix_scr[rows, P + hh * RET_HEAD_DIM:P + (hh + 1) * RET_HEAD_DIM] = (
                    ret.astype(BF16))
            return run

        def out_project(lo, width):
            def run():
                n_rows = sub + rows_s if r0 == last_r0 else sub
                y = _dot(mix_scr[r0:r0 + n_rows, :], wout_ref[:, lo:lo + width])
                x1_ref[0, r0:r0 + sub, lo:lo + width] = (
                    x_ref[0, r0:r0 + sub, lo:lo + width] + y[0:sub])
                if r0 == last_r0:
                    x1s_ref[:, lo:lo + width] = xs_ref[:, lo:lo + width] + y[sub:n_rows]
            return run

        heads = [(c, hh) for c in range(sub // PROMPT_CHUNK) for hh in range(RET_HEADS)]
        stages = tuple((n * RET_STAGE_LAG, stage)
                       for n, stage in enumerate((scores, outputs, normalise)))
        pools = [pool_group(gi, win) for gi, win in enumerate(POOL_WINDOWS)]
        for n in range(len(heads) + stages[-1][0]):
            for lag, stage in stages:
                if 0 <= n - lag < len(heads):
                    items.append(stage(*heads[n - lag]))
            if n % 2 == 1 and pools:
                items.append(pools.pop(0))
        items.extend(pools)
        return items, [out_project(lo, PROJ_COLS) for lo in range(0, D_MODEL, PROJ_COLS)]

    def sample_items():
        srows = slice(tile, tile + rows_s)

        def stage_context():
            for b in range(nb):
                ue[b, 1:POOL_PAD, :] = ctx_ref[:, b, :]

        def pool_group(gi, win):
            def run():
                sl = slice(gi * POOL_GROUP_WIDTH, (gi + 1) * POOL_GROUP_WIDTH)
                pos = PAST_LEN + lax.broadcasted_iota(jnp.int32, (1, seq, 1), 1)
                cur = ue[:, POOL_PAD:POOL_PAD + seq, sl]
                s = cur
                for j in range(1, win):
                    s = s + ue[:, POOL_PAD - j:POOL_PAD - j + seq, sl]
                cnt = jnp.minimum(pos + 1, win).astype(F32)
                pooled = (s / cnt - cur).reshape(rows_s, POOL_GROUP_WIDTH)
                mix_scr[srows, sl] = pooled.astype(BF16)
            return run

        def new_context():
            for b in range(nb):
                pools_out_ref[:, b, :] = ue[b, seq + 1:seq + POOL_PAD, :]

        def scores(hh):
            def run():
                ls = _head(hh)
                qr = _rotate(qraw_s[:, ls], coss_ref[...], sins_ref[...])
                kr = _rotate(kraw_s[:, ls], coss_ref[...], sins_ref[...])
                sc = _dot_nt(qr.astype(BF16), kr.astype(BF16)) * dmasks_ref[hh]
                o_s[:, ls] = _dot(sc.astype(BF16), v_s[:, ls].astype(BF16))
                qdec_s[:, ls] = qr * qds_ref[:, ls]
                kdec_s[:, ls] = kr * kds_ref[:, ls]
            return run

        def state(b):
            def run():
                r = slice(b * seq, (b + 1) * seq)
                for hh in range(RET_HEADS):
                    ls = _head(hh)
                    st = s_in_ref[b, hh]
                    vb = v_s[r, ls].astype(BF16)
                    o_s[r, ls] = o_s[r, ls] + _dot(qdec_s[r, ls].astype(BF16), st.astype(BF16))
                    ss_out_ref[b, hh] = (st * cds_ref[:, ls]
                                         + _dot_tn(kdec_s[r, ls].astype(BF16), vb))
            return run

        def normalise(hh):
            def run():
                ls = _head(hh)
                ret = _head_norm_gate(o_s[:, ls], g_s[:, ls], gng_ref[this_layer, ls], gnb_ref[this_layer, ls])
                mix_scr[srows, P + hh * RET_HEAD_DIM:P + (hh + 1) * RET_HEAD_DIM] = (
                    ret.astype(BF16))
            return run

        return ([stage_context]
                + [pool_group(gi, win) for gi, win in enumerate(POOL_WINDOWS)]
                + [new_context]
                + [scores(hh) for hh in range(RET_HEADS)]
                + [state(b) for b in range(nb)]
                + [normalise(hh) for hh in range(RET_HEADS)])

    first = project_items(0)
    middle, closing = [], []
    for r0 in range(0, tile, sub):
        nxt = project_items(r0 + sub) if r0 + sub < tile else []
        items, out_proj = mix_items(r0)
        if r0 == last_r0:
            middle += _interleave(items, sample_items())
            closing = out_proj
        else:
            middle += _interleave(items + out_proj, nxt)
    side = cast_items(wup32_ref, wup16_ref, 4) + cast_items(wdown32_ref, wdown16_ref, 4)
    for item in first + _interleave(middle, side) + closing:
        item()

    pool_out_ref[0] = ubuf[tile + 1:tile + POOL_PAD, :]
    ubuf[0:POOL_PAD, :] = ubuf[tile:tile + POOL_PAD, :]


def _mixer(x, xs2d, state_pool_t, state_ret, w, big, layer, tabs, tabs_s, prev, ffn_w32):
    B, T, D = x.shape
    depth, n_seq = state_ret.shape[:2]
    assert w["g1"].shape[0] == depth
    tile, sub = PROMPT_TILE, PROMPT_SUB
    n_t = T // tile
    steps = B * n_t
    nb = n_seq // steps
    seq = xs2d.shape[0] // n_seq
    rows_s = nb * seq
    assert T % tile == 0 and tile % sub == 0 and sub % PROMPT_CHUNK == 0
    assert n_seq == nb * steps and seq == SUBLANES and rows_s % (2 * SUBLANES) == 0
    step_of = lambda b, t: b * n_t + t
    tab_spec = pl.BlockSpec((tile, RET_HEAD_DIM), lambda b, t: (t, 0))
    in_specs = [
        pl.BlockSpec((1, tile, D), lambda b, t: (b, t, 0)),
        pl.BlockSpec((rows_s, D), lambda b, t: (step_of(b, t), 0)),
        pl.BlockSpec((None, POOL_CTX, nb, POOL_WIDTH), lambda b, t: (layer, 0, step_of(b, t), 0)),
        pl.BlockSpec((None, nb, RET_HEADS, RET_HEAD_DIM, RET_HEAD_DIM),
                     lambda b, t: (layer, step_of(b, t), 0, 0, 0)),
        _whole_spec(w["g1"]),
        _whole_spec(big["w_in"]),
        _whole_spec(w["gn_g"]),
        _whole_spec(w["gn_b"]),
        pl.BlockSpec((None,) + big["w_out"].shape[1:], lambda b, t: (layer, 0, 0)),
        tab_spec, tab_spec,
        _whole_spec(tabs["qd"]), _whole_spec(tabs["kd"]), _whole_spec(tabs["cd"]),
        _whole_spec(tabs["dmask"]),
    ] + [_whole_spec(tabs_s[k]) for k in TABLE_KEYS]
    casts = [_cast_specs(src, layer, steps, step_of) for src in ffn_w32]
    in_specs += [c[0] for c in casts]
    out_shape = (
        jax.ShapeDtypeStruct((B, T, D), F32),
        jax.ShapeDtypeStruct((depth, B, POOL_CTX, POOL_WIDTH), F32),
        jax.ShapeDtypeStruct((depth, B, RET_HEADS, RET_HEAD_DIM, RET_HEAD_DIM), F32),
        jax.ShapeDtypeStruct(state_pool_t.shape, F32),
        jax.ShapeDtypeStruct(state_ret.shape, F32),
        jax.ShapeDtypeStruct(xs2d.shape, F32),
    ) + tuple(c[2] for c in casts)
    out_specs = (
        pl.BlockSpec((1, tile, D), lambda b, t: (b, t, 0)),
        pl.BlockSpec((None, 1, POOL_CTX, POOL_WIDTH), lambda b, t: (layer, b, 0, 0)),
        pl.BlockSpec((None, 1, RET_HEADS, RET_HEAD_DIM, RET_HEAD_DIM),
                     lambda b, t: (layer, b, 0, 0, 0)),
        pl.BlockSpec((None, POOL_CTX, nb, POOL_WIDTH), lambda b, t: (layer, 0, step_of(b, t), 0)),
        pl.BlockSpec((None, nb, RET_HEADS, RET_HEAD_DIM, RET_HEAD_DIM),
                     lambda b, t: (layer, step_of(b, t), 0, 0, 0)),
        pl.BlockSpec((rows_s, D), lambda b, t: (step_of(b, t), 0)),
    ) + tuple(c[1] for c in casts)
    body = functools.partial(_mixer_kernel, layer=layer, tile=tile, sub=sub, nb=nb, seq=seq)
    kernel_fn, alias_specs, aliases, alias_args = _stacked_outputs(body, len(in_specs), prev)
    sample_tile = pltpu.VMEM((rows_s, RET_WIDTH), F32)
    row_major = pltpu.VMEM((tile, RET_WIDTH), BF16)
    key_tiles = pltpu.VMEM((tile // PROMPT_CHUNK, RET_WIDTH, PROMPT_CHUNK), BF16)
    scratch = [
        pltpu.VMEM((POOL_PAD + tile, POOL_WIDTH), F32),
        pltpu.VMEM((sub + rows_s, D_MODEL), BF16),
    ] + [row_major, row_major, key_tiles, key_tiles, row_major] + [
        pltpu.VMEM((tile, RET_WIDTH), F32),
        pltpu.VMEM((tile + rows_s, D_MODEL), BF16),
        pltpu.VMEM((nb, POOL_PAD + seq, POOL_WIDTH), F32),
    ] + [sample_tile] * 7
    return pl.pallas_call(
        kernel_fn,
        grid=(B, n_t),
        in_specs=in_specs + alias_specs,
        out_specs=out_specs,
        out_shape=out_shape,
        scratch_shapes=scratch,
        input_output_aliases=aliases,
        compiler_params=pltpu.CompilerParams(
            dimension_semantics=("arbitrary", "arbitrary"),
            vmem_limit_bytes=VMEM_LIMIT_BYTES,
        ),
        name="mixer",
    )(x, xs2d, state_pool_t, state_ret, w["g1"], big["w_in"], w["gn_g"], w["gn_b"], big["w_out"],
      *[tabs[k] for k in TABLE_KEYS], *[tabs_s[k] for k in TABLE_KEYS],
      *ffn_w32, *alias_args)


def _ffn_kernel(xp_ref, xs_ref, g2_ref, wup_ref, wdown_ref, fg_ref, *rest, layer, final_norm):
    n_cast = (len(rest) - 2) // 2
    op_ref, os_ref = rest[n_cast:n_cast + 2]
    for src, dst in zip(rest[:n_cast], rest[n_cast + 2:]):
        dst[...] = src[...].astype(BF16)
    n_p = xp_ref.shape[0]
    xp, xs = xp_ref[...], xs_ref[...]
    g2 = g2_ref[layer:layer + 1, :]
    h = jnp.concatenate([_rmsnorm(xp, g2).astype(BF16), _rmsnorm(xs, g2).astype(BF16)], axis=0)
    acc = None
    for c in range(D_FF // FFN_CHUNK):
        cols = slice(c * FFN_CHUNK, (c + 1) * FFN_CHUNK)
        a = jnp.square(jnp.maximum(_dot(h, wup_ref[:, cols]), 0.0)).astype(BF16)
        d = _dot(a, wdown_ref[cols, :])
        acc = d if acc is None else acc + d
    yp, ys = xp + acc[:n_p], xs + acc[n_p:]
    if final_norm:
        yp, ys = _rmsnorm(yp, fg_ref[...]), _rmsnorm(ys, fg_ref[...])
    op_ref[...] = yp
    os_ref[...] = ys


def _ffn(xp2d, xs2d, w, big, layer, final_g, mixer_w32):
    final_norm = not mixer_w32
    n, d = xp2d.shape
    tile = FFN_TILE
    steps = n // tile
    tile_s = xs2d.shape[0] // steps
    assert n % tile == 0 and xs2d.shape[0] == steps * tile_s and tile_s % (2 * SUBLANES) == 0
    const2 = lambda i: (0, 0)
    casts = [_cast_specs(src, layer + 1, steps, lambda i: i) for src in mixer_w32]
    return pl.pallas_call(
        functools.partial(_ffn_kernel, layer=layer, final_norm=final_norm),
        grid=(steps,),
        in_specs=[
            pl.BlockSpec((tile, d), lambda i: (i, 0)),
            pl.BlockSpec((tile_s, d), lambda i: (i, 0)),
            _whole_spec(w["g2"]),
            _whole_spec(big["w_up"]),
            _whole_spec(big["w_down"]),
            pl.BlockSpec((1, d), const2),
        ] + [c[0] for c in casts],
        out_specs=(pl.BlockSpec((tile, d), lambda i: (i, 0)),
                   pl.BlockSpec((tile_s, d), lambda i: (i, 0))) + tuple(c[1] for c in casts),
        out_shape=(jax.ShapeDtypeStruct((n, d), F32),
                   jax.ShapeDtypeStruct(xs2d.shape, F32)) + tuple(c[2] for c in casts),
        compiler_params=pltpu.CompilerParams(
            dimension_semantics=("arbitrary",),
            vmem_limit_bytes=VMEM_LIMIT_BYTES),
        name="ffn",
    )(xp2d, xs2d, w["g2"], big["w_up"], big["w_down"], final_g, *mixer_w32)


LOG_GAMMA = tuple(math.log(1.0 - 2.0 ** (-5.0 - h)) for h in range(RET_HEADS))
KEY_SCALE = RET_HEAD_DIM ** -0.5


def _fill_phases(cos_ref, sin_ref, *, first_row, pos_base, chunk, wrap):
    half = RET_HEAD_DIM // 2
    shape = cos_ref.shape
    lane = lax.broadcasted_iota(jnp.int32, shape, 1)
    row = first_row + lax.broadcasted_iota(jnp.int32, shape, 0)
    inv = jnp.exp((lane & (half - 1)).astype(F32) * (-math.log(ROPE_BASE) / half))
    pos = pos_base + ((row & (chunk - 1)) if wrap else row)
    ang = pos.astype(F32) * inv
    sin = jnp.sin(ang)
    cos_ref[...] = jnp.cos(ang)
    sin_ref[...] = jnp.where(lane < half, -sin, sin)


def _fill_decays(qd_ref, kd_ref, cd_ref, dmask_ref, *, chunk):
    assert chunk & (chunk - 1) == 0
    shift = chunk.bit_length() - 1
    n = qd_ref.shape[0]
    i = (lax.broadcasted_iota(jnp.int32, (n, RET_HEAD_DIM), 0) & (chunk - 1)).astype(F32)
    ri = lax.broadcasted_iota(jnp.int32, (n, n), 0)
    ci = lax.broadcasted_iota(jnp.int32, (n, n), 1)
    diff = ((ri & (chunk - 1)) - (ci & (chunk - 1))).astype(F32)
    same_chunk = (ri >> shift) == (ci >> shift)
    for hh in range(RET_HEADS):
        ls, lg = _head(hh), LOG_GAMMA[hh]
        qd_ref[:, ls] = jnp.exp(lg * (i + 1.0))
        kd_ref[:, ls] = jnp.exp(lg * (chunk - 1.0 - i)) * KEY_SCALE
        cd_ref[:, ls] = jnp.full((1, RET_HEAD_DIM), math.exp(lg * chunk), F32)
        dmask_ref[hh] = jnp.where(same_chunk & (diff >= 0),
                                  jnp.exp(lg * jnp.maximum(diff, 0.0)), 0.0) * KEY_SCALE


def _prepare_kernel(win32_ref, wout32_ref, poolw_ref, pscale_ref, win16_ref, wout16_ref,
                    *tables, chunk_s, depth):
    step = pl.program_id(0)
    win16_ref[...] = win32_ref[...].astype(BF16)

    for l in range(depth):
        @pl.when(step == 2 * l)
        def _(l=l):
            for g in range(len(POOL_WINDOWS)):
                rows = slice(g * POOL_GROUP_WIDTH, (g + 1) * POOL_GROUP_WIDTH)
                group_w = poolw_ref[l, g] * pscale_ref[l:l + 1, rows]
                wout16_ref[rows, :] = _dot(group_w.astype(BF16),
                                           wout32_ref[rows, :].astype(BF16)).astype(BF16)

    @pl.when(step % 2 == 1)
    def _():
        wout16_ref[...] = wout32_ref[...].astype(BF16)

    n = len(TABLE_KEYS)
    prompt, sample = tables[:n], tables[n:]
    _fill_phases(*prompt[:2], first_row=step * prompt[0].shape[0], pos_base=0,
                 chunk=PROMPT_CHUNK, wrap=False)

    @pl.when(step == 0)
    def _():
        _fill_decays(*prompt[2:], chunk=PROMPT_CHUNK)
        _fill_phases(*sample[:2], first_row=0, pos_base=PAST_LEN, chunk=chunk_s, wrap=True)
        _fill_decays(*sample[2:], chunk=chunk_s)


def _prepare(w_in, w_out, pool_w, pool_scale, n_pos, chunk_s, seqs_per_step):
    depth = w_out.shape[0]
    assert POOL_WIDTH * 2 == w_out.shape[1]
    steps = depth * 2
    win_specs = _cast_specs(w_in, 0, steps, lambda i: i)
    wout_block = pl.BlockSpec((None, POOL_WIDTH, D_MODEL), lambda i: (i // 2, i % 2, 0))

    def table_shapes(rows_pos, rows_dec):
        return {"cos": (rows_pos, RET_HEAD_DIM), "sin": (rows_pos, RET_HEAD_DIM),
                "qd": (rows_dec, RET_WIDTH), "kd": (rows_dec, RET_WIDTH),
                "cd": (1, RET_WIDTH), "dmask": (RET_HEADS, rows_dec, rows_dec)}

    rows_s = chunk_s * seqs_per_step
    shapes = [table_shapes(n_pos, PROMPT_CHUNK)[k] for k in TABLE_KEYS]
    shapes += [table_shapes(rows_s, rows_s)[k] for k in TABLE_KEYS]
    assert n_pos % (steps * SUBLANES) == 0
    table_specs = [pl.BlockSpec(sh, functools.partial(lambda nd, i: (0,) * nd, len(sh)))
                   for sh in shapes]
    table_specs[:2] = [pl.BlockSpec((n_pos // steps, RET_HEAD_DIM), lambda i: (i, 0))] * 2
    outs = pl.pallas_call(
        functools.partial(_prepare_kernel, chunk_s=chunk_s, depth=depth),
        grid=(steps,),
        in_specs=[win_specs[0], wout_block, _whole_spec(pool_w), _whole_spec(pool_scale)],
        out_specs=(win_specs[1], wout_block) + tuple(table_specs),
        out_shape=(win_specs[2], jax.ShapeDtypeStruct(w_out.shape, BF16)) + tuple(
            jax.ShapeDtypeStruct(sh, F32) for sh in shapes),
        compiler_params=pltpu.CompilerParams(
            dimension_semantics=("arbitrary",),
            vmem_limit_bytes=VMEM_LIMIT_BYTES),
        name="prepare",
    )(w_in, w_out, pool_w, pool_scale)
    n = len(TABLE_KEYS)
    return (outs[0], outs[1], dict(zip(TABLE_KEYS, outs[2:2 + n])),
            dict(zip(TABLE_KEYS, outs[2 + n:])))


def kernel(x_prompt, x_sample, state_pool, state_ret, norm1_g, w_in, pool_w, pool_scale,
           gn_g, gn_b, w_out, norm2_g, w_up, w_down, final_g):
    depth = w_in.shape[0]
    B, T, D = x_prompt.shape
    NB, TS, _ = x_sample.shape
    seqs_per_step = NB // (B * (T // PROMPT_TILE))

    w_in16, w_out16, tabs_p, tabs_s = _prepare(w_in, w_out, pool_w, pool_scale, T, TS,
                                               seqs_per_step)

    state_pool_t = jnp.transpose(state_pool, (0, 2, 1, 3))
    yp = x_prompt
    ys = x_sample.reshape(NB * TS, D)
    fg = final_g.reshape(1, D)
    w = {"g1": norm1_g, "gn_g": gn_g, "gn_b": gn_b, "g2": norm2_g}
    big = {"w_in": w_in16, "w_out": w_out16}
    states = ()
    for l in range(depth):
        x1p, *states, x1s, big["w_up"], big["w_down"] = _mixer(
            yp, ys, state_pool_t, state_ret, w, big, l, tabs_p, tabs_s, states, (w_up, w_down))
        nxt = (w_in,) if l + 1 < depth else ()
        yp, ys, *nxt16 = _ffn(x1p.reshape(B * T, D), x1s, w, big, l, fg, nxt)
        if nxt16:
            big["w_in"], = nxt16
        yp = yp.reshape(B, T, D)
    pool_p, ret_p, pool_s, ret_s = states
    return (yp, ys.reshape(NB, TS, D), pool_p, ret_p, jnp.transpose(pool_s, (0, 2, 1, 3)), ret_s)
```

```python
import functools
import math

import jax
import jax.numpy as jnp
from jax import lax
from jax.experimental import pallas as pl
from jax.experimental.pallas import tpu as pltpu

D_MODEL = 1024
POOL_WIDTH = D_MODEL // 2
POOL_WINDOWS = (2, 4, 8, 16)
POOL_GROUP_WIDTH = POOL_WIDTH // len(POOL_WINDOWS)
POOL_CTX = max(POOL_WINDOWS) - 1
POOL_PAD = POOL_CTX + 1
RET_WIDTH = D_MODEL - POOL_WIDTH
RET_HEADS = 4
RET_HEAD_DIM = RET_WIDTH // RET_HEADS
ROPE_BASE = 10000.0
D_FF = 4 * D_MODEL
RMS_EPS = 1e-6
GN_EPS = 1e-5
PAST_LEN = 16384

SUBLANES = 8
VMEM_LIMIT_BYTES = 60 * 1024 * 1024

PROMPT_TILE = 1024
PROMPT_SUBS = (384, 640)
PROMPT_CHUNK = 128
RET_STAGE_LAG = 3
PROJ_COLS = 256
FFN_TILE = 512
FFN_CHUNK = 1024

TABLE_KEYS = ("cos", "sin", "qd", "kd", "cd", "dmask")

F32 = jnp.float32
BF16 = jnp.bfloat16


def _dot(a, b):
    return jnp.dot(a, b, preferred_element_type=F32)


def _dot_nt(a, b):
    return lax.dot_general(a, b, (((1,), (1,)), ((), ())), preferred_element_type=F32)


def _dot_tn(a, b):
    return lax.dot_general(a, b, (((0,), (0,)), ((), ())), preferred_element_type=F32)


def _rmsnorm(x, g):
    ms = jnp.mean(x * x, axis=-1, keepdims=True)
    return x * lax.rsqrt(ms + RMS_EPS) * g


def _rotate(x, cos, sin_signed):
    return x * cos + pltpu.roll(x, RET_HEAD_DIM // 2, 1) * sin_signed


def _head_norm_gate(o, gate, gn_g, gn_b):
    mu = jnp.mean(o, axis=-1, keepdims=True)
    d = o - mu
    var = jnp.mean(d * d, axis=-1, keepdims=True)
    on = d * lax.rsqrt(var + GN_EPS) * gn_g + gn_b
    return gate * jax.nn.sigmoid(gate) * on


def _head(hh):
    return slice(hh * RET_HEAD_DIM, (hh + 1) * RET_HEAD_DIM)


def _whole_spec(arr):
    zeros = (0,) * arr.ndim
    return pl.BlockSpec(arr.shape, lambda *_: zeros)


def _cast_specs(src, layer, steps, step_of):
    _, rows, cols = src.shape
    assert rows % (steps * 2 * SUBLANES) == 0
    blk = rows // steps
    in_spec = pl.BlockSpec((None, blk, cols), lambda *g: (layer, step_of(*g), 0))
    out_spec = pl.BlockSpec((blk, cols), lambda *g: (step_of(*g), 0))
    return in_spec, out_spec, jax.ShapeDtypeStruct((rows, cols), BF16)


def _stacked_outputs(body, n_in, prev):
    prev = tuple(prev)

    def kernel(*refs):
        body(*refs[:n_in], *refs[n_in + len(prev):])

    specs = [pl.BlockSpec(memory_space=pl.ANY)] * len(prev)
    aliases = {n_in + k: 1 + k for k in range(len(prev))}
    return kernel, specs, aliases, prev


def _interleave(major, minor):
    out, taken = [], 0
    for n, item in enumerate(major):
        out.append(item)
        want = (n + 1) * len(minor) // len(major)
        out.extend(minor[taken:want])
        taken = want
    return out


def _mixer_kernel(x_ref, xs_ref, ctx_ref, s_in_ref,
                  g1_ref, win_ref, gng_ref, gnb_ref, wout_ref,
                  cos_ref, sin_ref, qd_ref, kd_ref, cd_ref, dmask_ref,
                  coss_ref, sins_ref, qds_ref, kds_ref, cds_ref, dmasks_ref,
                  wup32_ref, wdown32_ref,
                  x1_ref, pool_out_ref, s_ref, pools_out_ref, ss_out_ref, x1s_ref,
                  wup16_ref, wdown16_ref,
                  ubuf, h_scr, qb_scr, qdec_scr, kbt_scr, kdect_scr, vb_scr, g_scr, mix_scr,
                  ue, qraw_s, kraw_s, v_s, g_s, qdec_s, kdec_s, o_s,
                  *, layer, tile, subs, nb, seq):
    t = pl.program_id(1)
    this_layer = slice(layer, layer + 1)
    P, R = POOL_WIDTH, RET_WIDTH
    rows_s = nb * seq
    last_r0 = tile - subs[-1]

    def cast_items(src, dst, pieces):
        blk = src.shape[0] // pieces

        def piece(n):
            def run():
                dst[n * blk:(n + 1) * blk, :] = src[n * blk:(n + 1) * blk, :].astype(BF16)
            return run

        return [piece(n) for n in range(pieces)]

    @pl.when(t == 0)
    def _():
        ubuf[0:POOL_PAD, :] = jnp.zeros((POOL_PAD, POOL_WIDTH), F32)
        s_ref[...] = jnp.zeros_like(s_ref)

    def project_items(r0, sub):
        rows = slice(r0, r0 + sub)
        n_rows = sub + rows_s if r0 == 0 else sub

        def norm():
            h_scr[0:sub, :] = _rmsnorm(x_ref[0, rows, :], g1_ref[this_layer, :]).astype(BF16)
            if r0 == 0:
                h_scr[sub:n_rows, :] = _rmsnorm(xs_ref[...], g1_ref[this_layer, :]).astype(BF16)

        def column_group(lo, keep, put_sample):
            def half(off):
                def run():
                    z = _dot(h_scr[0:n_rows, :], win_ref[:, lo + off:lo + off + PROJ_COLS])
                    keep(z[0:sub], off)
                    if r0 == 0:
                        put_sample(z[sub:n_rows], off)
                return run
            return [half(off) for off in range(0, R, PROJ_COLS)]

        def keep_rows(dst, dst_rows, dtype):
            def run(z, off):
                dst[dst_rows, off:off + PROJ_COLS] = z.astype(dtype)
            return run

        def keep_rotated(decay_ref, plain, decayed, transposed):
            def run(z, off):
                for c in range(sub // PROMPT_CHUNK):
                    zr = slice(c * PROMPT_CHUNK, (c + 1) * PROMPT_CHUNK)
                    tr = slice(r0 + c * PROMPT_CHUNK, r0 + (c + 1) * PROMPT_CHUNK)
                    for lo in range(0, PROJ_COLS, RET_HEAD_DIM):
                        ls = slice(off + lo, off + lo + RET_HEAD_DIM)
                        rot = _rotate(z[zr, lo:lo + RET_HEAD_DIM], cos_ref[tr, :], sin_ref[tr, :])
                        dec = rot * decay_ref[:, ls]
                        if transposed:
                            ci = (r0 + c * PROMPT_CHUNK) // PROMPT_CHUNK
                            plain[ci, ls, :] = rot.T.astype(BF16)
                            decayed[ci, ls, :] = dec.T.astype(BF16)
                        else:
                            plain[tr, ls] = rot.astype(BF16)
                            decayed[tr, ls] = dec.astype(BF16)
            return run

        def put_u(z, off):
            ue[:, POOL_PAD:POOL_PAD + seq, off:off + PROJ_COLS] = z.reshape(nb, seq, PROJ_COLS)

        def put(dst):
            def run(z, off):
                dst[:, off:off + PROJ_COLS] = z
            return run

        u_rows = slice(POOL_PAD + r0, POOL_PAD + r0 + sub)
        return ([norm]
                + column_group(0, keep_rows(ubuf, u_rows, F32), put_u)
                + column_group(P, keep_rotated(qd_ref, qb_scr, qdec_scr, False),
                               put(qraw_s))
                + column_group(P + R, keep_rotated(kd_ref, kbt_scr, kdect_scr, True),
                               put(kraw_s))
                + column_group(P + 2 * R, keep_rows(vb_scr, rows, BF16), put(v_s))
                + column_group(P + 3 * R, keep_rows(g_scr, rows, F32), put(g_s)))

    def mix_items(r0, sub):
        base = POOL_PAD + r0
        items = []

        def pool_group(gi, win):
            def run():
                sl = slice(gi * POOL_GROUP_WIDTH, (gi + 1) * POOL_GROUP_WIDTH)
                pos = t * tile + r0 + lax.broadcasted_iota(jnp.int32, (sub, 1), 0)
                s = ubuf[base - POOL_PAD:base + sub, sl]
                cur = s[POOL_PAD:]
                shift = 1
                while shift < win:
                    s = s + pltpu.roll(s, shift, 0)
                    shift *= 2
                s = s[POOL_PAD:]
                cnt = jnp.minimum(pos + 1, win).astype(F32)
                mix_scr[r0:r0 + sub, sl] = (s / cnt - cur).astype(BF16)
            return run

        live = {}

        def scores(c, hh):
            def run():
                rows = slice(r0 + c * PROMPT_CHUNK, r0 + (c + 1) * PROMPT_CHUNK)
                ls = _head(hh)
                ci = (r0 + c * PROMPT_CHUNK) // PROMPT_CHUNK
                sc = _dot(qb_scr[rows, ls], kbt_scr[ci, ls, :]) * dmask_ref[hh]
                live[(c, hh)] = jnp.concatenate([sc.astype(BF16), qdec_scr[rows, ls]], axis=1)
            return run

        def outputs(c, hh):
            def run():
                rows = slice(r0 + c * PROMPT_CHUNK, r0 + (c + 1) * PROMPT_CHUNK)
                ls = _head(hh)
                lhs = live.pop((c, hh))
                vb = vb_scr[rows, ls]
                state = s_ref[0, hh]
                rhs = jnp.concatenate([vb, state.astype(BF16)], axis=0)
                live[(c, hh, "o")] = _dot(lhs, rhs)
                ci = (r0 + c * PROMPT_CHUNK) // PROMPT_CHUNK
                s_ref[0, hh] = state * cd_ref[:, ls] + _dot(kdect_scr[ci, ls, :], vb)
            return run

        def normalise(c, hh):
            def run():
                rows = slice(r0 + c * PROMPT_CHUNK, r0 + (c + 1) * PROMPT_CHUNK)
                ls = _head(hh)
                o = live.pop((c, hh, "o"))
                ret = _head_norm_gate(o, g_scr[rows, ls], gng_ref[this_layer, ls], gnb_ref[this_layer, ls])
                mix_scr[rows, P + hh * RET_HEAD_DIM:P + (hh + 1) * RET_HEAD_DIM] = (
                    ret.astype(BF16))
            return run

        def out_project(lo, width):
            def run():
                n_rows = sub + rows_s if r0 == last_r0 else sub
                y = _dot(mix_scr[r0:r0 + n_rows, :], wout_ref[:, lo:lo + width])
                x1_ref[0, r0:r0 + sub, lo:lo + width] = (
                    x_ref[0, r0:r0 + sub, lo:lo + width] + y[0:sub])
                if r0 == last_r0:
                    x1s_ref[:, lo:lo + width] = xs_ref[:, lo:lo + width] + y[sub:n_rows]
            return run

        heads = [(c, hh) for c in range(sub // PROMPT_CHUNK) for hh in range(RET_HEADS)]
        stages = tuple((n * RET_STAGE_LAG, stage)
                       for n, stage in enumerate((scores, outputs, normalise)))
        pools = [pool_group(gi, win) for gi, win in enumerate(POOL_WINDOWS)]
        for n in range(len(heads) + stages[-1][0]):
            for lag, stage in stages:
                if 0 <= n - lag < len(heads):
                    items.append(stage(*heads[n - lag]))
            if n % 2 == 1 and pools:
                items.append(pools.pop(0))
        items.extend(pools)
        return items, [out_project(lo, PROJ_COLS) for lo in range(0, D_MODEL, PROJ_COLS)]

    def sample_items():
        srows = slice(tile, tile + rows_s)

        def stage_context():
            for b in range(nb):
                ue[b, 1:POOL_PAD, :] = ctx_ref[:, b, :]

        def pool_group(gi, win):
            def run():
                sl = slice(gi * POOL_GROUP_WIDTH, (gi + 1) * POOL_GROUP_WIDTH)
                pos = PAST_LEN + lax.broadcasted_iota(jnp.int32, (1, seq, 1), 1)
                cur = ue[:, POOL_PAD:POOL_PAD + seq, sl]
                s = cur
                for j in range(1, win):
                    s = s + ue[:, POOL_PAD - j:POOL_PAD - j + seq, sl]
                cnt = jnp.minimum(pos + 1, win).astype(F32)
                pooled = (s / cnt - cur).reshape(rows_s, POOL_GROUP_WIDTH)
                mix_scr[srows, sl] = pooled.astype(BF16)
            return run

        def new_context():
            for b in range(nb):
                pools_out_ref[:, b, :] = ue[b, seq + 1:seq + POOL_PAD, :]

        def scores(hh):
            def run():
                ls = _head(hh)
                qr = _rotate(qraw_s[:, ls], coss_ref[...], sins_ref[...])
                kr = _rotate(kraw_s[:, ls], coss_ref[...], sins_ref[...])
                sc = _dot_nt(qr.astype(BF16), kr.astype(BF16)) * dmasks_ref[hh]
                o_s[:, ls] = _dot(sc.astype(BF16), v_s[:, ls].astype(BF16))
                qdec_s[:, ls] = qr * qds_ref[:, ls]
                kdec_s[:, ls] = kr * kds_ref[:, ls]
            return run

        def state(b):
            def run():
                r = slice(b * seq, (b + 1) * seq)
                for hh in range(RET_HEADS):
                    ls = _head(hh)
                    st = s_in_ref[b, hh]
                    vb = v_s[r, ls].astype(BF16)
                    o_s[r, ls] = o_s[r, ls] + _dot(qdec_s[r, ls].astype(BF16), st.astype(BF16))
                    ss_out_ref[b, hh] = (st * cds_ref[:, ls]
                                         + _dot_tn(kdec_s[r, ls].astype(BF16), vb))
            return run

        def normalise(hh):
            def run():
                ls = _head(hh)
                ret = _head_norm_gate(o_s[:, ls], g_s[:, ls], gng_ref[this_layer, ls], gnb_ref[this_layer, ls])
                mix_scr[srows, P + hh * RET_HEAD_DIM:P + (hh + 1) * RET_HEAD_DIM] = (
                    ret.astype(BF16))
            return run

        return ([stage_context]
                + [pool_group(gi, win) for gi, win in enumerate(POOL_WINDOWS)]
                + [new_context]
                + [scores(hh) for hh in range(RET_HEADS)]
                + [state(b) for b in range(nb)]
                + [normalise(hh) for hh in range(RET_HEADS)])

    starts = [sum(subs[:n]) for n in range(len(subs))]
    first = project_items(0, subs[0])
    middle, closing = [], []
    for n, (r0, sub) in enumerate(zip(starts, subs)):
        nxt = project_items(starts[n + 1], subs[n + 1]) if n + 1 < len(subs) else []
        items, out_proj = mix_items(r0, sub)
        if r0 == last_r0:
            middle += _interleave(items, sample_items())
            closing = out_proj
        else:
            middle += _interleave(items + out_proj, nxt)
    side = cast_items(wup32_ref, wup16_ref, 4) + cast_items(wdown32_ref, wdown16_ref, 4)
    for item in first + _interleave(middle, side) + closing:
        item()

    pool_out_ref[0] = ubuf[tile + 1:tile + POOL_PAD, :]
    ubuf[0:POOL_PAD, :] = ubuf[tile:tile + POOL_PAD, :]


def _mixer(x, xs2d, state_pool_t, state_ret, w, big, layer, tabs, tabs_s, prev, ffn_w32):
    B, T, D = x.shape
    depth, n_seq = state_ret.shape[:2]
    assert w["g1"].shape[0] == depth
    tile, subs = PROMPT_TILE, PROMPT_SUBS
    n_t = T // tile
    steps = B * n_t
    nb = n_seq // steps
    seq = xs2d.shape[0] // n_seq
    rows_s = nb * seq
    assert T % tile == 0 and sum(subs) == tile and all(n % PROMPT_CHUNK == 0 for n in subs)
    assert n_seq == nb * steps and seq == SUBLANES and rows_s % (2 * SUBLANES) == 0
    step_of = lambda b, t: b * n_t + t
    tab_spec = pl.BlockSpec((tile, RET_HEAD_DIM), lambda b, t: (t, 0))
    in_specs = [
        pl.BlockSpec((1, tile, D), lambda b, t: (b, t, 0)),
        pl.BlockSpec((rows_s, D), lambda b, t: (step_of(b, t), 0)),
        pl.BlockSpec((None, POOL_CTX, nb, POOL_WIDTH), lambda b, t: (layer, 0, step_of(b, t), 0)),
        pl.BlockSpec((None, nb, RET_HEADS, RET_HEAD_DIM, RET_HEAD_DIM),
                     lambda b, t: (layer, step_of(b, t), 0, 0, 0)),
        _whole_spec(w["g1"]),
        _whole_spec(big["w_in"]),
        _whole_spec(w["gn_g"]),
        _whole_spec(w["gn_b"]),
        pl.BlockSpec((None,) + big["w_out"].shape[1:], lambda b, t: (layer, 0, 0)),
        tab_spec, tab_spec,
        _whole_spec(tabs["qd"]), _whole_spec(tabs["kd"]), _whole_spec(tabs["cd"]),
        _whole_spec(tabs["dmask"]),
    ] + [_whole_spec(tabs_s[k]) for k in TABLE_KEYS]
    casts = [_cast_specs(src, layer, steps, step_of) for src in ffn_w32]
    in_specs += [c[0] for c in casts]
    out_shape = (
        jax.ShapeDtypeStruct((B, T, D), F32),
        jax.ShapeDtypeStruct((depth, B, POOL_CTX, POOL_WIDTH), F32),
        jax.ShapeDtypeStruct((depth, B, RET_HEADS, RET_HEAD_DIM, RET_HEAD_DIM), F32),
        jax.ShapeDtypeStruct(state_pool_t.shape, F32),
        jax.ShapeDtypeStruct(state_ret.shape, F32),
        jax.ShapeDtypeStruct(xs2d.shape, F32),
    ) + tuple(c[2] for c in casts)
    out_specs = (
        pl.BlockSpec((1, tile, D), lambda b, t: (b, t, 0)),
        pl.BlockSpec((None, 1, POOL_CTX, POOL_WIDTH), lambda b, t: (layer, b, 0, 0)),
        pl.BlockSpec((None, 1, RET_HEADS, RET_HEAD_DIM, RET_HEAD_DIM),
                     lambda b, t: (layer, b, 0, 0, 0)),
        pl.BlockSpec((None, POOL_CTX, nb, POOL_WIDTH), lambda b, t: (layer, 0, step_of(b, t), 0)),
        pl.BlockSpec((None, nb, RET_HEADS, RET_HEAD_DIM, RET_HEAD_DIM),
                     lambda b, t: (layer, step_of(b, t), 0, 0, 0)),
        pl.BlockSpec((rows_s, D), lambda b, t: (step_of(b, t), 0)),
    ) + tuple(c[1] for c in casts)
    body = functools.partial(_mixer_kernel, layer=layer, tile=tile, subs=subs, nb=nb, seq=seq)
    kernel_fn, alias_specs, aliases, alias_args = _stacked_outputs(body, len(in_specs), prev)
    sample_tile = pltpu.VMEM((rows_s, RET_WIDTH), F32)
    row_major = pltpu.VMEM((tile, RET_WIDTH), BF16)
    key_tiles = pltpu.VMEM((tile // PROMPT_CHUNK, RET_WIDTH, PROMPT_CHUNK), BF16)
    scratch = [
        pltpu.VMEM((POOL_PAD + tile, POOL_WIDTH), F32),
        pltpu.VMEM((max(subs[0] + rows_s, *subs), D_MODEL), BF16),
    ] + [row_major, row_major, key_tiles, key_tiles, row_major] + [
        pltpu.VMEM((tile, RET_WIDTH), F32),
        pltpu.VMEM((tile + rows_s, D_MODEL), BF16),
        pltpu.VMEM((nb, POOL_PAD + seq, POOL_WIDTH), F32),
    ] + [sample_tile] * 7
    return pl.pallas_call(
        kernel_fn,
        grid=(B, n_t),
        in_specs=in_specs + alias_specs,
        out_specs=out_specs,
        out_shape=out_shape,
        scratch_shapes=scratch,
        input_output_aliases=aliases,
        compiler_params=pltpu.CompilerParams(
            dimension_semantics=("arbitrary", "arbitrary"),
            vmem_limit_bytes=VMEM_LIMIT_BYTES,
        ),
        name="mixer",
    )(x, xs2d, state_pool_t, state_ret, w["g1"], big["w_in"], w["gn_g"], w["gn_b"], big["w_out"],
      *[tabs[k] for k in TABLE_KEYS], *[tabs_s[k] for k in TABLE_KEYS],
      *ffn_w32, *alias_args)


def _ffn_kernel(xp_ref, xs_ref, g2_ref, wup_ref, wdown_ref, fg_ref, *rest, layer, final_norm):
    n_cast = (len(rest) - 2) // 2
    op_ref, os_ref = rest[n_cast:n_cast + 2]
    for src, dst in zip(rest[:n_cast], rest[n_cast + 2:]):
        dst[...] = src[...].astype(BF16)
    n_p = xp_ref.shape[0]
    xp, xs = xp_ref[...], xs_ref[...]
    g2 = g2_ref[layer:layer + 1, :]
    h = jnp.concatenate([_rmsnorm(xp, g2).astype(BF16), _rmsnorm(xs, g2).astype(BF16)], axis=0)
    acc = None
    for c in range(D_FF // FFN_CHUNK):
        cols = slice(c * FFN_CHUNK, (c + 1) * FFN_CHUNK)
        a = jnp.square(jnp.maximum(_dot(h, wup_ref[:, cols]), 0.0)).astype(BF16)
        d = _dot(a, wdown_ref[cols, :])
        acc = d if acc is None else acc + d
    yp, ys = xp + acc[:n_p], xs + acc[n_p:]
    if final_norm:
        yp, ys = _rmsnorm(yp, fg_ref[...]), _rmsnorm(ys, fg_ref[...])
    op_ref[...] = yp
    os_ref[...] = ys


def _ffn(xp2d, xs2d, w, big, layer, final_g, mixer_w32):
    final_norm = not mixer_w32
    n, d = xp2d.shape
    tile = FFN_TILE
    steps = n // tile
    tile_s = xs2d.shape[0] // steps
    assert n % tile == 0 and xs2d.shape[0] == steps * tile_s and tile_s % (2 * SUBLANES) == 0
    const2 = lambda i: (0, 0)
    casts = [_cast_specs(src, layer + 1, steps, lambda i: i) for src in mixer_w32]
    return pl.pallas_call(
        functools.partial(_ffn_kernel, layer=layer, final_norm=final_norm),
        grid=(steps,),
        in_specs=[
            pl.BlockSpec((tile, d), lambda i: (i, 0)),
            pl.BlockSpec((tile_s, d), lambda i: (i, 0)),
            _whole_spec(w["g2"]),
            _whole_spec(big["w_up"]),
            _whole_spec(big["w_down"]),
            pl.BlockSpec((1, d), const2),
        ] + [c[0] for c in casts],
        out_specs=(pl.BlockSpec((tile, d), lambda i: (i, 0)),
                   pl.BlockSpec((tile_s, d), lambda i: (i, 0))) + tuple(c[1] for c in casts),
        out_shape=(jax.ShapeDtypeStruct((n, d), F32),
                   jax.ShapeDtypeStruct(xs2d.shape, F32)) + tuple(c[2] for c in casts),
        compiler_params=pltpu.CompilerParams(
            dimension_semantics=("arbitrary",),
            vmem_limit_bytes=VMEM_LIMIT_BYTES),
        name="ffn",
    )(xp2d, xs2d, w["g2"], big["w_up"], big["w_down"], final_g, *mixer_w32)


LOG_GAMMA = tuple(math.log(1.0 - 2.0 ** (-5.0 - h)) for h in range(RET_HEADS))
KEY_SCALE = RET_HEAD_DIM ** -0.5


def _fill_phases(cos_ref, sin_ref, *, first_row, pos_base, chunk, wrap):
    half = RET_HEAD_DIM // 2
    shape = cos_ref.shape
    lane = lax.broadcasted_iota(jnp.int32, shape, 1)
    row = first_row + lax.broadcasted_iota(jnp.int32, shape, 0)
    inv = jnp.exp((lane & (half - 1)).astype(F32) * (-math.log(ROPE_BASE) / half))
    pos = pos_base + ((row & (chunk - 1)) if wrap else row)
    ang = pos.astype(F32) * inv
    sin = jnp.sin(ang)
    cos_ref[...] = jnp.cos(ang)
    sin_ref[...] = jnp.where(lane < half, -sin, sin)


def _fill_decays(qd_ref, kd_ref, cd_ref, dmask_ref, *, chunk):
    assert chunk & (chunk - 1) == 0
    shift = chunk.bit_length() - 1
    n = qd_ref.shape[0]
    i = (lax.broadcasted_iota(jnp.int32, (n, RET_HEAD_DIM), 0) & (chunk - 1)).astype(F32)
    ri = lax.broadcasted_iota(jnp.int32, (n, n), 0)
    ci = lax.broadcasted_iota(jnp.int32, (n, n), 1)
    diff = ((ri & (chunk - 1)) - (ci & (chunk - 1))).astype(F32)
    same_chunk = (ri >> shift) == (ci >> shift)
    for hh in range(RET_HEADS):
        ls, lg = _head(hh), LOG_GAMMA[hh]
        qd_ref[:, ls] = jnp.exp(lg * (i + 1.0))
        kd_ref[:, ls] = jnp.exp(lg * (chunk - 1.0 - i)) * KEY_SCALE
        cd_ref[:, ls] = jnp.full((1, RET_HEAD_DIM), math.exp(lg * chunk), F32)
        dmask_ref[hh] = jnp.where(same_chunk & (diff >= 0),
                                  jnp.exp(lg * jnp.maximum(diff, 0.0)), 0.0) * KEY_SCALE


def _prepare_kernel(win32_ref, wout32_ref, poolw_ref, pscale_ref, win16_ref, wout16_ref,
                    *tables, chunk_s, depth):
    step = pl.program_id(0)
    win16_ref[...] = win32_ref[...].astype(BF16)

    for l in range(depth):
        @pl.when(step == 2 * l)
        def _(l=l):
            for g in range(len(POOL_WINDOWS)):
                rows = slice(g * POOL_GROUP_WIDTH, (g + 1) * POOL_GROUP_WIDTH)
                group_w = poolw_ref[l, g] * pscale_ref[l:l + 1, rows]
                wout16_ref[rows, :] = _dot(group_w.astype(BF16),
                                           wout32_ref[rows, :].astype(BF16)).astype(BF16)

    @pl.when(step % 2 == 1)
    def _():
        wout16_ref[...] = wout32_ref[...].astype(BF16)

    n = len(TABLE_KEYS)
    prompt, sample = tables[:n], tables[n:]
    _fill_phases(*prompt[:2], first_row=step * prompt[0].shape[0], pos_base=0,
                 chunk=PROMPT_CHUNK, wrap=False)

    @pl.when(step == 0)
    def _():
        _fill_decays(*prompt[2:], chunk=PROMPT_CHUNK)
        _fill_phases(*sample[:2], first_row=0, pos_base=PAST_LEN, chunk=chunk_s, wrap=True)
        _fill_decays(*sample[2:], chunk=chunk_s)


def _prepare(w_in, w_out, pool_w, pool_scale, n_pos, chunk_s, seqs_per_step):
    depth = w_out.shape[0]
    assert POOL_WIDTH * 2 == w_out.shape[1]
    steps = depth * 2
    win_specs = _cast_specs(w_in, 0, steps, lambda i: i)
    wout_block = pl.BlockSpec((None, POOL_WIDTH, D_MODEL), lambda i: (i // 2, i % 2, 0))

    def table_shapes(rows_pos, rows_dec):
        return {"cos": (rows_pos, RET_HEAD_DIM), "sin": (rows_pos, RET_HEAD_DIM),
                "qd": (rows_dec, RET_WIDTH), "kd": (rows_dec, RET_WIDTH),
                "cd": (1, RET_WIDTH), "dmask": (RET_HEADS, rows_dec, rows_dec)}

    rows_s = chunk_s * seqs_per_step
    shapes = [table_shapes(n_pos, PROMPT_CHUNK)[k] for k in TABLE_KEYS]
    shapes += [table_shapes(rows_s, rows_s)[k] for k in TABLE_KEYS]
    assert n_pos % (steps * SUBLANES) == 0
    table_specs = [pl.BlockSpec(sh, functools.partial(lambda nd, i: (0,) * nd, len(sh)))
                   for sh in shapes]
    table_specs[:2] = [pl.BlockSpec((n_pos // steps, RET_HEAD_DIM), lambda i: (i, 0))] * 2
    outs = pl.pallas_call(
        functools.partial(_prepare_kernel, chunk_s=chunk_s, depth=depth),
        grid=(steps,),
        in_specs=[win_specs[0], wout_block, _whole_spec(pool_w), _whole_spec(pool_scale)],
        out_specs=(win_specs[1], wout_block) + tuple(table_specs),
        out_shape=(win_specs[2], jax.ShapeDtypeStruct(w_out.shape, BF16)) + tuple(
            jax.ShapeDtypeStruct(sh, F32) for sh in shapes),
        compiler_params=pltpu.CompilerParams(
            dimension_semantics=("arbitrary",),
            vmem_limit_bytes=VMEM_LIMIT_BYTES),
        name="prepare",
    )(w_in, w_out, pool_w, pool_scale)
    n = len(TABLE_KEYS)
    return (outs[0], outs[1], dict(zip(TABLE_KEYS, outs[2:2 + n])),
            dict(zip(TABLE_KEYS, outs[2 + n:])))


def kernel(x_prompt, x_sample, state_pool, state_ret, norm1_g, w_in, pool_w, pool_scale,
           gn_g, gn_b, w_out, norm2_g, w_up, w_down, final_g):
    depth = w_in.shape[0]
    B, T, D = x_prompt.shape
    NB, TS, _ = x_sample.shape
    seqs_per_step = NB // (B * (T // PROMPT_TILE))

    w_in16, w_out16, tabs_p, tabs_s = _prepare(w_in, w_out, pool_w, pool_scale, T, TS,
                                               seqs_per_step)

    state_pool_t = jnp.transpose(state_pool, (0, 2, 1, 3))
    yp = x_prompt
    ys = x_sample.reshape(NB * TS, D)
    fg = final_g.reshape(1, D)
    w = {"g1": norm1_g, "gn_g": gn_g, "gn_b": gn_b, "g2": norm2_g}
    big = {"w_in": w_in16, "w_out": w_out16}
    states = ()
    for l in range(depth):
        x1p, *states, x1s, big["w_up"], big["w_down"] = _mixer(
            yp, ys, state_pool_t, state_ret, w, big, l, tabs_p, tabs_s, states, (w_up, w_down))
        nxt = (w_in,) if l + 1 < depth else ()
        yp, ys, *nxt16 = _ffn(x1p.reshape(B * T, D), x1s, w, big, l, fg, nxt)
        if nxt16:
            big["w_in"], = nxt16
        yp = yp.reshape(B, T, D)
    pool_p, ret_p, pool_s, ret_s = states
    return (yp, ys.reshape(NB, TS, D), pool_p, ret_p, jnp.transpose(pool_s, (0, 2, 1, 3)), ret_s)
```

```python
import functools
import math

import jax
import jax.numpy as jnp
from jax import lax
from jax.experimental import pallas as pl
from jax.experimental.pallas import tpu as pltpu

D_MODEL = 1024
POOL_WIDTH = D_MODEL // 2
POOL_WINDOWS = (2, 4, 8, 16)
POOL_GROUP_WIDTH = POOL_WIDTH // len(POOL_WINDOWS)
POOL_CTX = max(POOL_WINDOWS) - 1
POOL_PAD = POOL_CTX + 1
RET_WIDTH = D_MODEL - POOL_WIDTH
RET_HEADS = 4
RET_HEAD_DIM = RET_WIDTH // RET_HEADS
ROPE_BASE = 10000.0
D_FF = 4 * D_MODEL
RMS_EPS = 1e-6
GN_EPS = 1e-5
PAST_LEN = 16384

SUBLANES = 8
VMEM_LIMIT_BYTES = 60 * 1024 * 1024

PROMPT_TILE = 1024
PROMPT_SUBS = (384, 640)
PROMPT_CHUNK = 128
RET_STAGE_LAG = 3
PROJ_COLS = 256
FFN_TILE = 512
FFN_CHUNK = 1024

TABLE_KEYS = ("cos", "sin", "qd", "kd", "cd", "dmask")

F32 = jnp.float32
BF16 = jnp.bfloat16


def _dot(a, b):
    return jnp.dot(a, b, preferred_element_type=F32)


def _dot_nt(a, b):
    return lax.dot_general(a, b, (((1,), (1,)), ((), ())), preferred_element_type=F32)


def _dot_tn(a, b):
    return lax.dot_general(a, b, (((0,), (0,)), ((), ())), preferred_element_type=F32)


def _rmsnorm(x, g):
    ms = jnp.mean(x * x, axis=-1, keepdims=True)
    return x * lax.rsqrt(ms + RMS_EPS) * g


def _rotate(x, cos, sin_signed):
    return x * cos + pltpu.roll(x, RET_HEAD_DIM // 2, 1) * sin_signed


def _head_norm_gate(o, gate, gn_g, gn_b):
    mu = jnp.mean(o, axis=-1, keepdims=True)
    d = o - mu
    var = jnp.mean(d * d, axis=-1, keepdims=True)
    on = d * lax.rsqrt(var + GN_EPS) * gn_g + gn_b
    return gate * jax.nn.sigmoid(gate) * on


def _head(hh):
    return slice(hh * RET_HEAD_DIM, (hh + 1) * RET_HEAD_DIM)


def _whole_spec(arr):
    zeros = (0,) * arr.ndim
    return pl.BlockSpec(arr.shape, lambda *_: zeros)


def _cast_specs(src, layer, steps, step_of):
    _, rows, cols = src.shape
    assert rows % (steps * 2 * SUBLANES) == 0
    blk = rows // steps
    in_spec = pl.BlockSpec((None, blk, cols), lambda *g: (layer, step_of(*g), 0))
    out_spec = pl.BlockSpec((blk, cols), lambda *g: (step_of(*g), 0))
    return in_spec, out_spec, jax.ShapeDtypeStruct((rows, cols), BF16)


def _stacked_outputs(body, n_in, prev):
    prev = tuple(prev)

    def kernel(*refs):
        body(*refs[:n_in], *refs[n_in + len(prev):])

    specs = [pl.BlockSpec(memory_space=pl.ANY)] * len(prev)
    aliases = {n_in + k: 1 + k for k in range(len(prev))}
    return kernel, specs, aliases, prev


def _interleave(major, minor):
    out, taken = [], 0
    for n, item in enumerate(major):
        out.append(item)
        want = (n + 1) * len(minor) // len(major)
        out.extend(minor[taken:want])
        taken = want
    return out


def _mixer_kernel(x_ref, xs_ref, ctx_ref, s_in_ref,
                  g1_ref, win_ref, gng_ref, gnb_ref, wout_ref,
                  cos_ref, sin_ref, qd_ref, kd_ref, cd_ref, dmask_ref,
                  coss_ref, sins_ref, qds_ref, kds_ref, cds_ref, dmasks_ref,
                  wup32_ref, wdown32_ref,
                  x1_ref, pool_out_ref, s_ref, pools_out_ref, ss_out_ref, x1s_ref,
                  wup16_ref, wdown16_ref,
                  ubuf, h_scr, qdec_scr, kbt_scr, kdect_scr, vb_scr, g_scr, mix_scr,
                  ue, qraw_s, kraw_s, v_s, g_s, qdec_s, kdec_s, o_s,
                  *, layer, tile, subs, nb, seq):
    t = pl.program_id(1)
    this_layer = slice(layer, layer + 1)
    P, R = POOL_WIDTH, RET_WIDTH
    rows_s = nb * seq
    last_r0 = tile - subs[-1]

    def cast_items(src, dst, pieces):
        blk = src.shape[0] // pieces

        def piece(n):
            def run():
                dst[n * blk:(n + 1) * blk, :] = src[n * blk:(n + 1) * blk, :].astype(BF16)
            return run

        return [piece(n) for n in range(pieces)]

    @pl.when(t == 0)
    def _():
        ubuf[0:POOL_PAD, :] = jnp.zeros((POOL_PAD, POOL_WIDTH), F32)
        s_ref[...] = jnp.zeros_like(s_ref)

    def project_items(r0, sub):
        rows = slice(r0, r0 + sub)
        n_rows = sub + rows_s if r0 == 0 else sub

        def norm():
            h_scr[0:sub, :] = _rmsnorm(x_ref[0, rows, :], g1_ref[this_layer, :]).astype(BF16)
            if r0 == 0:
                h_scr[sub:n_rows, :] = _rmsnorm(xs_ref[...], g1_ref[this_layer, :]).astype(BF16)

        def column_group(lo, keep, put_sample):
            def half(off):
                def run():
                    z = _dot(h_scr[0:n_rows, :], win_ref[:, lo + off:lo + off + PROJ_COLS])
                    keep(z[0:sub], off)
                    if r0 == 0:
                        put_sample(z[sub:n_rows], off)
                return run
            return [half(off) for off in range(0, R, PROJ_COLS)]

        def keep_rows(dst, dst_rows, dtype):
            def run(z, off):
                dst[dst_rows, off:off + PROJ_COLS] = z.astype(dtype)
            return run

        def keep_rotated(decay_ref, plain, decayed, transposed):
            def run(z, off):
                for c in range(sub // PROMPT_CHUNK):
                    zr = slice(c * PROMPT_CHUNK, (c + 1) * PROMPT_CHUNK)
                    tr = slice(r0 + c * PROMPT_CHUNK, r0 + (c + 1) * PROMPT_CHUNK)
                    for lo in range(0, PROJ_COLS, RET_HEAD_DIM):
                        ls = slice(off + lo, off + lo + RET_HEAD_DIM)
                        rot = _rotate(z[zr, lo:lo + RET_HEAD_DIM], cos_ref[tr, :], sin_ref[tr, :])
                        dec = rot * decay_ref[:, ls]
                        if transposed:
                            ci = (r0 + c * PROMPT_CHUNK) // PROMPT_CHUNK
                            grown = dec * (1.0 / cd_ref[:, ls])
                            plain[ci, ls, :] = grown.T.astype(BF16)
                            decayed[ci, ls, :] = dec.T.astype(BF16)
                        else:
                            decayed[tr, ls] = dec.astype(BF16)
            return run

        def put_u(z, off):
            ue[:, POOL_PAD:POOL_PAD + seq, off:off + PROJ_COLS] = z.reshape(nb, seq, PROJ_COLS)

        def put(dst):
            def run(z, off):
                dst[:, off:off + PROJ_COLS] = z
            return run

        u_rows = slice(POOL_PAD + r0, POOL_PAD + r0 + sub)
        return ([norm]
                + column_group(0, keep_rows(ubuf, u_rows, F32), put_u)
                + column_group(P, keep_rotated(qd_ref, None, qdec_scr, False),
                               put(qraw_s))
                + column_group(P + R, keep_rotated(kd_ref, kbt_scr, kdect_scr, True),
                               put(kraw_s))
                + column_group(P + 2 * R, keep_rows(vb_scr, rows, BF16), put(v_s))
                + column_group(P + 3 * R, keep_rows(g_scr, rows, F32), put(g_s)))

    def mix_items(r0, sub):
        base = POOL_PAD + r0
        items = []

        def pool_group(gi, win):
            def run():
                sl = slice(gi * POOL_GROUP_WIDTH, (gi + 1) * POOL_GROUP_WIDTH)
                pos = t * tile + r0 + lax.broadcasted_iota(jnp.int32, (sub, 1), 0)
                s = ubuf[base - POOL_PAD:base + sub, sl]
                cur = s[POOL_PAD:]
                shift = 1
                while shift < win:
                    s = s + pltpu.roll(s, shift, 0)
                    shift *= 2
                s = s[POOL_PAD:]
                cnt = jnp.minimum(pos + 1, win).astype(F32)
                mix_scr[r0:r0 + sub, sl] = (s / cnt - cur).astype(BF16)
            return run

        live = {}

        def scores(c, hh):
            def run():
                rows = slice(r0 + c * PROMPT_CHUNK, r0 + (c + 1) * PROMPT_CHUNK)
                ls = _head(hh)
                ci = (r0 + c * PROMPT_CHUNK) // PROMPT_CHUNK
                sc = _dot(qdec_scr[rows, ls], kbt_scr[ci, ls, :])
                sc = jnp.where(dmask_ref[hh] > 0.0, sc, 0.0)
                live[(c, hh)] = jnp.concatenate([sc.astype(BF16), qdec_scr[rows, ls]], axis=1)
            return run

        def outputs(c, hh):
            def run():
                rows = slice(r0 + c * PROMPT_CHUNK, r0 + (c + 1) * PROMPT_CHUNK)
                ls = _head(hh)
                lhs = live.pop((c, hh))
                vb = vb_scr[rows, ls]
                state = s_ref[0, hh]
                rhs = jnp.concatenate([vb, state.astype(BF16)], axis=0)
                live[(c, hh, "o")] = _dot(lhs, rhs)
                ci = (r0 + c * PROMPT_CHUNK) // PROMPT_CHUNK
                s_ref[0, hh] = state * cd_ref[:, ls] + _dot(kdect_scr[ci, ls, :], vb)
            return run

        def normalise(c, hh):
            def run():
                rows = slice(r0 + c * PROMPT_CHUNK, r0 + (c + 1) * PROMPT_CHUNK)
                ls = _head(hh)
                o = live.pop((c, hh, "o"))
                ret = _head_norm_gate(o, g_scr[rows, ls], gng_ref[this_layer, ls], gnb_ref[this_layer, ls])
                mix_scr[rows, P + hh * RET_HEAD_DIM:P + (hh + 1) * RET_HEAD_DIM] = (
                    ret.astype(BF16))
            return run

        def out_project(lo, width):
            def run():
                n_rows = sub + rows_s if r0 == last_r0 else sub
                y = _dot(mix_scr[r0:r0 + n_rows, :], wout_ref[:, lo:lo + width])
                x1_ref[0, r0:r0 + sub, lo:lo + width] = (
                    x_ref[0, r0:r0 + sub, lo:lo + width] + y[0:sub])
                if r0 == last_r0:
                    x1s_ref[:, lo:lo + width] = xs_ref[:, lo:lo + width] + y[sub:n_rows]
            return run

        heads = [(c, hh) for c in range(sub // PROMPT_CHUNK) for hh in range(RET_HEADS)]
        stages = tuple((n * RET_STAGE_LAG, stage)
                       for n, stage in enumerate((scores, outputs, normalise)))
        pools = [pool_group(gi, win) for gi, win in enumerate(POOL_WINDOWS)]
        for n in range(len(heads) + stages[-1][0]):
            for lag, stage in stages:
                if 0 <= n - lag < len(heads):
                    items.append(stage(*heads[n - lag]))
            if n % 2 == 1 and pools:
                items.append(pools.pop(0))
        items.extend(pools)
        return items, [out_project(lo, PROJ_COLS) for lo in range(0, D_MODEL, PROJ_COLS)]

    def sample_items():
        srows = slice(tile, tile + rows_s)

        def stage_context():
            for b in range(nb):
                ue[b, 1:POOL_PAD, :] = ctx_ref[:, b, :]

        def pool_group(gi, win):
            def run():
                sl = slice(gi * POOL_GROUP_WIDTH, (gi + 1) * POOL_GROUP_WIDTH)
                pos = PAST_LEN + lax.broadcasted_iota(jnp.int32, (1, seq, 1), 1)
                cur = ue[:, POOL_PAD:POOL_PAD + seq, sl]
                s = cur
                for j in range(1, win):
                    s = s + ue[:, POOL_PAD - j:POOL_PAD - j + seq, sl]
                cnt = jnp.minimum(pos + 1, win).astype(F32)
                pooled = (s / cnt - cur).reshape(rows_s, POOL_GROUP_WIDTH)
                mix_scr[srows, sl] = pooled.astype(BF16)
            return run

        def new_context():
            for b in range(nb):
                pools_out_ref[:, b, :] = ue[b, seq + 1:seq + POOL_PAD, :]

        def scores(hh):
            def run():
                ls = _head(hh)
                qr = _rotate(qraw_s[:, ls], coss_ref[...], sins_ref[...])
                kr = _rotate(kraw_s[:, ls], coss_ref[...], sins_ref[...])
                sc = _dot_nt(qr.astype(BF16), kr.astype(BF16)) * dmasks_ref[hh]
                o_s[:, ls] = _dot(sc.astype(BF16), v_s[:, ls].astype(BF16))
                qdec_s[:, ls] = qr * qds_ref[:, ls]
                kdec_s[:, ls] = kr * kds_ref[:, ls]
            return run

        def state(b):
            def run():
                r = slice(b * seq, (b + 1) * seq)
                for hh in range(RET_HEADS):
                    ls = _head(hh)
                    st = s_in_ref[b, hh]
                    vb = v_s[r, ls].astype(BF16)
                    o_s[r, ls] = o_s[r, ls] + _dot(qdec_s[r, ls].astype(BF16), st.astype(BF16))
                    ss_out_ref[b, hh] = (st * cds_ref[:, ls]
                                         + _dot_tn(kdec_s[r, ls].astype(BF16), vb))
            return run

        def normalise(hh):
            def run():
                ls = _head(hh)
                ret = _head_norm_gate(o_s[:, ls], g_s[:, ls], gng_ref[this_layer, ls], gnb_ref[this_layer, ls])
                mix_scr[srows, P + hh * RET_HEAD_DIM:P + (hh + 1) * RET_HEAD_DIM] = (
                    ret.astype(BF16))
            return run

        return ([stage_context]
                + [pool_group(gi, win) for gi, win in enumerate(POOL_WINDOWS)]
                + [new_context]
                + [scores(hh) for hh in range(RET_HEADS)]
                + [state(b) for b in range(nb)]
                + [normalise(hh) for hh in range(RET_HEADS)])

    starts = [sum(subs[:n]) for n in range(len(subs))]
    first = project_items(0, subs[0])
    middle, closing = [], []
    for n, (r0, sub) in enumerate(zip(starts, subs)):
        nxt = project_items(starts[n + 1], subs[n + 1]) if n + 1 < len(subs) else []
        items, out_proj = mix_items(r0, sub)
        if r0 == last_r0:
            middle += _interleave(items, sample_items())
            closing = out_proj
        else:
            middle += _interleave(items + out_proj, nxt)
    side = cast_items(wup32_ref, wup16_ref, 4) + cast_items(wdown32_ref, wdown16_ref, 4)
    for item in first + _interleave(middle, side) + closing:
        item()

    pool_out_ref[0] = ubuf[tile + 1:tile + POOL_PAD, :]
    ubuf[0:POOL_PAD, :] = ubuf[tile:tile + POOL_PAD, :]


def _mixer(x, xs2d, state_pool_t, state_ret, w, big, layer, tabs, tabs_s, prev, ffn_w32):
    B, T, D = x.shape
    depth, n_seq = state_ret.shape[:2]
    assert w["g1"].shape[0] == depth
    tile, subs = PROMPT_TILE, PROMPT_SUBS
    n_t = T // tile
    steps = B * n_t
    nb = n_seq // steps
    seq = xs2d.shape[0] // n_seq
    rows_s = nb * seq
    assert T % tile == 0 and sum(subs) == tile and all(n % PROMPT_CHUNK == 0 for n in subs)
    assert n_seq == nb * steps and seq == SUBLANES and rows_s % (2 * SUBLANES) == 0
    step_of = lambda b, t: b * n_t + t
    tab_spec = pl.BlockSpec((tile, RET_HEAD_DIM), lambda b, t: (t, 0))
    in_specs = [
        pl.BlockSpec((1, tile, D), lambda b, t: (b, t, 0)),
        pl.BlockSpec((rows_s, D), lambda b, t: (step_of(b, t), 0)),
        pl.BlockSpec((None, POOL_CTX, nb, POOL_WIDTH), lambda b, t: (layer, 0, step_of(b, t), 0)),
        pl.BlockSpec((None, nb, RET_HEADS, RET_HEAD_DIM, RET_HEAD_DIM),
                     lambda b, t: (layer, step_of(b, t), 0, 0, 0)),
        _whole_spec(w["g1"]),
        _whole_spec(big["w_in"]),
        _whole_spec(w["gn_g"]),
        _whole_spec(w["gn_b"]),
        pl.BlockSpec((None,) + big["w_out"].shape[1:], lambda b, t: (layer, 0, 0)),
        tab_spec, tab_spec,
        _whole_spec(tabs["qd"]), _whole_spec(tabs["kd"]), _whole_spec(tabs["cd"]),
        _whole_spec(tabs["dmask"]),
    ] + [_whole_spec(tabs_s[k]) for k in TABLE_KEYS]
    casts = [_cast_specs(src, layer, steps, step_of) for src in ffn_w32]
    in_specs += [c[0] for c in casts]
    out_shape = (
        jax.ShapeDtypeStruct((B, T, D), F32),
        jax.ShapeDtypeStruct((depth, B, POOL_CTX, POOL_WIDTH), F32),
        jax.ShapeDtypeStruct((depth, B, RET_HEADS, RET_HEAD_DIM, RET_HEAD_DIM), F32),
        jax.ShapeDtypeStruct(state_pool_t.shape, F32),
        jax.ShapeDtypeStruct(state_ret.shape, F32),
        jax.ShapeDtypeStruct(xs2d.shape, F32),
    ) + tuple(c[2] for c in casts)
    out_specs = (
        pl.BlockSpec((1, tile, D), lambda b, t: (b, t, 0)),
        pl.BlockSpec((None, 1, POOL_CTX, POOL_WIDTH), lambda b, t: (layer, b, 0, 0)),
        pl.BlockSpec((None, 1, RET_HEADS, RET_HEAD_DIM, RET_HEAD_DIM),
                     lambda b, t: (layer, b, 0, 0, 0)),
        pl.BlockSpec((None, POOL_CTX, nb, POOL_WIDTH), lambda b, t: (layer, 0, step_of(b, t), 0)),
        pl.BlockSpec((None, nb, RET_HEADS, RET_HEAD_DIM, RET_HEAD_DIM),
                     lambda b, t: (layer, step_of(b, t), 0, 0, 0)),
        pl.BlockSpec((rows_s, D), lambda b, t: (step_of(b, t), 0)),
    ) + tuple(c[1] for c in casts)
    body = functools.partial(_mixer_kernel, layer=layer, tile=tile, subs=subs, nb=nb, seq=seq)
    kernel_fn, alias_specs, aliases, alias_args = _stacked_outputs(body, len(in_specs), prev)
    sample_tile = pltpu.VMEM((rows_s, RET_WIDTH), F32)
    row_major = pltpu.VMEM((tile, RET_WIDTH), BF16)
    key_tiles = pltpu.VMEM((tile // PROMPT_CHUNK, RET_WIDTH, PROMPT_CHUNK), BF16)
    scratch = [
        pltpu.VMEM((POOL_PAD + tile, POOL_WIDTH), F32),
        pltpu.VMEM((max(subs[0] + rows_s, *subs), D_MODEL), BF16),
    ] + [row_major, key_tiles, key_tiles, row_major] + [
        pltpu.VMEM((tile, RET_WIDTH), F32),
        pltpu.VMEM((tile + rows_s, D_MODEL), BF16),
        pltpu.VMEM((nb, POOL_PAD + seq, POOL_WIDTH), F32),
    ] + [sample_tile] * 7
    return pl.pallas_call(
        kernel_fn,
        grid=(B, n_t),
        in_specs=in_specs + alias_specs,
        out_specs=out_specs,
        out_shape=out_shape,
        scratch_shapes=scratch,
        input_output_aliases=aliases,
        compiler_params=pltpu.CompilerParams(
            dimension_semantics=("arbitrary", "arbitrary"),
            vmem_limit_bytes=VMEM_LIMIT_BYTES,
        ),
        name="mixer",
    )(x, xs2d, state_pool_t, state_ret, w["g1"], big["w_in"], w["gn_g"], w["gn_b"], big["w_out"],
      *[tabs[k] for k in TABLE_KEYS], *[tabs_s[k] for k in TABLE_KEYS],
      *ffn_w32, *alias_args)


def _ffn_kernel(xp_ref, xs_ref, g2_ref, wup_ref, wdown_ref, fg_ref, *rest, layer, final_norm):
    n_cast = (len(rest) - 2) // 2
    op_ref, os_ref = rest[n_cast:n_cast + 2]
    for src, dst in zip(rest[:n_cast], rest[n_cast + 2:]):
        dst[...] = src[...].astype(BF16)
    n_p = xp_ref.shape[0]
    xp, xs = xp_ref[...], xs_ref[...]
    g2 = g2_ref[layer:layer + 1, :]
    h = jnp.concatenate([_rmsnorm(xp, g2).astype(BF16), _rmsnorm(xs, g2).astype(BF16)], axis=0)
    acc = None
    for c in range(D_FF // FFN_CHUNK):
        cols = slice(c * FFN_CHUNK, (c + 1) * FFN_CHUNK)
        a = jnp.square(jnp.maximum(_dot(h, wup_ref[:, cols]), 0.0)).astype(BF16)
        d = _dot(a, wdown_ref[cols, :])
        acc = d if acc is None else acc + d
    yp, ys = xp + acc[:n_p], xs + acc[n_p:]
    if final_norm:
        yp, ys = _rmsnorm(yp, fg_ref[...]), _rmsnorm(ys, fg_ref[...])
    op_ref[...] = yp
    os_ref[...] = ys


def _ffn(xp2d, xs2d, w, big, layer, final_g, mixer_w32):
    final_norm = not mixer_w32
    n, d = xp2d.shape
    tile = FFN_TILE
    steps = n // tile
    tile_s = xs2d.shape[0] // steps
    assert n % tile == 0 and xs2d.shape[0] == steps * tile_s and tile_s % (2 * SUBLANES) == 0
    const2 = lambda i: (0, 0)
    casts = [_cast_specs(src, layer + 1, steps, lambda i: i) for src in mixer_w32]
    return pl.pallas_call(
        functools.partial(_ffn_kernel, layer=layer, final_norm=final_norm),
        grid=(steps,),
        in_specs=[
            pl.BlockSpec((tile, d), lambda i: (i, 0)),
            pl.BlockSpec((tile_s, d), lambda i: (i, 0)),
            _whole_spec(w["g2"]),
            _whole_spec(big["w_up"]),
            _whole_spec(big["w_down"]),
            pl.BlockSpec((1, d), const2),
        ] + [c[0] for c in casts],
        out_specs=(pl.BlockSpec((tile, d), lambda i: (i, 0)),
                   pl.BlockSpec((tile_s, d), lambda i: (i, 0))) + tuple(c[1] for c in casts),
        out_shape=(jax.ShapeDtypeStruct((n, d), F32),
                   jax.ShapeDtypeStruct(xs2d.shape, F32)) + tuple(c[2] for c in casts),
        compiler_params=pltpu.CompilerParams(
            dimension_semantics=("arbitrary",),
            vmem_limit_bytes=VMEM_LIMIT_BYTES),
        name="ffn",
    )(xp2d, xs2d, w["g2"], big["w_up"], big["w_down"], final_g, *mixer_w32)


LOG_GAMMA = tuple(math.log(1.0 - 2.0 ** (-5.0 - h)) for h in range(RET_HEADS))
KEY_SCALE = RET_HEAD_DIM ** -0.5


def _fill_phases(cos_ref, sin_ref, *, first_row, pos_base, chunk, wrap):
    half = RET_HEAD_DIM // 2
    shape = cos_ref.shape
    lane = lax.broadcasted_iota(jnp.int32, shape, 1)
    row = first_row + lax.broadcasted_iota(jnp.int32, shape, 0)
    inv = jnp.exp((lane & (half - 1)).astype(F32) * (-math.log(ROPE_BASE) / half))
    pos = pos_base + ((row & (chunk - 1)) if wrap else row)
    ang = pos.astype(F32) * inv
    sin = jnp.sin(ang)
    cos_ref[...] = jnp.cos(ang)
    sin_ref[...] = jnp.where(lane < half, -sin, sin)


def _fill_decays(qd_ref, kd_ref, cd_ref, dmask_ref, *, chunk):
    assert chunk & (chunk - 1) == 0
    shift = chunk.bit_length() - 1
    n = qd_ref.shape[0]
    i = (lax.broadcasted_iota(jnp.int32, (n, RET_HEAD_DIM), 0) & (chunk - 1)).astype(F32)
    ri = lax.broadcasted_iota(jnp.int32, (n, n), 0)
    ci = lax.broadcasted_iota(jnp.int32, (n, n), 1)
    diff = ((ri & (chunk - 1)) - (ci & (chunk - 1))).astype(F32)
    same_chunk = (ri >> shift) == (ci >> shift)
    for hh in range(RET_HEADS):
        ls, lg = _head(hh), LOG_GAMMA[hh]
        qd_ref[:, ls] = jnp.exp(lg * (i + 1.0))
        kd_ref[:, ls] = jnp.exp(lg * (chunk - 1.0 - i)) * KEY_SCALE
        cd_ref[:, ls] = jnp.full((1, RET_HEAD_DIM), math.exp(lg * chunk), F32)
        dmask_ref[hh] = jnp.where(same_chunk & (diff >= 0),
                                  jnp.exp(lg * jnp.maximum(diff, 0.0)), 0.0) * KEY_SCALE


def _prepare_kernel(win32_ref, wout32_ref, poolw_ref, pscale_ref, win16_ref, wout16_ref,
                    *tables, chunk_s, depth):
    step = pl.program_id(0)
    win16_ref[...] = win32_ref[...].astype(BF16)

    for l in range(depth):
        @pl.when(step == 2 * l)
        def _(l=l):
            for g in range(len(POOL_WINDOWS)):
                rows = slice(g * POOL_GROUP_WIDTH, (g + 1) * POOL_GROUP_WIDTH)
                group_w = poolw_ref[l, g] * pscale_ref[l:l + 1, rows]
                wout16_ref[rows, :] = _dot(group_w.astype(BF16),
                                           wout32_ref[rows, :].astype(BF16)).astype(BF16)

    @pl.when(step % 2 == 1)
    def _():
        wout16_ref[...] = wout32_ref[...].astype(BF16)

    n = len(TABLE_KEYS)
    prompt, sample = tables[:n], tables[n:]
    _fill_phases(*prompt[:2], first_row=step * prompt[0].shape[0], pos_base=0,
                 chunk=PROMPT_CHUNK, wrap=False)

    @pl.when(step == 0)
    def _():
        _fill_decays(*prompt[2:], chunk=PROMPT_CHUNK)
        _fill_phases(*sample[:2], first_row=0, pos_base=PAST_LEN, chunk=chunk_s, wrap=True)
        _fill_decays(*sample[2:], chunk=chunk_s)


def _prepare(w_in, w_out, pool_w, pool_scale, n_pos, chunk_s, seqs_per_step):
    depth = w_out.shape[0]
    assert POOL_WIDTH * 2 == w_out.shape[1]
    steps = depth * 2
    win_specs = _cast_specs(w_in, 0, steps, lambda i: i)
    wout_block = pl.BlockSpec((None, POOL_WIDTH, D_MODEL), lambda i: (i // 2, i % 2, 0))

    def table_shapes(rows_pos, rows_dec):
        return {"cos": (rows_pos, RET_HEAD_DIM), "sin": (rows_pos, RET_HEAD_DIM),
                "qd": (rows_dec, RET_WIDTH), "kd": (rows_dec, RET_WIDTH),
                "cd": (1, RET_WIDTH), "dmask": (RET_HEADS, rows_dec, rows_dec)}

    rows_s = chunk_s * seqs_per_step
    shapes = [table_shapes(n_pos, PROMPT_CHUNK)[k] for k in TABLE_KEYS]
    shapes += [table_shapes(rows_s, rows_s)[k] for k in TABLE_KEYS]
    assert n_pos % (steps * SUBLANES) == 0
    table_specs = [pl.BlockSpec(sh, functools.partial(lambda nd, i: (0,) * nd, len(sh)))
                   for sh in shapes]
    table_specs[:2] = [pl.BlockSpec((n_pos // steps, RET_HEAD_DIM), lambda i: (i, 0))] * 2
    outs = pl.pallas_call(
        functools.partial(_prepare_kernel, chunk_s=chunk_s, depth=depth),
        grid=(steps,),
        in_specs=[win_specs[0], wout_block, _whole_spec(pool_w), _whole_spec(pool_scale)],
        out_specs=(win_specs[1], wout_block) + tuple(table_specs),
        out_shape=(win_specs[2], jax.ShapeDtypeStruct(w_out.shape, BF16)) + tuple(
            jax.ShapeDtypeStruct(sh, F32) for sh in shapes),
        compiler_params=pltpu.CompilerParams(
            dimension_semantics=("arbitrary",),
            vmem_limit_bytes=VMEM_LIMIT_BYTES),
        name="prepare",
    )(w_in, w_out, pool_w, pool_scale)
    n = len(TABLE_KEYS)
    return (outs[0], outs[1], dict(zip(TABLE_KEYS, outs[2:2 + n])),
            dict(zip(TABLE_KEYS, outs[2 + n:])))


def kernel(x_prompt, x_sample, state_pool, state_ret, norm1_g, w_in, pool_w, pool_scale,
           gn_g, gn_b, w_out, norm2_g, w_up, w_down, final_g):
    depth = w_in.shape[0]
    B, T, D = x_prompt.shape
    NB, TS, _ = x_sample.shape
    seqs_per_step = NB // (B * (T // PROMPT_TILE))

    w_in16, w_out16, tabs_p, tabs_s = _prepare(w_in, w_out, pool_w, pool_scale, T, TS,
                                               seqs_per_step)

    state_pool_t = jnp.transpose(state_pool, (0, 2, 1, 3))
    yp = x_prompt
    ys = x_sample.reshape(NB * TS, D)
    fg = final_g.reshape(1, D)
    w = {"g1": norm1_g, "gn_g": gn_g, "gn_b": gn_b, "g2": norm2_g}
    big = {"w_in": w_in16, "w_out": w_out16}
    states = ()
    for l in range(depth):
        x1p, *states, x1s, big["w_up"], big["w_down"] = _mixer(
            yp, ys, state_pool_t, state_ret, w, big, l, tabs_p, tabs_s, states, (w_up, w_down))
        nxt = (w_in,) if l + 1 < depth else ()
        yp, ys, *nxt16 = _ffn(x1p.reshape(B * T, D), x1s, w, big, l, fg, nxt)
        if nxt16:
            big["w_in"], = nxt16
        yp = yp.reshape(B, T, D)
    pool_p, ret_p, pool_s, ret_s = states
    return (yp, ys.reshape(NB, TS, D), pool_p, ret_p, jnp.transpose(pool_s, (0, 2, 1, 3)), ret_s)
```
